```python
import math
import jax, jax.numpy as jnp
from jax import lax
import numpy as np

D_MODEL = 1024
BATCH = 2
SEQ = 16384
DEPTH = 2
DEC_BATCH = 8
DEC_SEQ = 8192
PAST_LEN = 128

EPS = 1e-6
ROPE_THETA = 10000.0
Q_BLOCK = 128

MLA_HEADS = 6
MLA_Q_LORA = 256
MLA_KV_LORA = 128
MLA_NOPE = 64
MLA_ROPE = 32
MLA_QK = MLA_NOPE + MLA_ROPE
MLA_V = 64
MLA_WIDTH = MLA_HEADS * MLA_V

SSD_HEADS = 6
SSD_HEAD_DIM = 64
SSD_INNER = SSD_HEADS * SSD_HEAD_DIM
SSD_GROUPS = 2
SSD_STATE = 128
SSD_CONV = 5
SSD_CONV_DIM = SSD_INNER + 2 * SSD_GROUPS * SSD_STATE
SSD_CHUNK = 128
SSD_WIDTH = SSD_INNER

DIFF_HEADS = 4
DIFF_QK = 32
DIFF_V = 2 * DIFF_QK
DIFF_WIDTH = DIFF_HEADS * DIFF_V

D_MIX = MLA_WIDTH + SSD_WIDTH + DIFF_WIDTH

N_EXPERTS = 16
EC_CAPACITY = 2
EXPERT_FF = 1024

OFF_KV = MLA_Q_LORA
OFF_KPE = OFF_KV + MLA_KV_LORA
OFF_Z = OFF_KPE + MLA_ROPE
OFF_XBC = OFF_Z + SSD_INNER
OFF_DT = OFF_XBC + SSD_CONV_DIM
OFF_DQ = OFF_DT + 2 * SSD_HEADS
OFF_DK = OFF_DQ + DIFF_HEADS * 2 * DIFF_QK
OFF_DV = OFF_DK + DIFF_HEADS * 2 * DIFF_QK
IN_COLS = OFF_DV + DIFF_WIDTH
IN_SPLITS = (OFF_KV, OFF_KPE, OFF_Z, OFF_XBC, OFF_DT, OFF_DQ, OFF_DK, OFF_DV)

kernel_name = 'hymba_mla_ssd_diff_ec_encoder'


def rmsnorm(x, g):
    xf = x.astype(jnp.float32)
    y = xf * lax.rsqrt(jnp.mean(xf * xf, axis=-1, keepdims=True) + EPS)
    return (y * g.astype(jnp.float32)).astype(x.dtype)


def rope_tables(seq, dim):
    inv = ROPE_THETA ** (-jnp.arange(0, dim, 2, dtype=jnp.float32) / dim)
    ang = jnp.arange(seq, dtype=jnp.float32)[:, None] * inv[None, :]
    return jnp.cos(ang), jnp.sin(ang)


def apply_rope(x, cos, sin):
    half = x.shape[-1] // 2
    shape = (x.shape[1],) + (1,) * (x.ndim - 3) + (half,)
    c = cos.reshape(shape)
    s = sin.reshape(shape)
    xf = x.astype(jnp.float32)
    x1, x2 = xf[..., :half], xf[..., half:]
    return jnp.concatenate([x1 * c - x2 * s, x2 * c + x1 * s], axis=-1).astype(x.dtype)


def sweep_query_blocks(block_fn, q):
    b, s = q.shape[:2]
    nb = s // Q_BLOCK
    qb = jnp.moveaxis(q.reshape((b, nb, Q_BLOCK) + q.shape[2:]), 1, 0)
    out = jnp.moveaxis(lax.map(block_fn, qb), 0, 1)
    return out.reshape((b, s) + out.shape[3:])


def mla_mixer(c_q, c_kv, k_pe, q_norm, kv_norm, w_uq, w_ukv, q_gain, k_gain, cos, sin):
    b, s, _ = c_q.shape
    q = jnp.einsum('bsr,rf->bsf', rmsnorm(c_q, q_norm), w_uq).reshape(b, s, MLA_HEADS, MLA_QK)
    kv = jnp.einsum('bsr,rf->bsf', rmsnorm(c_kv, kv_norm), w_ukv).reshape(b, s, MLA_HEADS, MLA_NOPE + MLA_V)
    k_nope, v = kv[..., :MLA_NOPE], kv[..., MLA_NOPE:]
    k_pe_h = jnp.broadcast_to(k_pe[:, :, None, :], (b, s, MLA_HEADS, MLA_ROPE))
    k = jnp.concatenate([k_nope, k_pe_h], axis=-1)
    q = rmsnorm(q, q_gain)
    k = rmsnorm(k, k_gain)
    q = jnp.concatenate([q[..., :MLA_NOPE], apply_rope(q[..., MLA_NOPE:], cos, sin)], axis=-1)
    k = jnp.concatenate([k[..., :MLA_NOPE], apply_rope(k[..., MLA_NOPE:], cos, sin)], axis=-1)
    scale = MLA_QK ** -0.5

    def block(qb):
        sc = jnp.einsum('bqhd,bkhd->bhqk', qb, k).astype(jnp.float32) * scale
        p = jax.nn.softmax(sc, axis=-1).astype(v.dtype)
        return jnp.einsum('bhqk,bkhd->bqhd', p, v)

    return sweep_query_blocks(block, q).reshape(b, s, MLA_WIDTH)


def centred_depthwise_conv(x, w, bias):
    pad = SSD_CONV // 2
    y = lax.conv_general_dilated(
        x, w[:, None, :].astype(x.dtype), window_strides=(1,), padding=[(pad, pad)],
        dimension_numbers=('NWC', 'WIO', 'NWC'), feature_group_count=x.shape[-1])
    return y + bias


def ssd_chunked_scan(x, a, bmat, cmat):
    b, s, h, p = x.shape
    n = bmat.shape[-1]
    nc = s // SSD_CHUNK
    x = x.reshape(b, nc, SSD_CHUNK, h, p)
    bmat = bmat.reshape(b, nc, SSD_CHUNK, h, n)
    cmat = cmat.reshape(b, nc, SSD_CHUNK, h, n)
    a = jnp.transpose(a.astype(jnp.float32).reshape(b, nc, SSD_CHUNK, h), (0, 3, 1, 2))
    a_cs = jnp.cumsum(a, axis=-1)
    tril = jnp.tril(jnp.ones((SSD_CHUNK, SSD_CHUNK), dtype=bool))
    seg = a_cs[..., :, None] - a_cs[..., None, :]
    decay_mat = jnp.exp(jnp.where(tril, seg, -jnp.inf)).astype(x.dtype)
    scores = jnp.einsum('bclhn,bcshn->bhcls', cmat, bmat) * decay_mat
    y_diag = jnp.einsum('bhcls,bcshp->bclhp', scores, x)
    decay_to_end = jnp.exp(a_cs[..., -1:] - a_cs).astype(x.dtype)
    states = jnp.einsum('bclhn,bhcl,bclhp->bchpn', bmat, decay_to_end, x)
    chunk_decay = jnp.exp(a_cs[..., -1]).astype(x.dtype)

    def step(carry, inp):
        st, dec = inp
        return carry * dec[..., None, None] + st, carry

    init = jnp.zeros((b, h, p, n), x.dtype)
    _, prev = lax.scan(step, init, (jnp.moveaxis(states, 1, 0), jnp.moveaxis(chunk_decay, 2, 0)))
    prev = jnp.moveaxis(prev, 0, 1)
    y_off = jnp.einsum('bclhn,bchpn,bhcl->bclhp', cmat, prev, jnp.exp(a_cs).astype(x.dtype))
    return (y_diag + y_off).reshape(b, s, h, p)


def ssd_mixer(z, xbc, dt_raw, conv_w, conv_b, dt_bias, a_log, d_skip, norm_w):
    b, s, _ = z.shape
    xbc = jax.nn.silu(centred_depthwise_conv(xbc, conv_w, conv_b))
    xs = xbc[..., :SSD_INNER].reshape(b, s, SSD_HEADS, SSD_HEAD_DIM)
    gn = SSD_GROUPS * SSD_STATE
    bmat = xbc[..., SSD_INNER:SSD_INNER + gn].reshape(b, s, SSD_GROUPS, SSD_STATE)
    cmat = xbc[..., SSD_INNER + gn:].reshape(b, s, SSD_GROUPS, SSD_STATE)
    hpg = SSD_HEADS // SSD_GROUPS
    bmat = jnp.repeat(bmat, hpg, axis=2)
    cmat = jnp.repeat(cmat, hpg, axis=2)
    dt = jax.nn.softplus((dt_raw.reshape(b, s, 2, SSD_HEADS) + dt_bias).astype(jnp.float32))
    a_dir = dt * (-jnp.exp(a_log.astype(jnp.float32)))
    dt = dt.astype(xs.dtype)
    y_fwd = ssd_chunked_scan(xs * dt[:, :, 0, :, None], a_dir[:, :, 0], bmat, cmat)
    y_bwd = jnp.flip(ssd_chunked_scan(jnp.flip(xs * dt[:, :, 1, :, None], axis=1),
                                      jnp.flip(a_dir[:, :, 1], axis=1),
                                      jnp.flip(bmat, axis=1), jnp.flip(cmat, axis=1)), axis=1)
    y = (y_fwd + y_bwd + xs * d_skip[:, None]).reshape(b, s, SSD_INNER)
    return rmsnorm(y * jax.nn.silu(z), norm_w)


def diff_mixer(q, k, v, q_gain, k_gain, lam, subln, lambda_init, cos, sin):
    b, s, _ = q.shape
    q = q.reshape(b, s, DIFF_HEADS, 2, DIFF_QK)
    k = k.reshape(b, s, DIFF_HEADS, 2, DIFF_QK)
    v = v.reshape(b, s, DIFF_HEADS, DIFF_V)
    q = apply_rope(rmsnorm(q, q_gain), cos, sin)
    k = apply_rope(rmsnorm(k, k_gain), cos, sin)
    lf = lam.astype(jnp.float32)
    lam_full = jnp.exp(jnp.sum(lf[0] * lf[1])) - jnp.exp(jnp.sum(lf[2] * lf[3])) + lambda_init
    scale = DIFF_QK ** -0.5

    def block(qb):
        sc = jnp.einsum('bqhmd,bkhmd->bhmqk', qb, k).astype(jnp.float32) * scale
        p = jax.nn.softmax(sc, axis=-1)
        w = (p[:, :, 0] - lam_full * p[:, :, 1]).astype(v.dtype)
        return jnp.einsum('bhqk,bkhd->bqhd', w, v)

    o = sweep_query_blocks(block, q)
    o = rmsnorm(o, subln) * (1.0 - lambda_init)
    return o.reshape(b, s, DIFF_WIDTH)


def expert_choice_ffn(h, w_router, w_gate, w_up, w_down):
    b, s, d = h.shape
    t = b * s
    cap = EC_CAPACITY * t // N_EXPERTS
    hf = h.reshape(t, d)
    affinity = jax.nn.softmax(jnp.einsum('td,de->te', hf, w_router).astype(jnp.float32), axis=-1)
    gate, idx = lax.top_k(affinity.T, cap)
    xe = hf[idx]
    hid = jax.nn.silu(jnp.einsum('ecd,edf->ecf', xe, w_gate)) * jnp.einsum('ecd,edf->ecf', xe, w_up)
    ye = jnp.einsum('ecf,efd->ecd', hid, w_down) * gate[..., None].astype(h.dtype)
    out = jnp.zeros_like(hf).at[idx.reshape(-1)].add(ye.reshape(-1, d))
    return out.reshape(b, s, d)


def encoder_trunk(x, attn_norm, w_in, mla_q_norm, mla_kv_norm, mla_w_uq, mla_w_ukv, mla_q_gain,
                  mla_k_gain, ssd_conv_w, ssd_conv_b, ssd_dt_bias, ssd_a_log, ssd_d, ssd_norm,
                  diff_q_gain, diff_k_gain, diff_lambda, diff_subln, w_out, ffn_norm, w_router,
                  w_gate, w_up, w_down):
    s = x.shape[1]
    cos, sin = rope_tables(s, MLA_ROPE)
    for l in range(DEPTH):
        h = rmsnorm(x, attn_norm[l])
        proj = jnp.einsum('bsd,df->bsf', h, w_in[l])
        c_q, c_kv, k_pe, z, xbc, dt_raw, dq, dk, dv = jnp.split(proj, IN_SPLITS, axis=-1)
        y_mla = mla_mixer(c_q, c_kv, k_pe, mla_q_norm[l], mla_kv_norm[l], mla_w_uq[l], mla_w_ukv[l],
                          mla_q_gain[l], mla_k_gain[l], cos, sin)
        y_ssd = ssd_mixer(z, xbc, dt_raw, ssd_conv_w[l], ssd_conv_b[l], ssd_dt_bias[l], ssd_a_log[l],
                          ssd_d[l], ssd_norm[l])
        lambda_init = 0.8 - 0.6 * math.exp(-0.3 * l)
        y_diff = diff_mixer(dq, dk, dv, diff_q_gain[l], diff_k_gain[l], diff_lambda[l], diff_subln[l],
                            lambda_init, cos, sin)
        mix = jnp.concatenate([y_mla, y_ssd, y_diff], axis=-1)
        x = x + jnp.einsum('bsf,fd->bsd', mix, w_out[l])
        x = x + expert_choice_ffn(rmsnorm(x, ffn_norm[l]), w_router[l], w_gate[l], w_up[l], w_down[l])
    return x


def setup_inputs(seed: int = 0) -> dict:
    key = jax.random.key(seed)
    ks = jax.random.split(key, 32)
    f32 = jnp.float32

    def dense(k, shape, fan_in):
        return jax.random.normal(k, shape, f32) * fan_in ** -0.5

    def gain(k, shape):
        return 1.0 + 0.02 * jax.random.normal(k, shape, f32)

    dt0 = jnp.exp(jax.random.uniform(ks[12], (DEPTH, 2, SSD_HEADS), f32, math.log(1e-3), math.log(1e-1)))
    return {
        'x_prompt': jax.random.normal(ks[0], (BATCH, SEQ, D_MODEL), f32),
        'x_sample': jax.random.normal(ks[1], (DEC_BATCH, DEC_SEQ, D_MODEL), f32),
        'attn_norm': gain(ks[2], (DEPTH, D_MODEL)),
        'w_in': dense(ks[3], (DEPTH, D_MODEL, IN_COLS), D_MODEL),
        'mla_q_norm': gain(ks[4], (DEPTH, MLA_Q_LORA)),
        'mla_kv_norm': gain(ks[5], (DEPTH, MLA_KV_LORA)),
        'mla_w_uq': dense(ks[6], (DEPTH, MLA_Q_LORA, MLA_HEADS * MLA_QK), MLA_Q_LORA),
        'mla_w_ukv': dense(ks[7], (DEPTH, MLA_KV_LORA, MLA_HEADS * (MLA_NOPE + MLA_V)), MLA_KV_LORA),
        'mla_q_gain': gain(ks[8], (DEPTH, MLA_QK)),
        'mla_k_gain': gain(ks[9], (DEPTH, MLA_QK)),
        'ssd_conv_w': dense(ks[10], (DEPTH, SSD_CONV, SSD_CONV_DIM), SSD_CONV),
        'ssd_conv_b': 0.02 * jax.random.normal(ks[11], (DEPTH, SSD_CONV_DIM), f32),
        'ssd_dt_bias': dt0 + jnp.log(-jnp.expm1(-dt0)),
        'ssd_a_log': jnp.log(jax.random.uniform(ks[13], (DEPTH, 2, SSD_HEADS), f32, 1.0, 16.0)),
        'ssd_d': gain(ks[14], (DEPTH, SSD_HEADS)),
        'ssd_norm': gain(ks[15], (DEPTH, SSD_INNER)),
        'diff_q_gain': gain(ks[16], (DEPTH, DIFF_QK)),
        'diff_k_gain': gain(ks[17], (DEPTH, DIFF_QK)),
        'diff_lambda': 0.1 * jax.random.normal(ks[18], (DEPTH, 4, DIFF_QK), f32),
        'diff_subln': gain(ks[19], (DEPTH, DIFF_V)),
        'w_out': dense(ks[20], (DEPTH, D_MIX, D_MODEL), D_MIX),
        'ffn_norm': gain(ks[21], (DEPTH, D_MODEL)),
        'w_router': dense(ks[22], (DEPTH, D_MODEL, N_EXPERTS), D_MODEL),
        'w_gate': dense(ks[23], (DEPTH, N_EXPERTS, D_MODEL, EXPERT_FF), D_MODEL),
        'w_up': dense(ks[24], (DEPTH, N_EXPERTS, D_MODEL, EXPERT_FF), D_MODEL),
        'w_down': dense(ks[25], (DEPTH, N_EXPERTS, EXPERT_FF, D_MODEL), EXPERT_FF),
    }


def reference(x_prompt, x_sample, attn_norm, w_in, mla_q_norm, mla_kv_norm, mla_w_uq, mla_w_ukv,
              mla_q_gain, mla_k_gain, ssd_conv_w, ssd_conv_b, ssd_dt_bias, ssd_a_log, ssd_d, ssd_norm,
              diff_q_gain, diff_k_gain, diff_lambda, diff_subln, w_out, ffn_norm, w_router, w_gate,
              w_up, w_down):
    y_prompt = encoder_trunk(x_prompt, attn_norm, w_in, mla_q_norm, mla_kv_norm, mla_w_uq, mla_w_ukv,
                             mla_q_gain, mla_k_gain, ssd_conv_w, ssd_conv_b, ssd_dt_bias, ssd_a_log,
                             ssd_d, ssd_norm, diff_q_gain, diff_k_gain, diff_lambda, diff_subln, w_out,
                             ffn_norm, w_router, w_gate, w_up, w_down)
    y_sample = encoder_trunk(x_sample, attn_norm, w_in, mla_q_norm, mla_kv_norm, mla_w_uq, mla_w_ukv,
                             mla_q_gain, mla_k_gain, ssd_conv_w, ssd_conv_b, ssd_dt_bias, ssd_a_log,
                             ssd_d, ssd_norm, diff_q_gain, diff_k_gain, diff_lambda, diff_subln, w_out,
                             ffn_norm, w_router, w_gate, w_up, w_down)
    return (y_prompt, y_sample)
```

```python
import functools
import math

import jax
import jax.numpy as jnp
from jax import lax
from jax.experimental import pallas as pl
from jax.experimental.pallas import tpu as pltpu

F32 = jnp.float32
BF16 = jnp.bfloat16
I32 = jnp.int32

D_MODEL = 1024
DEPTH = 2
EPS = 1e-6
ROPE_THETA = 10000.0

MLA_HEADS = 6
MLA_Q_LORA = 256
MLA_KV_LORA = 128
MLA_NOPE = 64
MLA_ROPE = 32
MLA_QK = MLA_NOPE + MLA_ROPE
MLA_V = 64
MLA_WIDTH = MLA_HEADS * MLA_V

SSD_HEADS = 6
SSD_HEAD_DIM = 64
SSD_INNER = SSD_HEADS * SSD_HEAD_DIM
SSD_GROUPS = 2
SSD_STATE = 128
SSD_CONV = 5
SSD_CONV_DIM = SSD_INNER + 2 * SSD_GROUPS * SSD_STATE

DIFF_HEADS = 4
DIFF_QK = 32
DIFF_V = 2 * DIFF_QK
DIFF_WIDTH = DIFF_HEADS * DIFF_V

N_EXPERTS = 16
EC_CAPACITY = 2
EXPERT_FF = 1024

OFF_KV = MLA_Q_LORA
OFF_KPE = OFF_KV + MLA_KV_LORA
OFF_Z = OFF_KPE + MLA_ROPE
OFF_XBC = OFF_Z + SSD_INNER
OFF_DT = OFF_XBC + SSD_CONV_DIM
OFF_DQ = OFF_DT + 2 * SSD_HEADS
OFF_DK = OFF_DQ + DIFF_HEADS * 2 * DIFF_QK
OFF_DV = OFF_DK + DIFF_HEADS * 2 * DIFF_QK
IN_COLS = OFF_DV + DIFF_WIDTH

LANE = 128
SUBLANE = 8

MLA_IN = 512
SSD_IN = SSD_INNER + SSD_CONV_DIM + LANE
DIFF_IN = 768
PROJ_COLS = MLA_IN + SSD_IN + DIFF_IN

TOKEN_TILE = 512
ATTN_TQ = 512
ATTN_TK = 512
SSD_T = 256
COMBINE_TILE = 128
FFN_SLOTS = 256
IDX_SLOTS = 256
VMEM_LIMIT = 56 * 1024 * 1024


def _cparams(*sem):
    return pltpu.CompilerParams(dimension_semantics=sem, vmem_limit_bytes=VMEM_LIMIT)


def _split_bf16(a, terms):
    parts = []
    rem = a
    for _ in range(terms):
        p = rem.astype(BF16)
        parts.append(p)
        rem = rem - p.astype(F32)
    return parts


def _dot(a, b):
    return jnp.dot(a, b, preferred_element_type=F32)


def _dot_nt(a, b):
    return lax.dot_general(a, b, (((1,), (1,)), ((), ())), preferred_element_type=F32)


def _dot_tn(a, b):
    return lax.dot_general(a, b, (((0,), (0,)), ((), ())), preferred_element_type=F32)


def _dot_split_lhs(a_f32, b_bf16, terms):
    out = None
    for p in _split_bf16(a_f32, terms):
        d = _dot(p, b_bf16)
        out = d if out is None else out + d
    return out


def _dot_split_rhs(a_bf16, b_f32, terms):
    out = None
    for p in _split_bf16(b_f32, terms):
        d = _dot(a_bf16, p)
        out = d if out is None else out + d
    return out


def _silu(x):
    return x * jax.nn.sigmoid(x)


def _inproj_kernel(x_ref, g_ref, w_ref, a_ref, b_ref, c_ref):
    x = x_ref[...]
    ms = jnp.mean(x * x, axis=-1, keepdims=True)
    h = (x * lax.rsqrt(ms + EPS) * g_ref[...]).astype(BF16)
    y = _dot(h, w_ref[...])
    a_ref[...] = y[:, :MLA_IN]
    b_ref[...] = y[:, MLA_IN:MLA_IN + SSD_IN]
    c_ref[...] = y[:, MLA_IN + SSD_IN:]


def _inproj(x2, gain, w):
    t = x2.shape[0]
    tm = TOKEN_TILE
    return pl.pallas_call(
        _inproj_kernel,
        grid=(t // tm,),
        in_specs=[pl.BlockSpec((tm, D_MODEL), lambda i: (i, 0)),
                  pl.BlockSpec((1, D_MODEL), lambda i: (0, 0)),
                  pl.BlockSpec((D_MODEL, PROJ_COLS), lambda i: (0, 0))],
        out_specs=[pl.BlockSpec((tm, MLA_IN), lambda i: (i, 0)),
                   pl.BlockSpec((tm, SSD_IN), lambda i: (i, 0)),
                   pl.BlockSpec((tm, DIFF_IN), lambda i: (i, 0))],
        out_shape=[jax.ShapeDtypeStruct((t, MLA_IN), F32),
                   jax.ShapeDtypeStruct((t, SSD_IN), F32),
                   jax.ShapeDtypeStruct((t, DIFF_IN), F32)],
        compiler_params=_cparams("parallel"),
        name="inproj",
    )(x2, gain, w)


def _rope(x, c, sa, sb, shift):
    return x * c + pltpu.roll(x, shift, 1) * sa + pltpu.roll(x, LANE - shift, 1) * sb


def _mla_prep_kernel(a_ref, qn_ref, kvn_ref, wq_ref, wk_ref, wv_ref, qg_ref, kg_ref,
                     c_ref, sa_ref, sb_ref, q_ref, k_ref, vt_ref):
    a = a_ref[...]
    cq = a[:, :MLA_Q_LORA]
    ckv = a[:, MLA_Q_LORA:MLA_Q_LORA + MLA_KV_LORA]
    kpe = a[:, MLA_Q_LORA + MLA_KV_LORA:]
    cqn = (cq * lax.rsqrt(jnp.mean(cq * cq, axis=-1, keepdims=True) + EPS) * qn_ref[...]).astype(BF16)
    ckvn = (ckv * lax.rsqrt(jnp.mean(ckv * ckv, axis=-1, keepdims=True) + EPS) * kvn_ref[...]).astype(BF16)
    q = _dot(cqn, wq_ref[...])
    kn = _dot(ckvn, wk_ref[...])
    v = _dot(ckvn, wv_ref[...])
    c = c_ref[...]
    sa = sa_ref[...]
    sb = sb_ref[...]
    scale = MLA_QK ** -0.5
    for h in range(MLA_HEADS):
        sl = slice(h * LANE, (h + 1) * LANE)
        qh = q[:, sl]
        qh = qh * lax.rsqrt(jnp.sum(qh * qh, axis=-1, keepdims=True) / MLA_QK + EPS) * qg_ref[...]
        q_ref[:, sl] = (_rope(qh, c, sa, sb, MLA_ROPE // 2) * scale).astype(BF16)
        kh = kn[:, sl] + kpe
        kh = kh * lax.rsqrt(jnp.sum(kh * kh, axis=-1, keepdims=True) / MLA_QK + EPS) * kg_ref[...]
        k_ref[:, sl] = _rope(kh, c, sa, sb, MLA_ROPE // 2).astype(BF16)
    vt_ref[0] = v.T.astype(BF16)


def _mla_prep(a, qn, kvn, wq, wk, wv, qg, kg, tabs, seq):
    t = a.shape[0]
    tm = TOKEN_TILE
    nt = seq // tm
    full = lambda r, c: pl.BlockSpec((r, c), lambda i: (0, 0))
    tab = pl.BlockSpec((tm, LANE), lambda i: (i % nt, 0))
    return pl.pallas_call(
        _mla_prep_kernel,
        grid=(t // tm,),
        in_specs=[pl.BlockSpec((tm, MLA_IN), lambda i: (i, 0)),
                  full(1, MLA_Q_LORA), full(1, MLA_KV_LORA),
                  full(MLA_Q_LORA, MLA_HEADS * LANE), full(MLA_KV_LORA, MLA_HEADS * LANE),
                  full(MLA_KV_LORA, MLA_WIDTH), full(1, LANE), full(1, LANE), tab, tab, tab],
        out_specs=[pl.BlockSpec((tm, MLA_HEADS * LANE), lambda i: (i, 0)),
                   pl.BlockSpec((tm, MLA_HEADS * LANE), lambda i: (i, 0)),
                   pl.BlockSpec((1, MLA_WIDTH, tm), lambda i: (i, 0, 0))],
        out_shape=[jax.ShapeDtypeStruct((t, MLA_HEADS * LANE), BF16),
                   jax.ShapeDtypeStruct((t, MLA_HEADS * LANE), BF16),
                   jax.ShapeDtypeStruct((t // tm, MLA_WIDTH, tm), BF16)],
        compiler_params=_cparams("parallel"),
        name="mla_prep",
    )(a, qn, kvn, wq, wk, wv, qg, kg, *tabs)


def _diff_prep_kernel(cin_ref, qg_ref, kg_ref, bd_ref, c_ref, sa_ref, sb_ref, q_ref, k_ref, vt_ref):
    cin = cin_ref[...]
    bd = bd_ref[...]
    c = jnp.concatenate([c_ref[...], c_ref[...]], axis=1)
    sa = jnp.concatenate([sa_ref[...], sa_ref[...]], axis=1)
    sb = jnp.concatenate([sb_ref[...], sb_ref[...]], axis=1)
    lane = lax.broadcasted_iota(I32, (1, LANE), 1)

    def norm_rope(x, g):
        ms = _dot_split_lhs(x * x, bd, 3) / DIFF_QK
        x = x * lax.rsqrt(ms + EPS) * g
        halves = []
        for j in range(2):
            sl = slice(j * LANE, (j + 1) * LANE)
            halves.append(_rope(x[:, sl], c[:, sl], sa[:, sl], sb[:, sl], DIFF_QK // 2))
        return halves

    qh = norm_rope(cin[:, :256], qg_ref[...])
    kh = norm_rope(cin[:, 256:512], kg_ref[...])
    scale = DIFF_QK ** -0.5
    for j in range(2 * DIFF_HEADS):
        grp, sub = divmod(j, LANE // DIFF_QK)
        keep = (lane >= sub * DIFF_QK) & (lane < (sub + 1) * DIFF_QK)
        q_ref[:, j * LANE:(j + 1) * LANE] = jnp.where(keep, qh[grp] * scale, 0.0).astype(BF16)
    k_ref[...] = jnp.concatenate(kh, axis=1).astype(BF16)
    vt_ref[0] = cin[:, 512:].T.astype(BF16)


def _diff_prep(cin, qg, kg, bd, tabs, seq):
    t = cin.shape[0]
    tm = TOKEN_TILE
    nt = seq // tm
    full = lambda r, c: pl.BlockSpec((r, c), lambda i: (0, 0))
    tab = pl.BlockSpec((tm, LANE), lambda i: (i % nt, 0))
    nq = 2 * DIFF_HEADS * LANE
    return pl.pallas_call(
        _diff_prep_kernel,
        grid=(t // tm,),
        in_specs=[pl.BlockSpec((tm, DIFF_IN), lambda i: (i, 0)),
                  full(1, 256), full(1, 256), full(256, 256), tab, tab, tab],
        out_specs=[pl.BlockSpec((tm, nq), lambda i: (i, 0)),
                   pl.BlockSpec((tm, 256), lambda i: (i, 0)),
                   pl.BlockSpec((1, DIFF_WIDTH, tm), lambda i: (i, 0, 0))],
        out_shape=[jax.ShapeDtypeStruct((t, nq), BF16),
                   jax.ShapeDtypeStruct((t, 256), BF16),
                   jax.ShapeDtypeStruct((t // tm, DIFF_WIDTH, tm), BF16)],
        compiler_params=_cparams("parallel"),
        name="diff_prep",
    )(cin, qg, kg, bd, *tabs)


def _attn_kernel(q_ref, k_ref, vt_ref, o_ref, m_scr, l_scr, acc_scr, *, nkc, tk):
    q = q_ref[0]
    m_scr[...] = jnp.full_like(m_scr, -jnp.inf)
    l_scr[...] = jnp.zeros_like(l_scr)
    acc_scr[...] = jnp.zeros_like(acc_scr)

    def body(kc, carry):
        k = k_ref[0, pl.ds(pl.multiple_of(kc * tk, tk), tk), :]
        s = _dot_nt(k, q)
        m_prev = m_scr[...]
        m_new = jnp.maximum(m_prev, jnp.max(s, axis=0, keepdims=True))
        alpha = jnp.exp(m_prev - m_new)
        p = jnp.exp(s - m_new)
        l_scr[...] = alpha * l_scr[...] + jnp.sum(p, axis=0, keepdims=True)
        acc_scr[...] = alpha * acc_scr[...] + _dot(vt_ref[0, kc], p.astype(BF16))
        m_scr[...] = m_new
        return carry

    lax.fori_loop(0, nkc, body, 0)
    o_ref[0] = acc_scr[...] / l_scr[...]


def _attention(q, k, vt, heads, kmap, vmap, dv, tk):
    b, s, _ = q.shape
    tq = min(ATTN_TQ, s)
    nkc = s // tk
    return pl.pallas_call(
        functools.partial(_attn_kernel, nkc=nkc, tk=tk),
        grid=(b, heads, s // tq),
        in_specs=[pl.BlockSpec((1, tq, LANE), lambda bi, h, qi: (bi, qi, h)),
                  pl.BlockSpec((1, s, LANE), lambda bi, h, qi: (bi, 0, kmap(h))),
                  pl.BlockSpec((1, nkc, dv, tk), lambda bi, h, qi: (bi, 0, vmap(h), 0))],
        out_specs=pl.BlockSpec((1, dv, tq), lambda bi, h, qi: (bi, h, qi)),
        out_shape=jax.ShapeDtypeStruct((b, heads * dv, s), F32),
        scratch_shapes=[pltpu.VMEM((1, tq), F32), pltpu.VMEM((1, tq), F32), pltpu.VMEM((dv, tq), F32)],
        compiler_params=_cparams("parallel", "parallel", "arbitrary"),
        name="attention",
    )(q, k, vt)


def _ssd_kernel(*refs, rev, nct, t):
    if rev:
        (main_ref, prev_ref, next_ref, yf_ref, cw_ref, cb_ref, dtb_ref, alog_ref, dsk_ref, nw_ref,
         out_ref, h_scr) = refs
    else:
        (main_ref, prev_ref, next_ref, cw_ref, cb_ref, dtb_ref, alog_ref, out_ref, h_scr) = refs
    i = pl.program_id(1)
    c = (nct - 1 - i) if rev else i

    @pl.when(i == 0)
    def _():
        h_scr[...] = jnp.zeros_like(h_scr)

    main = main_ref[...]
    xlo, xhi = SSD_INNER, SSD_INNER + SSD_CONV_DIM
    prev = jnp.where(c > 0, prev_ref[:, xlo:xhi], 0.0)
    nxt = jnp.where(c < nct - 1, next_ref[:, xlo:xhi], 0.0)
    xp = jnp.concatenate([prev, main[:, xlo:xhi], nxt], axis=0)
    acc = jnp.broadcast_to(cb_ref[...], (t, SSD_CONV_DIM))
    for j in range(SSD_CONV):
        sh = (SSD_CONV // 2 - j) % (t + 2 * SUBLANE)
        r = xp if sh == 0 else pltpu.roll(xp, sh, 0)
        acc = acc + r[SUBLANE:SUBLANE + t] * cw_ref[j:j + 1, :]
    xc = _silu(acc)
    xs = xc[:, :SSD_INNER]
    gn = SSD_GROUPS * SSD_STATE
    bmat = xc[:, SSD_INNER:SSD_INNER + gn]
    cmat = xc[:, SSD_INNER + gn:]

    dtr = main[:, xhi:] + dtb_ref[...]
    dt = jnp.maximum(dtr, 0.0) + jnp.log1p(jnp.exp(-jnp.abs(dtr)))
    a = dt * (-jnp.exp(alog_ref[...]))

    row = lax.broadcasted_iota(I32, (t, t), 0)
    col = lax.broadcasted_iota(I32, (t, t), 1)
    lower = row >= col
    tri_l = jnp.where(lower, 1.0, 0.0).astype(BF16)
    tri_u = jnp.where(row <= col, 1.0, 0.0).astype(BF16)
    a_t = a.T
    cs = _dot_split_rhs(tri_l, a, 3)
    cs_t = _dot_split_lhs(a_t, tri_u, 3)
    tot = cs[t - 1:t, :]
    if rev:
        ecol = cs - a
        erow = cs_t - a_t
    lane = lax.broadcasted_iota(I32, (1, LANE), 1)
    hb = SSD_HEADS if rev else 0
    hpg = SSD_HEADS // SSD_GROUPS

    g = []
    bt = []
    for grp in range(SSD_GROUPS):
        bg = bmat[:, grp * SSD_STATE:(grp + 1) * SSD_STATE]
        cg = cmat[:, grp * SSD_STATE:(grp + 1) * SSD_STATE].astype(BF16)
        g.append((_dot_nt(cg, bg.astype(BF16)), cg))
        bt.append(bg.T.astype(BF16))

    for pair in range(SSD_HEADS // 2):
        xpair = xs[:, pair * LANE:(pair + 1) * LANE]
        hstate = h_scr[pair]
        hbf = hstate.astype(BF16)
        y = jnp.zeros((t, LANE), F32)
        hnew = jnp.zeros((SSD_STATE, LANE), F32)
        dec = jnp.zeros((1, LANE), F32)
        for sub in range(2):
            h = pair * 2 + sub
            grp = h // hpg
            gmat, cg = g[grp]
            keep = (lane >= sub * SSD_HEAD_DIM) & (lane < (sub + 1) * SSD_HEAD_DIM)
            hl = hb + h
            dtc = dt[:, hl:hl + 1]
            xm = jnp.where(keep, xpair * dtc, 0.0)
            if rev:
                dmat = jnp.exp(jnp.where(row <= col, erow[hl:hl + 1, :] - ecol[:, hl:hl + 1], -jnp.inf))
                off_scale = jnp.exp(tot[:, hl:hl + 1] - ecol[:, hl:hl + 1])
                st_w = jnp.exp(ecol[:, hl:hl + 1])
            else:
                dmat = jnp.exp(jnp.where(lower, cs[:, hl:hl + 1] - cs_t[hl:hl + 1, :], -jnp.inf))
                off_scale = jnp.exp(cs[:, hl:hl + 1])
                st_w = jnp.exp(tot[:, hl:hl + 1] - cs[:, hl:hl + 1])
            y = y + _dot((gmat * dmat).astype(BF16), xm.astype(BF16))
            y = y + jnp.where(keep, _dot(cg, hbf) * off_scale, 0.0)
            hnew = hnew + _dot(bt[grp], (xm * st_w).astype(BF16))
            dec = dec + jnp.where(keep, jnp.exp(tot[:, hl:hl + 1]), 0.0)
        h_scr[pair] = hstate * dec + hnew
        if rev:
            out_ref[:, pair * LANE:(pair + 1) * LANE] = y
        else:
            out_ref[:, pair * LANE:(pair + 1) * LANE] = y

    if rev:
        yall = out_ref[...] + yf_ref[...] + xs * dsk_ref[...]
        gt = yall * _silu(main[:, :SSD_INNER])
        out_ref[...] = gt * lax.rsqrt(jnp.mean(gt * gt, axis=-1, keepdims=True) + EPS) * nw_ref[...]


def _ssd_pass(bin_, yf, cw, cb, dtb, alog, dsk, nw, batch, seq, rev):
    t = SSD_T
    nct = seq // t
    hb = t // SUBLANE
    nrow8 = batch * seq // SUBLANE

    def cidx(i):
        return (nct - 1 - i) if rev else i

    main = pl.BlockSpec((t, SSD_IN), lambda b, i: (b * nct + cidx(i), 0))
    prev = pl.BlockSpec((SUBLANE, SSD_IN),
                        lambda b, i: (jnp.maximum((b * nct + cidx(i)) * hb - 1, 0), 0))
    nxt = pl.BlockSpec((SUBLANE, SSD_IN),
                       lambda b, i: (jnp.minimum((b * nct + cidx(i) + 1) * hb, nrow8 - 1), 0))
    full = lambda r, c: pl.BlockSpec((r, c), lambda b, i: (0, 0))
    yspec = pl.BlockSpec((t, SSD_INNER), lambda b, i: (b * nct + cidx(i), 0))
    if rev:
        in_specs = [main, prev, nxt, yspec, full(SSD_CONV, SSD_CONV_DIM), full(1, SSD_CONV_DIM),
                    full(1, LANE), full(1, LANE), full(1, SSD_INNER), full(1, SSD_INNER)]
        args = (bin_, bin_, bin_, yf, cw, cb, dtb, alog, dsk, nw)
    else:
        in_specs = [main, prev, nxt, full(SSD_CONV, SSD_CONV_DIM), full(1, SSD_CONV_DIM),
                    full(1, LANE), full(1, LANE)]
        args = (bin_, bin_, bin_, cw, cb, dtb, alog)
    return pl.pallas_call(
        functools.partial(_ssd_kernel, rev=rev, nct=nct, t=t),
        grid=(batch, nct),
        in_specs=in_specs,
        out_specs=yspec,
        out_shape=jax.ShapeDtypeStruct((batch * seq, SSD_INNER), F32),
        scratch_shapes=[pltpu.VMEM((SSD_HEADS // 2, SSD_STATE, LANE), F32)],
        compiler_params=_cparams("parallel", "arbitrary"),
        name="ssd_bwd" if rev else "ssd_fwd",
    )(*args)


def _outproj_kernel(x_ref, mla_ref, ssd_ref, dif_ref, lam_ref, sub_ref, wo_ref, fn_ref, wrh_ref, wrl_ref,
                    xo_ref, hn_ref, aff_ref, *, lambda_init):
    lam = lam_ref[...]
    lam_full = (jnp.exp(jnp.sum(lam[0:1] * lam[1:2], keepdims=True))
                - jnp.exp(jnp.sum(lam[2:3] * lam[3:4], keepdims=True)) + lambda_init)
    dif = dif_ref[0]
    outs = []
    for hd in range(DIFF_HEADS):
        o = dif[2 * hd * DIFF_V:(2 * hd + 1) * DIFF_V] - lam_full * dif[(2 * hd + 1) * DIFF_V:(2 * hd + 2) * DIFF_V]
        o = o * lax.rsqrt(jnp.mean(o * o, axis=0, keepdims=True) + EPS) * sub_ref[...] * (1.0 - lambda_init)
        outs.append(o)
    dt_ = jnp.concatenate(outs, axis=0).astype(BF16)
    wo = wo_ref[...]
    x = x_ref[...]
    x = x + _dot_tn(mla_ref[0].astype(BF16), wo[:MLA_WIDTH])
    x = x + _dot(ssd_ref[...].astype(BF16), wo[MLA_WIDTH:MLA_WIDTH + SSD_INNER])
    x = x + _dot_tn(dt_, wo[MLA_WIDTH + SSD_INNER:])
    xo_ref[...] = x
    hn = x * lax.rsqrt(jnp.mean(x * x, axis=-1, keepdims=True) + EPS) * fn_ref[...]
    hn_ref[...] = hn
    hh, hl = _split_bf16(hn, 2)
    logits = _dot(hh, wrh_ref[...]) + _dot(hl, wrh_ref[...]) + _dot(hh, wrl_ref[...])
    lane = lax.broadcasted_iota(I32, logits.shape, 1)
    logits = jnp.where(lane < N_EXPERTS, logits, -jnp.inf)
    e = jnp.exp(logits - jnp.max(logits, axis=-1, keepdims=True))
    aff = e / jnp.sum(e, axis=-1, keepdims=True)
    aff_ref[...] = aff.T[:N_EXPERTS]


def _outproj(x2, mla_t, ssd, dif_t, lam, sub, wo, fn, wrh, wrl, seq, lambda_init):
    t = x2.shape[0]
    tm = TOKEN_TILE
    nt = seq // tm
    full = lambda r, c: pl.BlockSpec((r, c), lambda i: (0, 0))
    row = lambda c: pl.BlockSpec((tm, c), lambda i: (i, 0))
    return pl.pallas_call(
        functools.partial(_outproj_kernel, lambda_init=lambda_init),
        grid=(t // tm,),
        in_specs=[row(D_MODEL),
                  pl.BlockSpec((1, MLA_WIDTH, tm), lambda i: (i // nt, 0, i % nt)),
                  row(SSD_INNER),
                  pl.BlockSpec((1, 2 * DIFF_WIDTH, tm), lambda i: (i // nt, 0, i % nt)),
                  full(4, DIFF_QK), full(DIFF_V, 1), full(D_MODEL, D_MODEL), full(1, D_MODEL),
                  full(D_MODEL, LANE), full(D_MODEL, LANE)],
        out_specs=[row(D_MODEL), row(D_MODEL), pl.BlockSpec((N_EXPERTS, tm), lambda i: (0, i))],
        out_shape=[jax.ShapeDtypeStruct((t, D_MODEL), F32),
                   jax.ShapeDtypeStruct((t, D_MODEL), F32),
                   jax.ShapeDtypeStruct((N_EXPERTS, t), F32)],
        compiler_params=_cparams("parallel"),
        name="outproj",
    )(x2, mla_t, ssd, dif_t, lam, sub, wo, fn, wrh, wrl)


def _cumsum_rowmajor(mask_f32, tri_u, tri_ls):
    local = _dot(mask_f32.astype(BF16), tri_u)
    rowtot = jnp.broadcast_to(local[:, LANE - 1:LANE], local.shape).astype(BF16)
    return local + _dot(tri_ls, rowtot)


def _select_kernel(aff_ref, pos_ref, idx_ref, toff_ref, tcnt_ref, cum_scr, *, cap, sb, rows):
    a = aff_ref[0]
    key = lax.bitcast_convert_type(a, I32)

    def bit_body(it, prefix):
        cand = prefix | lax.shift_left(jnp.int32(1), 30 - it)
        cnt = jnp.sum((key >= cand).astype(I32))
        return jnp.where(cnt >= cap, cand, prefix)

    thr = lax.fori_loop(0, 31, bit_body, jnp.int32(0))
    gt = key > thr
    eq = key == thr
    need = cap - jnp.sum(gt.astype(I32))
    r_i = lax.broadcasted_iota(I32, (LANE, LANE), 0)
    c_i = lax.broadcasted_iota(I32, (LANE, LANE), 1)
    tri_u = jnp.where(r_i <= c_i, 1.0, 0.0).astype(BF16)
    r_r = lax.broadcasted_iota(I32, (rows, rows), 0)
    c_r = lax.broadcasted_iota(I32, (rows, rows), 1)
    tri_ls = jnp.where(r_r > c_r, 1.0, 0.0).astype(BF16)
    eqf = jnp.where(eq, 1.0, 0.0)
    tie_rank = _cumsum_rowmajor(eqf, tri_u, tri_ls) - eqf
    sel = gt | (eq & (tie_rank < need.astype(F32)))
    self_ = jnp.where(sel, 1.0, 0.0)
    cum = _cumsum_rowmajor(self_, tri_u, tri_ls).astype(I32)
    pos_ref[0] = jnp.where(sel, cum - 1, -1)
    cum_scr[...] = cum
    rowend = cum[:, LANE - 1:LANE]
    rowstart = rowend - jnp.sum(self_, axis=-1, keepdims=True).astype(I32)
    toff_ref[0] = rowstart
    tcnt_ref[0] = rowend - rowstart

    def blk(bi, carry):
        p0 = bi * sb
        r_lo = jnp.sum((rowend <= p0).astype(I32))
        r_hi = jnp.sum((rowstart < p0 + sb).astype(I32))
        s_col = p0 + lax.broadcasted_iota(I32, (sb, LANE), 0)

        def rowbody(r, acc):
            return acc + (cum_scr[pl.ds(r, 1), :] <= s_col).astype(I32)

        acc = lax.fori_loop(r_lo, r_hi, rowbody, jnp.zeros((sb, LANE), I32))
        idx_ref[0, pl.ds(pl.multiple_of(p0, sb), sb), :] = jnp.sum(acc, axis=1, keepdims=True) + r_lo * LANE
        return carry

    lax.fori_loop(0, cap // sb, blk, 0)


def _select(aff3, cap):
    e, rows, _ = aff3.shape
    sb = min(IDX_SLOTS, cap)
    return pl.pallas_call(
        functools.partial(_select_kernel, cap=cap, sb=sb, rows=rows),
        grid=(e,),
        in_specs=[pl.BlockSpec((1, rows, LANE), lambda i: (i, 0, 0))],
        out_specs=[pl.BlockSpec((1, rows, LANE), lambda i: (i, 0, 0)),
                   pl.BlockSpec((1, cap, 1), lambda i: (i, 0, 0)),
                   pl.BlockSpec((1, rows, 1), lambda i: (i, 0, 0)),
                   pl.BlockSpec((1, rows, 1), lambda i: (i, 0, 0))],
        out_shape=[jax.ShapeDtypeStruct((e, rows, LANE), I32),
                   jax.ShapeDtypeStruct((e, cap, 1), I32),
                   jax.ShapeDtypeStruct((e, rows, 1), I32),
                   jax.ShapeDtypeStruct((e, rows, 1), I32)],
        scratch_shapes=[pltpu.VMEM((rows, LANE), I32)],
        compiler_params=_cparams("parallel"),
        name="select",
    )(aff3)


def _row_copy(src_hbm, src_row, dst_vmem, dst_row, sem):
    return pltpu.make_async_copy(src_hbm.at[pl.ds(src_row, 1)], dst_vmem.at[pl.ds(dst_row, 1)], sem)


def _ffn_kernel(idx_ref, hn_hbm, wg_ref, wu_ref, wd_ref, ye_ref, xbuf, sem, *, slots):
    def issue(r, carry):
        _row_copy(hn_hbm, idx_ref[0, 0, r], xbuf, r, sem).start()
        return carry

    lax.fori_loop(0, slots, issue, 0)

    def wait(r, carry):
        _row_copy(hn_hbm, 0, xbuf, r, sem).wait()
        return carry

    lax.fori_loop(0, slots, wait, 0)
    x = xbuf[...].astype(BF16)
    gate = _dot(x, wg_ref[0])
    up = _dot(x, wu_ref[0])
    hid = (_silu(gate) * up).astype(BF16)
    ye_ref[...] = _dot(hid, wd_ref[0])


def _ffn(idx3, hn, wg, wu, wd, cap):
    slots = idx3.shape[-1]
    nb = cap // slots
    wspec = lambda: pl.BlockSpec((1, D_MODEL, EXPERT_FF), lambda e, b: (e, 0, 0))
    return pl.pallas_call(
        functools.partial(_ffn_kernel, slots=slots),
        grid=(N_EXPERTS, nb),
        in_specs=[pl.BlockSpec((1, 1, slots), lambda e, b: (e * nb + b, 0, 0), memory_space=pltpu.SMEM),
                  pl.BlockSpec(memory_space=pl.ANY),
                  wspec(), wspec(),
                  pl.BlockSpec((1, EXPERT_FF, D_MODEL), lambda e, b: (e, 0, 0))],
        out_specs=pl.BlockSpec((slots, D_MODEL), lambda e, b: (e * nb + b, 0)),
        out_shape=jax.ShapeDtypeStruct((N_EXPERTS * cap, D_MODEL), F32),
        scratch_shapes=[pltpu.VMEM((slots, D_MODEL), F32), pltpu.SemaphoreType.DMA(())],
        compiler_params=_cparams("arbitrary", "arbitrary"),
        name="expert_ffn",
    )(idx3, hn, wg, wu, wd)


def _combine_kernel(toff_ref, tcnt_ref, x_ref, pos_ref, aff_ref, ye_hbm, out_ref, buf, sem, *, cap, rows, chunk):
    r = pl.program_id(0)
    dst = jnp.int32(0)
    offs = []
    dsts = []
    for e in range(N_EXPERTS):
        off = toff_ref[e * rows + r]
        cnt = tcnt_ref[e * rows + r]

        def issue(s, carry, e=e, off=off, dst=dst):
            _row_copy(ye_hbm, e * cap + off + s, buf, dst + s, sem).start()
            return carry

        lax.fori_loop(0, cnt, issue, 0)
        offs.append(off)
        dsts.append(dst)
        dst = dst + cnt
    total = dst

    def wait(s, carry):
        _row_copy(ye_hbm, 0, buf, s, sem).wait()
        return carry

    lax.fori_loop(0, total, wait, 0)
    out_ref[...] = x_ref[...]
    pos = pos_ref[...]
    aff = aff_ref[...]
    tgt = [jnp.where(pos[e:e + 1] >= 0, pos[e:e + 1] - offs[e] + dsts[e], -1) for e in range(N_EXPERTS)]

    def chunk_body(k, carry):
        base = pl.multiple_of(k * chunk, chunk)
        cidx = base + lax.broadcasted_iota(I32, (chunk, COMBINE_TILE), 0)
        wt = jnp.zeros((chunk, COMBINE_TILE), F32)
        for e in range(N_EXPERTS):
            wt = wt + jnp.where(cidx == tgt[e], aff[e:e + 1], 0.0)
        rowid = base + lax.broadcasted_iota(I32, (chunk, 1), 0)
        rowsv = jnp.where(rowid < total, buf[pl.ds(base, chunk), :], 0.0).astype(BF16)
        wh, wl = _split_bf16(wt, 2)
        out_ref[...] += _dot_tn(wh, rowsv) + _dot_tn(wl, rowsv)
        return carry

    lax.fori_loop(0, (total + chunk - 1) // chunk, chunk_body, 0)


def _combine(toff, tcnt, x2, pos, aff_t, ye, cap):
    t = x2.shape[0]
    tile = COMBINE_TILE
    rows = t // tile
    chunk = 256
    grid_spec = pltpu.PrefetchScalarGridSpec(
        num_scalar_prefetch=2,
        grid=(rows,),
        in_specs=[pl.BlockSpec((tile, D_MODEL), lambda i, *_: (i, 0)),
                  pl.BlockSpec((N_EXPERTS, tile), lambda i, *_: (0, i)),
                  pl.BlockSpec((N_EXPERTS, tile), lambda i, *_: (0, i)),
                  pl.BlockSpec(memory_space=pl.ANY)],
        out_specs=pl.BlockSpec((tile, D_MODEL), lambda i, *_: (i, 0)),
        scratch_shapes=[pltpu.VMEM((N_EXPERTS * tile, D_MODEL), F32), pltpu.SemaphoreType.DMA(())],
    )
    return pl.pallas_call(
        functools.partial(_combine_kernel, cap=cap, rows=rows, chunk=chunk),
        grid_spec=grid_spec,
        out_shape=jax.ShapeDtypeStruct((t, D_MODEL), F32),
        compiler_params=_cparams("arbitrary"),
        name="combine",
    )(toff, tcnt, x2, pos, aff_t, ye)


def _rope_tables(seq):
    inv = ROPE_THETA ** (-jnp.arange(0, MLA_ROPE, 2, dtype=F32) / MLA_ROPE)
    ang = jnp.arange(seq, dtype=F32)[:, None] * inv[None, :]
    cos, sin = jnp.cos(ang), jnp.sin(ang)
    one = jnp.ones((seq, 1), F32)
    zero = jnp.zeros((seq, 1), F32)
    rep = lambda v, n: jnp.broadcast_to(v, (seq, n))
    half = MLA_ROPE // 2
    mla = (jnp.concatenate([rep(one, MLA_NOPE), cos, cos, rep(one, LANE - MLA_QK)], axis=1),
           jnp.concatenate([rep(zero, MLA_NOPE + half), sin, rep(zero, LANE - MLA_QK)], axis=1),
           jnp.concatenate([rep(zero, MLA_NOPE), -sin, rep(zero, LANE - MLA_NOPE - half)], axis=1))
    nblk = LANE // DIFF_QK
    dif = (jnp.tile(jnp.concatenate([cos, cos], axis=1), (1, nblk)),
           jnp.tile(jnp.concatenate([rep(zero, half), sin], axis=1), (1, nblk)),
           jnp.tile(jnp.concatenate([-sin, rep(zero, half)], axis=1), (1, nblk)))
    return mla, dif


def _pad_cols(w, width):
    return jnp.pad(w, ((0, 0), (0, width - w.shape[1])))


def _layer_params(p, l):
    w_in = p['w_in'][l]
    z64 = jnp.zeros((D_MODEL, 64), F32)
    z32 = jnp.zeros((D_MODEL, 32), F32)
    w_proj = jnp.concatenate([
        w_in[:, :OFF_KPE], z64, w_in[:, OFF_KPE:OFF_Z], z32,
        w_in[:, OFF_Z:OFF_DT], _pad_cols(w_in[:, OFF_DT:OFF_DQ], LANE),
        w_in[:, OFF_DQ:]], axis=1).astype(BF16)
    wq = p['mla_w_uq'][l].reshape(MLA_Q_LORA, MLA_HEADS, MLA_QK)
    wq = jnp.pad(wq, ((0, 0), (0, 0), (0, LANE - MLA_QK))).reshape(MLA_Q_LORA, MLA_HEADS * LANE)
    wkv = p['mla_w_ukv'][l].reshape(MLA_KV_LORA, MLA_HEADS, MLA_NOPE + MLA_V)
    wk = jnp.pad(wkv[:, :, :MLA_NOPE], ((0, 0), (0, 0), (0, LANE - MLA_NOPE))).reshape(MLA_KV_LORA, MLA_HEADS * LANE)
    wv = wkv[:, :, MLA_NOPE:].reshape(MLA_KV_LORA, MLA_WIDTH)
    wr = _pad_cols(p['w_router'][l], LANE)
    wrh = wr.astype(BF16)
    wrl = (wr - wrh.astype(F32)).astype(BF16)
    pad_row = lambda v, n: jnp.pad(v, (0, n - v.shape[0]))[None, :]
    return dict(
        attn_norm=p['attn_norm'][l][None, :], w_proj=w_proj,
        q_norm=p['mla_q_norm'][l][None, :], kv_norm=p['mla_kv_norm'][l][None, :],
        wq=wq.astype(BF16), wk=wk.astype(BF16), wv=wv.astype(BF16),
        q_gain=pad_row(p['mla_q_gain'][l], LANE), k_gain=pad_row(p['mla_k_gain'][l], LANE),
        conv_w=p['ssd_conv_w'][l], conv_b=p['ssd_conv_b'][l][None, :],
        dt_bias=pad_row(p['ssd_dt_bias'][l].reshape(-1), LANE),
        a_log=pad_row(p['ssd_a_log'][l].reshape(-1), LANE),
        d_skip=jnp.repeat(p['ssd_d'][l], SSD_HEAD_DIM)[None, :], ssd_norm=p['ssd_norm'][l][None, :],
        dq_gain=jnp.tile(p['diff_q_gain'][l], 2 * DIFF_HEADS)[None, :],
        dk_gain=jnp.tile(p['diff_k_gain'][l], 2 * DIFF_HEADS)[None, :],
        lam=p['diff_lambda'][l], subln=p['diff_subln'][l][:, None],
        w_out=p['w_out'][l].astype(BF16), ffn_norm=p['ffn_norm'][l][None, :],
        wrh=wrh, wrl=wrl,
        w_gate=p['w_gate'][l].astype(BF16), w_up=p['w_up'][l].astype(BF16), w_down=p['w_down'][l].astype(BF16),
    )


def _block_diag_ones(n, blk):
    i = jnp.arange(n)
    return (i[:, None] // blk == i[None, :] // blk).astype(BF16)


def _trunk(x, p):
    batch, seq, _ = x.shape
    t = batch * seq
    cap = EC_CAPACITY * t // N_EXPERTS
    tk = min(ATTN_TK, seq)
    mla_tabs, dif_tabs = _rope_tables(seq)
    bd = _block_diag_ones(2 * DIFF_HEADS * DIFF_QK, DIFF_QK)
    x2 = x.reshape(t, D_MODEL)
    rows = t // LANE
    for l in range(DEPTH):
        lp = _layer_params(p, l)
        a_in, b_in, c_in = _inproj(x2, lp['attn_norm'], lp['w_proj'])
        q, k, vt = _mla_prep(a_in, lp['q_norm'], lp['kv_norm'], lp['wq'], lp['wk'], lp['wv'],
                             lp['q_gain'], lp['k_gain'], mla_tabs, seq)
        mla_t = _attention(q.reshape(batch, seq, -1), k.reshape(batch, seq, -1),
                           vt.reshape(batch, seq // tk, MLA_WIDTH, tk),
                           MLA_HEADS, lambda h: h, lambda h: h, MLA_V, tk)
        dq, dk, dvt = _diff_prep(c_in, lp['dq_gain'], lp['dk_gain'], bd, dif_tabs, seq)
        dif_t = _attention(dq.reshape(batch, seq, -1), dk.reshape(batch, seq, -1),
                           dvt.reshape(batch, seq // tk, DIFF_WIDTH, tk),
                           2 * DIFF_HEADS, lambda h: h // (LANE // DIFF_QK), lambda h: h // 2, DIFF_V, tk)
        yf = _ssd_pass(b_in, None, lp['conv_w'], lp['conv_b'], lp['dt_bias'], lp['a_log'],
                       None, None, batch, seq, rev=False)
        ssd = _ssd_pass(b_in, yf, lp['conv_w'], lp['conv_b'], lp['dt_bias'], lp['a_log'],
                        lp['d_skip'], lp['ssd_norm'], batch, seq, rev=True)
        lambda_init = 0.8 - 0.6 * math.exp(-0.3 * l)
        x2, hn, aff_t = _outproj(x2, mla_t, ssd, dif_t, lp['lam'], lp['subln'], lp['w_out'],
                                 lp['ffn_norm'], lp['wrh'], lp['wrl'], seq, lambda_init)
        pos3, idx, toff, tcnt = _select(aff_t.reshape(N_EXPERTS, rows, LANE), cap)
        slots = min(FFN_SLOTS, cap)
        ye = _ffn(idx.reshape(N_EXPERTS * cap // slots, 1, slots), hn,
                  lp['w_gate'], lp['w_up'], lp['w_down'], cap)
        x2 = _combine(toff.reshape(-1), tcnt.reshape(-1), x2, pos3.reshape(N_EXPERTS, t), aff_t, ye, cap)
    return x2.reshape(batch, seq, D_MODEL)


def kernel(x_prompt, x_sample, attn_norm, w_in, mla_q_norm, mla_kv_norm, mla_w_uq, mla_w_ukv, mla_q_gain,
           mla_k_gain, ssd_conv_w, ssd_conv_b, ssd_dt_bias, ssd_a_log, ssd_d, ssd_norm, diff_q_gain,
           diff_k_gain, diff_lambda, diff_subln, w_out, ffn_norm, w_router, w_gate, w_up, w_down):
    p = dict(attn_norm=attn_norm, w_in=w_in, mla_q_norm=mla_q_norm, mla_kv_norm=mla_kv_norm,
             mla_w_uq=mla_w_uq, mla_w_ukv=mla_w_ukv, mla_q_gain=mla_q_gain, mla_k_gain=mla_k_gain,
             ssd_conv_w=ssd_conv_w, ssd_conv_b=ssd_conv_b, ssd_dt_bias=ssd_dt_bias, ssd_a_log=ssd_a_log,
             ssd_d=ssd_d, ssd_norm=ssd_norm, diff_q_gain=diff_q_gain, diff_k_gain=diff_k_gain,
             diff_lambda=diff_lambda, diff_subln=diff_subln, w_out=w_out, ffn_norm=ffn_norm,
             w_router=w_router, w_gate=w_gate, w_up=w_up, w_down=w_down)
    return _trunk(x_prompt, p), _trunk(x_sample, p)
```

```python
import functools
import math

import jax
import jax.numpy as jnp
from jax import lax
from jax.experimental import pallas as pl
from jax.experimental.pallas import tpu as pltpu

F32 = jnp.float32
BF16 = jnp.bfloat16
I32 = jnp.int32

D_MODEL = 1024
DEPTH = 2
EPS = 1e-6
ROPE_THETA = 10000.0

MLA_HEADS = 6
MLA_Q_LORA = 256
MLA_KV_LORA = 128
MLA_NOPE = 64
MLA_ROPE = 32
MLA_QK = MLA_NOPE + MLA_ROPE
MLA_V = 64
MLA_WIDTH = MLA_HEADS * MLA_V

SSD_HEADS = 6
SSD_HEAD_DIM = 64
SSD_INNER = SSD_HEADS * SSD_HEAD_DIM
SSD_GROUPS = 2
SSD_STATE = 128
SSD_CONV = 5
SSD_CONV_DIM = SSD_INNER + 2 * SSD_GROUPS * SSD_STATE

DIFF_HEADS = 4
DIFF_QK = 32
DIFF_V = 2 * DIFF_QK
DIFF_WIDTH = DIFF_HEADS * DIFF_V

N_EXPERTS = 16
EC_CAPACITY = 2
EXPERT_FF = 1024

OFF_KV = MLA_Q_LORA
OFF_KPE = OFF_KV + MLA_KV_LORA
OFF_Z = OFF_KPE + MLA_ROPE
OFF_XBC = OFF_Z + SSD_INNER
OFF_DT = OFF_XBC + SSD_CONV_DIM
OFF_DQ = OFF_DT + 2 * SSD_HEADS
OFF_DK = OFF_DQ + DIFF_HEADS * 2 * DIFF_QK
OFF_DV = OFF_DK + DIFF_HEADS * 2 * DIFF_QK
IN_COLS = OFF_DV + DIFF_WIDTH

LANE = 128
SUBLANE = 8

MLA_IN = 512
SSD_IN = SSD_INNER + SSD_CONV_DIM + LANE
DIFF_IN = 768
PROJ_COLS = MLA_IN + SSD_IN + DIFF_IN

TOKEN_TILE = 512
ATTN_TQ = 512
ATTN_TK = 512
ATTN_GROUP = 4
SSD_T = 256
COMBINE_TILE = 128
V_ROWS = 80
LOG2E = math.log2(math.e)
NOSHIFT_MAX_LOG2 = 60.0
FFN_SLOTS = 256
IDX_SLOTS = 256
VMEM_LIMIT = 56 * 1024 * 1024


def _cparams(*sem):
    return pltpu.CompilerParams(dimension_semantics=sem, vmem_limit_bytes=VMEM_LIMIT)


def _split_bf16(a, terms):
    parts = []
    rem = a
    for _ in range(terms):
        p = rem.astype(BF16)
        parts.append(p)
        rem = rem - p.astype(F32)
    return parts


def _dot(a, b):
    return jnp.dot(a, b, preferred_element_type=F32)


def _dot_nt(a, b):
    return lax.dot_general(a, b, (((1,), (1,)), ((), ())), preferred_element_type=F32)


def _dot_tn(a, b):
    return lax.dot_general(a, b, (((0,), (0,)), ((), ())), preferred_element_type=F32)


def _dot_split_lhs(a_f32, b_bf16, terms):
    out = None
    for p in _split_bf16(a_f32, terms):
        d = _dot(p, b_bf16)
        out = d if out is None else out + d
    return out


def _dot_split_rhs(a_bf16, b_f32, terms):
    out = None
    for p in _split_bf16(b_f32, terms):
        d = _dot(a_bf16, p)
        out = d if out is None else out + d
    return out


def _silu(x):
    return x * jax.nn.sigmoid(x)


def _inproj_kernel(x_ref, g_ref, w_ref, a_ref, b_ref, c_ref):
    x = x_ref[...]
    ms = jnp.mean(x * x, axis=-1, keepdims=True)
    h = (x * lax.rsqrt(ms + EPS) * g_ref[...]).astype(BF16)
    y = _dot(h, w_ref[...])
    a_ref[...] = y[:, :MLA_IN]
    b_ref[...] = y[:, MLA_IN:MLA_IN + SSD_IN]
    c_ref[...] = y[:, MLA_IN + SSD_IN:]


def _inproj(x2, gain, w):
    t = x2.shape[0]
    tm = TOKEN_TILE
    return pl.pallas_call(
        _inproj_kernel,
        grid=(t // tm,),
        in_specs=[pl.BlockSpec((tm, D_MODEL), lambda i: (i, 0)),
                  pl.BlockSpec((1, D_MODEL), lambda i: (0, 0)),
                  pl.BlockSpec((D_MODEL, PROJ_COLS), lambda i: (0, 0))],
        out_specs=[pl.BlockSpec((tm, MLA_IN), lambda i: (i, 0)),
                   pl.BlockSpec((tm, SSD_IN), lambda i: (i, 0)),
                   pl.BlockSpec((tm, DIFF_IN), lambda i: (i, 0))],
        out_shape=[jax.ShapeDtypeStruct((t, MLA_IN), F32),
                   jax.ShapeDtypeStruct((t, SSD_IN), F32),
                   jax.ShapeDtypeStruct((t, DIFF_IN), F32)],
        compiler_params=_cparams("parallel"),
        name="inproj",
    )(x2, gain, w)


def _rope(x, c, sa, sb, shift):
    return x * c + pltpu.roll(x, shift, 1) * sa + pltpu.roll(x, LANE - shift, 1) * sb


def _mla_prep_kernel(a_ref, qn_ref, kvn_ref, wq_ref, wk_ref, wv_ref, qg_ref, kg_ref,
                     c_ref, sa_ref, sb_ref, q_ref, k_ref, vt_ref):
    a = a_ref[...]
    cq = a[:, :MLA_Q_LORA]
    ckv = a[:, MLA_Q_LORA:MLA_Q_LORA + MLA_KV_LORA]
    kpe = a[:, MLA_Q_LORA + MLA_KV_LORA:]
    cqn = (cq * lax.rsqrt(jnp.mean(cq * cq, axis=-1, keepdims=True) + EPS) * qn_ref[...]).astype(BF16)
    ckvn = (ckv * lax.rsqrt(jnp.mean(ckv * ckv, axis=-1, keepdims=True) + EPS) * kvn_ref[...]).astype(BF16)
    q = _dot(cqn, wq_ref[...])
    kn = _dot(ckvn, wk_ref[...])
    v = _dot(ckvn, wv_ref[...])
    c = c_ref[...]
    sa = sa_ref[...]
    sb = sb_ref[...]
    scale = MLA_QK ** -0.5 * LOG2E
    for h in range(MLA_HEADS):
        sl = slice(h * LANE, (h + 1) * LANE)
        qh = q[:, sl]
        qh = qh * lax.rsqrt(jnp.sum(qh * qh, axis=-1, keepdims=True) / MLA_QK + EPS) * qg_ref[...]
        q_ref[:, sl] = (_rope(qh, c, sa, sb, MLA_ROPE // 2) * scale).astype(BF16)
        kh = kn[:, sl] + kpe
        kh = kh * lax.rsqrt(jnp.sum(kh * kh, axis=-1, keepdims=True) / MLA_QK + EPS) * kg_ref[...]
        k_ref[:, sl] = _rope(kh, c, sa, sb, MLA_ROPE // 2).astype(BF16)
    vt = v.T.astype(BF16)
    ones = jnp.ones((V_ROWS - MLA_V, vt.shape[1]), BF16)
    for h in range(MLA_HEADS):
        vt_ref[0, h * V_ROWS:(h + 1) * V_ROWS, :] = jnp.concatenate([vt[h * MLA_V:(h + 1) * MLA_V], ones], axis=0)


def _mla_prep(a, qn, kvn, wq, wk, wv, qg, kg, tabs, seq):
    t = a.shape[0]
    tm = TOKEN_TILE
    nt = seq // tm
    full = lambda r, c: pl.BlockSpec((r, c), lambda i: (0, 0))
    tab = pl.BlockSpec((tm, LANE), lambda i: (i % nt, 0))
    return pl.pallas_call(
        _mla_prep_kernel,
        grid=(t // tm,),
        in_specs=[pl.BlockSpec((tm, MLA_IN), lambda i: (i, 0)),
                  full(1, MLA_Q_LORA), full(1, MLA_KV_LORA),
                  full(MLA_Q_LORA, MLA_HEADS * LANE), full(MLA_KV_LORA, MLA_HEADS * LANE),
                  full(MLA_KV_LORA, MLA_WIDTH), full(1, LANE), full(1, LANE), tab, tab, tab],
        out_specs=[pl.BlockSpec((tm, MLA_HEADS * LANE), lambda i: (i, 0)),
                   pl.BlockSpec((tm, MLA_HEADS * LANE), lambda i: (i, 0)),
                   pl.BlockSpec((1, MLA_HEADS * V_ROWS, tm), lambda i: (i, 0, 0))],
        out_shape=[jax.ShapeDtypeStruct((t, MLA_HEADS * LANE), BF16),
                   jax.ShapeDtypeStruct((t, MLA_HEADS * LANE), BF16),
                   jax.ShapeDtypeStruct((t // tm, MLA_HEADS * V_ROWS, tm), BF16)],
        compiler_params=_cparams("parallel"),
        name="mla_prep",
    )(a, qn, kvn, wq, wk, wv, qg, kg, *tabs)


def _diff_prep_kernel(cin_ref, qg_ref, kg_ref, bd_ref, c_ref, sa_ref, sb_ref, q_ref, k_ref, vt_ref):
    cin = cin_ref[...]
    bd = bd_ref[...]
    c = jnp.concatenate([c_ref[...], c_ref[...]], axis=1)
    sa = jnp.concatenate([sa_ref[...], sa_ref[...]], axis=1)
    sb = jnp.concatenate([sb_ref[...], sb_ref[...]], axis=1)
    lane = lax.broadcasted_iota(I32, (1, LANE), 1)

    def norm_rope(x, g):
        ms = _dot_split_lhs(x * x, bd, 3) / DIFF_QK
        x = x * lax.rsqrt(ms + EPS) * g
        halves = []
        for j in range(2):
            sl = slice(j * LANE, (j + 1) * LANE)
            halves.append(_rope(x[:, sl], c[:, sl], sa[:, sl], sb[:, sl], DIFF_QK // 2))
        return halves

    qh = norm_rope(cin[:, :256], qg_ref[...])
    kh = norm_rope(cin[:, 256:512], kg_ref[...])
    scale = DIFF_QK ** -0.5 * LOG2E
    for j in range(2 * DIFF_HEADS):
        grp, sub = divmod(j, LANE // DIFF_QK)
        keep = (lane >= sub * DIFF_QK) & (lane < (sub + 1) * DIFF_QK)
        q_ref[:, j * LANE:(j + 1) * LANE] = jnp.where(keep, qh[grp] * scale, 0.0).astype(BF16)
    k_ref[...] = jnp.concatenate(kh, axis=1).astype(BF16)
    vt = cin[:, 512:].T.astype(BF16)
    ones = jnp.ones((V_ROWS - DIFF_V, vt.shape[1]), BF16)
    for h in range(DIFF_HEADS):
        vt_ref[0, h * V_ROWS:(h + 1) * V_ROWS, :] = jnp.concatenate([vt[h * DIFF_V:(h + 1) * DIFF_V], ones], axis=0)


def _diff_prep(cin, qg, kg, bd, tabs, seq):
    t = cin.shape[0]
    tm = TOKEN_TILE
    nt = seq // tm
    full = lambda r, c: pl.BlockSpec((r, c), lambda i: (0, 0))
    tab = pl.BlockSpec((tm, LANE), lambda i: (i % nt, 0))
    nq = 2 * DIFF_HEADS * LANE
    return pl.pallas_call(
        _diff_prep_kernel,
        grid=(t // tm,),
        in_specs=[pl.BlockSpec((tm, DIFF_IN), lambda i: (i, 0)),
                  full(1, 256), full(1, 256), full(256, 256), tab, tab, tab],
        out_specs=[pl.BlockSpec((tm, nq), lambda i: (i, 0)),
                   pl.BlockSpec((tm, 256), lambda i: (i, 0)),
                   pl.BlockSpec((1, DIFF_HEADS * V_ROWS, tm), lambda i: (i, 0, 0))],
        out_shape=[jax.ShapeDtypeStruct((t, nq), BF16),
                   jax.ShapeDtypeStruct((t, 256), BF16),
                   jax.ShapeDtypeStruct((t // tm, DIFF_HEADS * V_ROWS, tm), BF16)],
        compiler_params=_cparams("parallel"),
        name="diff_prep",
    )(cin, qg, kg, bd, *tabs)


def _attn_noshift_kernel(q_ref, k_ref, vt_ref, o_ref, acc_scr, *, nkc, tk, dv):
    q = q_ref[0]
    acc_scr[...] = jnp.zeros_like(acc_scr)

    group = math.gcd(nkc, ATTN_GROUP)

    def body(g, carry):
        tot = None
        for j in range(group):
            kc = g * group + j
            k = k_ref[0, pl.ds(pl.multiple_of(kc * tk, tk), tk), :]
            p = jnp.exp2(_dot_nt(k, q)).astype(BF16)
            d = _dot(vt_ref[0, kc], p)
            tot = d if tot is None else tot + d
        acc_scr[...] += tot
        return carry

    lax.fori_loop(0, nkc // group, body, 0)
    acc = acc_scr[...]
    o_ref[0] = acc[:dv] / acc[dv:dv + 1]


def _attn_online_kernel(q_ref, k_ref, vt_ref, o_ref, m_scr, acc_scr, *, nkc, tk, dv):
    q = q_ref[0]
    m_scr[...] = jnp.full_like(m_scr, -jnp.inf)
    acc_scr[...] = jnp.zeros_like(acc_scr)

    def body(kc, carry):
        k = k_ref[0, pl.ds(pl.multiple_of(kc * tk, tk), tk), :]
        s = _dot_nt(k, q)
        m_prev = m_scr[...]
        m_new = jnp.maximum(m_prev, jnp.max(s, axis=0, keepdims=True))
        p = jnp.exp2(s - m_new).astype(BF16)
        acc_scr[...] = jnp.exp2(m_prev - m_new) * acc_scr[...] + _dot(vt_ref[0, kc], p)
        m_scr[...] = m_new
        return carry

    lax.fori_loop(0, nkc, body, 0)
    acc = acc_scr[...]
    o_ref[0] = acc[:dv] / acc[dv:dv + 1]


def _attention(q, k, vt, heads, kmap, vmap, dv, tk, score_bound):
    b, s, _ = q.shape
    tq = min(ATTN_TQ, s)
    nkc = s // tk

    def call(body, scratch, name):
        return pl.pallas_call(
            functools.partial(body, nkc=nkc, tk=tk, dv=dv),
            grid=(b, heads, s // tq),
            in_specs=[pl.BlockSpec((1, tq, LANE), lambda bi, h, qi: (bi, qi, h)),
                      pl.BlockSpec((1, s, LANE), lambda bi, h, qi: (bi, 0, kmap(h))),
                      pl.BlockSpec((1, nkc, V_ROWS, tk), lambda bi, h, qi: (bi, 0, vmap(h), 0))],
            out_specs=pl.BlockSpec((1, dv, tq), lambda bi, h, qi: (bi, h, qi)),
            out_shape=jax.ShapeDtypeStruct((b, heads * dv, s), F32),
            scratch_shapes=scratch,
            compiler_params=_cparams("parallel", "parallel", "arbitrary"),
            name=name,
        )

    acc = pltpu.VMEM((V_ROWS, tq), F32)
    fast = call(_attn_noshift_kernel, [acc], "attention")
    safe = call(_attn_online_kernel, [pltpu.VMEM((1, tq), F32), acc], "attention_online")
    return lax.cond(score_bound < NOSHIFT_MAX_LOG2, fast, safe, q, k, vt)


def _ssd_kernel(*refs, rev, nct, t):
    if rev:
        (main_ref, prev_ref, next_ref, yf_ref, cw_ref, cb_ref, dtb_ref, alog_ref, dsk_ref, nw_ref,
         out_ref, h_scr) = refs
    else:
        (main_ref, prev_ref, next_ref, cw_ref, cb_ref, dtb_ref, alog_ref, out_ref, h_scr) = refs
    i = pl.program_id(1)
    c = (nct - 1 - i) if rev else i

    @pl.when(i == 0)
    def _():
        h_scr[...] = jnp.zeros_like(h_scr)

    main = main_ref[...]
    xlo, xhi = SSD_INNER, SSD_INNER + SSD_CONV_DIM
    prev = jnp.where(c > 0, prev_ref[:, xlo:xhi], 0.0)
    nxt = jnp.where(c < nct - 1, next_ref[:, xlo:xhi], 0.0)
    xp = jnp.concatenate([prev, main[:, xlo:xhi], nxt], axis=0)
    acc = jnp.broadcast_to(cb_ref[...], (t, SSD_CONV_DIM))
    for j in range(SSD_CONV):
        sh = (SSD_CONV // 2 - j) % (t + 2 * SUBLANE)
        r = xp if sh == 0 else pltpu.roll(xp, sh, 0)
        acc = acc + r[SUBLANE:SUBLANE + t] * cw_ref[j:j + 1, :]
    xc = _silu(acc)
    xs = xc[:, :SSD_INNER]
    gn = SSD_GROUPS * SSD_STATE
    bmat = xc[:, SSD_INNER:SSD_INNER + gn]
    cmat = xc[:, SSD_INNER + gn:]

    dtr = main[:, xhi:] + dtb_ref[...]
    dt = jnp.maximum(dtr, 0.0) + jnp.log1p(jnp.exp(-jnp.abs(dtr)))
    a = dt * (-jnp.exp(alog_ref[...]))

    row = lax.broadcasted_iota(I32, (t, t), 0)
    col = lax.broadcasted_iota(I32, (t, t), 1)
    lower = row >= col
    tri_l = jnp.where(lower, 1.0, 0.0).astype(BF16)
    tri_u = jnp.where(row <= col, 1.0, 0.0).astype(BF16)
    a_t = a.T
    cs = _dot_split_rhs(tri_l, a, 3)
    cs_t = _dot_split_lhs(a_t, tri_u, 3)
    tot = cs[t - 1:t, :]
    if rev:
        ecol = cs - a
        erow = cs_t - a_t
    lane = lax.broadcasted_iota(I32, (1, LANE), 1)
    hb = SSD_HEADS if rev else 0
    hpg = SSD_HEADS // SSD_GROUPS

    g = []
    bt = []
    for grp in range(SSD_GROUPS):
        bg = bmat[:, grp * SSD_STATE:(grp + 1) * SSD_STATE]
        cg = cmat[:, grp * SSD_STATE:(grp + 1) * SSD_STATE].astype(BF16)
        g.append((_dot_nt(cg, bg.astype(BF16)), cg))
        bt.append(bg.T.astype(BF16))

    for pair in range(SSD_HEADS // 2):
        xpair = xs[:, pair * LANE:(pair + 1) * LANE]
        hstate = h_scr[pair]
        hbf = hstate.astype(BF16)
        y = jnp.zeros((t, LANE), F32)
        hnew = jnp.zeros((SSD_STATE, LANE), F32)
        dec = jnp.zeros((1, LANE), F32)
        for sub in range(2):
            h = pair * 2 + sub
            grp = h // hpg
            gmat, cg = g[grp]
            keep = (lane >= sub * SSD_HEAD_DIM) & (lane < (sub + 1) * SSD_HEAD_DIM)
            hl = hb + h
            dtc = dt[:, hl:hl + 1]
            xm = jnp.where(keep, xpair * dtc, 0.0)
            if rev:
                dmat = jnp.exp(jnp.where(row <= col, erow[hl:hl + 1, :] - ecol[:, hl:hl + 1], -jnp.inf))
                off_scale = jnp.exp(tot[:, hl:hl + 1] - ecol[:, hl:hl + 1])
                st_w = jnp.exp(ecol[:, hl:hl + 1])
            else:
                dmat = jnp.exp(jnp.where(lower, cs[:, hl:hl + 1] - cs_t[hl:hl + 1, :], -jnp.inf))
                off_scale = jnp.exp(cs[:, hl:hl + 1])
                st_w = jnp.exp(tot[:, hl:hl + 1] - cs[:, hl:hl + 1])
            y = y + _dot((gmat * dmat).astype(BF16), xm.astype(BF16))
            y = y + jnp.where(keep, _dot(cg, hbf) * off_scale, 0.0)
            hnew = hnew + _dot(bt[grp], (xm * st_w).astype(BF16))
            dec = dec + jnp.where(keep, jnp.exp(tot[:, hl:hl + 1]), 0.0)
        h_scr[pair] = hstate * dec + hnew
        if rev:
            out_ref[:, pair * LANE:(pair + 1) * LANE] = y
        else:
            out_ref[:, pair * LANE:(pair + 1) * LANE] = y

    if rev:
        yall = out_ref[...] + yf_ref[...] + xs * dsk_ref[...]
        gt = yall * _silu(main[:, :SSD_INNER])
        out_ref[...] = gt * lax.rsqrt(jnp.mean(gt * gt, axis=-1, keepdims=True) + EPS) * nw_ref[...]


def _ssd_pass(bin_, yf, cw, cb, dtb, alog, dsk, nw, batch, seq, rev):
    t = SSD_T
    nct = seq // t
    hb = t // SUBLANE
    nrow8 = batch * seq // SUBLANE

    def cidx(i):
        return (nct - 1 - i) if rev else i

    main = pl.BlockSpec((t, SSD_IN), lambda b, i: (b * nct + cidx(i), 0))
    prev = pl.BlockSpec((SUBLANE, SSD_IN),
                        lambda b, i: (jnp.maximum((b * nct + cidx(i)) * hb - 1, 0), 0))
    nxt = pl.BlockSpec((SUBLANE, SSD_IN),
                       lambda b, i: (jnp.minimum((b * nct + cidx(i) + 1) * hb, nrow8 - 1), 0))
    full = lambda r, c: pl.BlockSpec((r, c), lambda b, i: (0, 0))
    yspec = pl.BlockSpec((t, SSD_INNER), lambda b, i: (b * nct + cidx(i), 0))
    if rev:
        in_specs = [main, prev, nxt, yspec, full(SSD_CONV, SSD_CONV_DIM), full(1, SSD_CONV_DIM),
                    full(1, LANE), full(1, LANE), full(1, SSD_INNER), full(1, SSD_INNER)]
        args = (bin_, bin_, bin_, yf, cw, cb, dtb, alog, dsk, nw)
    else:
        in_specs = [main, prev, nxt, full(SSD_CONV, SSD_CONV_DIM), full(1, SSD_CONV_DIM),
                    full(1, LANE), full(1, LANE)]
        args = (bin_, bin_, bin_, cw, cb, dtb, alog)
    return pl.pallas_call(
        functools.partial(_ssd_kernel, rev=rev, nct=nct, t=t),
        grid=(batch, nct),
        in_specs=in_specs,
        out_specs=yspec,
        out_shape=jax.ShapeDtypeStruct((batch * seq, SSD_INNER), F32),
        scratch_shapes=[pltpu.VMEM((SSD_HEADS // 2, SSD_STATE, LANE), F32)],
        compiler_params=_cparams("parallel", "arbitrary"),
        name="ssd_bwd" if rev else "ssd_fwd",
    )(*args)


def _outproj_kernel(x_ref, mla_ref, ssd_ref, dif_ref, lam_ref, sub_ref, wo_ref, fn_ref, wrh_ref, wrl_ref,
                    xo_ref, hn_ref, aff_ref, *, lambda_init):
    lam = lam_ref[...]
    lam_full = (jnp.exp(jnp.sum(lam[0:1] * lam[1:2], keepdims=True))
                - jnp.exp(jnp.sum(lam[2:3] * lam[3:4], keepdims=True)) + lambda_init)
    dif = dif_ref[0]
    outs = []
    for hd in range(DIFF_HEADS):
        o = dif[2 * hd * DIFF_V:(2 * hd + 1) * DIFF_V] - lam_full * dif[(2 * hd + 1) * DIFF_V:(2 * hd + 2) * DIFF_V]
        o = o * lax.rsqrt(jnp.mean(o * o, axis=0, keepdims=True) + EPS) * sub_ref[...] * (1.0 - lambda_init)
        outs.append(o)
    dt_ = jnp.concatenate(outs, axis=0).astype(BF16)
    wo = wo_ref[...]
    x = x_ref[...]
    x = x + _dot_tn(mla_ref[0].astype(BF16), wo[:MLA_WIDTH])
    x = x + _dot(ssd_ref[...].astype(BF16), wo[MLA_WIDTH:MLA_WIDTH + SSD_INNER])
    x = x + _dot_tn(dt_, wo[MLA_WIDTH + SSD_INNER:])
    xo_ref[...] = x
    hn = x * lax.rsqrt(jnp.mean(x * x, axis=-1, keepdims=True) + EPS) * fn_ref[...]
    hn_ref[...] = hn
    hh, hl = _split_bf16(hn, 2)
    logits = _dot(hh, wrh_ref[...]) + _dot(hl, wrh_ref[...]) + _dot(hh, wrl_ref[...])
    lane = lax.broadcasted_iota(I32, logits.shape, 1)
    logits = jnp.where(lane < N_EXPERTS, logits, -jnp.inf)
    e = jnp.exp(logits - jnp.max(logits, axis=-1, keepdims=True))
    aff = e / jnp.sum(e, axis=-1, keepdims=True)
    aff_ref[...] = aff.T[:N_EXPERTS]


def _outproj(x2, mla_t, ssd, dif_t, lam, sub, wo, fn, wrh, wrl, seq, lambda_init):
    t = x2.shape[0]
    tm = TOKEN_TILE
    nt = seq // tm
    full = lambda r, c: pl.BlockSpec((r, c), lambda i: (0, 0))
    row = lambda c: pl.BlockSpec((tm, c), lambda i: (i, 0))
    return pl.pallas_call(
        functools.partial(_outproj_kernel, lambda_init=lambda_init),
        grid=(t // tm,),
        in_specs=[row(D_MODEL),
                  pl.BlockSpec((1, MLA_WIDTH, tm), lambda i: (i // nt, 0, i % nt)),
                  row(SSD_INNER),
                  pl.BlockSpec((1, 2 * DIFF_WIDTH, tm), lambda i: (i // nt, 0, i % nt)),
                  full(4, DIFF_QK), full(DIFF_V, 1), full(D_MODEL, D_MODEL), full(1, D_MODEL),
                  full(D_MODEL, LANE), full(D_MODEL, LANE)],
        out_specs=[row(D_MODEL), row(D_MODEL), pl.BlockSpec((N_EXPERTS, tm), lambda i: (0, i))],
        out_shape=[jax.ShapeDtypeStruct((t, D_MODEL), F32),
                   jax.ShapeDtypeStruct((t, D_MODEL), F32),
                   jax.ShapeDtypeStruct((N_EXPERTS, t), F32)],
        compiler_params=_cparams("parallel"),
        name="outproj",
    )(x2, mla_t, ssd, dif_t, lam, sub, wo, fn, wrh, wrl)


def _cumsum_rowmajor(mask_f32, tri_u, tri_ls):
    local = _dot(mask_f32.astype(BF16), tri_u)
    rowtot = jnp.broadcast_to(local[:, LANE - 1:LANE], local.shape).astype(BF16)
    return local + _dot(tri_ls, rowtot)


def _select_kernel(aff_ref, pos_ref, idx_ref, toff_ref, tcnt_ref, cum_scr, *, cap, sb, rows):
    a = aff_ref[0]
    key = lax.bitcast_convert_type(a, I32)

    def bit_body(it, prefix):
        cand = prefix | lax.shift_left(jnp.int32(1), 30 - it)
        cnt = jnp.sum((key >= cand).astype(I32))
        return jnp.where(cnt >= cap, cand, prefix)

    thr = lax.fori_loop(0, 31, bit_body, jnp.int32(0))
    gt = key > thr
    eq = key == thr
    need = cap - jnp.sum(gt.astype(I32))
    r_i = lax.broadcasted_iota(I32, (LANE, LANE), 0)
    c_i = lax.broadcasted_iota(I32, (LANE, LANE), 1)
    tri_u = jnp.where(r_i <= c_i, 1.0, 0.0).astype(BF16)
    r_r = lax.broadcasted_iota(I32, (rows, rows), 0)
    c_r = lax.broadcasted_iota(I32, (rows, rows), 1)
    tri_ls = jnp.where(r_r > c_r, 1.0, 0.0).astype(BF16)
    eqf = jnp.where(eq, 1.0, 0.0)
    tie_rank = _cumsum_rowmajor(eqf, tri_u, tri_ls) - eqf
    sel = gt | (eq & (tie_rank < need.astype(F32)))
    self_ = jnp.where(sel, 1.0, 0.0)
    cum = _cumsum_rowmajor(self_, tri_u, tri_ls).astype(I32)
    pos_ref[0] = jnp.where(sel, cum - 1, -1)
    cum_scr[...] = cum
    rowend = cum[:, LANE - 1:LANE]
    rowstart = rowend - jnp.sum(self_, axis=-1, keepdims=True).astype(I32)
    toff_ref[0] = rowstart
    tcnt_ref[0] = rowend - rowstart

    def blk(bi, carry):
        p0 = bi * sb
        r_lo = jnp.sum((rowend <= p0).astype(I32))
        r_hi = jnp.sum((rowstart < p0 + sb).astype(I32))
        s_col = p0 + lax.broadcasted_iota(I32, (sb, LANE), 0)

        def rowbody(r, acc):
            return acc + (cum_scr[pl.ds(r, 1), :] <= s_col).astype(I32)

        acc = lax.fori_loop(r_lo, r_hi, rowbody, jnp.zeros((sb, LANE), I32))
        idx_ref[0, pl.ds(pl.multiple_of(p0, sb), sb), :] = jnp.sum(acc, axis=1, keepdims=True) + r_lo * LANE
        return carry

    lax.fori_loop(0, cap // sb, blk, 0)


def _select(aff3, cap):
    e, rows, _ = aff3.shape
    sb = min(IDX_SLOTS, cap)
    return pl.pallas_call(
        functools.partial(_select_kernel, cap=cap, sb=sb, rows=rows),
        grid=(e,),
        in_specs=[pl.BlockSpec((1, rows, LANE), lambda i: (i, 0, 0))],
        out_specs=[pl.BlockSpec((1, rows, LANE), lambda i: (i, 0, 0)),
                   pl.BlockSpec((1, cap, 1), lambda i: (i, 0, 0)),
                   pl.BlockSpec((1, rows, 1), lambda i: (i, 0, 0)),
                   pl.BlockSpec((1, rows, 1), lambda i: (i, 0, 0))],
        out_shape=[jax.ShapeDtypeStruct((e, rows, LANE), I32),
                   jax.ShapeDtypeStruct((e, cap, 1), I32),
                   jax.ShapeDtypeStruct((e, rows, 1), I32),
                   jax.ShapeDtypeStruct((e, rows, 1), I32)],
        scratch_shapes=[pltpu.VMEM((rows, LANE), I32)],
        compiler_params=_cparams("parallel"),
        name="select",
    )(aff3)


def _row_copy(src_hbm, src_row, dst_vmem, dst_row, sem):
    return pltpu.make_async_copy(src_hbm.at[pl.ds(src_row, 1)], dst_vmem.at[pl.ds(dst_row, 1)], sem)


def _ffn_kernel(idx_ref, nxt_ref, hn_hbm, wg_ref, wu_ref, wd_ref, ye_ref, xbuf, sem, *, slots, steps):
    n = pl.program_id(0) * pl.num_programs(1) + pl.program_id(1)
    cur = n % 2

    def gather(iref, buf_slot):
        for r in range(slots):
            _row_copy(hn_hbm, iref[0, 0, r], xbuf.at[buf_slot], r, sem.at[buf_slot]).start()

    def wait_all(buf_slot):
        pltpu.make_async_copy(hn_hbm.at[pl.ds(0, slots)], xbuf.at[buf_slot], sem.at[buf_slot]).wait()

    @pl.when(n == 0)
    def _():
        gather(idx_ref, 0)

    gather(nxt_ref, 1 - cur)
    wait_all(cur)
    x = xbuf[cur].astype(BF16)
    gate = _dot(x, wg_ref[0])
    up = _dot(x, wu_ref[0])
    hid = (_silu(gate) * up).astype(BF16)
    ye_ref[...] = _dot(hid, wd_ref[0])

    @pl.when(n == steps - 1)
    def _():
        wait_all(1 - cur)


def _ffn(idx3, hn, wg, wu, wd, cap):
    slots = idx3.shape[-1]
    nb = cap // slots
    steps = N_EXPERTS * nb
    wspec = lambda: pl.BlockSpec((1, D_MODEL, EXPERT_FF), lambda e, b: (e, 0, 0))
    ispec = lambda f: pl.BlockSpec((1, 1, slots), f, memory_space=pltpu.SMEM)
    return pl.pallas_call(
        functools.partial(_ffn_kernel, slots=slots, steps=steps),
        grid=(N_EXPERTS, nb),
        in_specs=[ispec(lambda e, b: (e * nb + b, 0, 0)),
                  ispec(lambda e, b: (jnp.minimum(e * nb + b + 1, steps - 1), 0, 0)),
                  pl.BlockSpec(memory_space=pl.ANY),
                  wspec(), wspec(),
                  pl.BlockSpec((1, EXPERT_FF, D_MODEL), lambda e, b: (e, 0, 0))],
        out_specs=pl.BlockSpec((slots, D_MODEL), lambda e, b: (e * nb + b, 0)),
        out_shape=jax.ShapeDtypeStruct((N_EXPERTS * cap, D_MODEL), F32),
        scratch_shapes=[pltpu.VMEM((2, slots, D_MODEL), F32), pltpu.SemaphoreType.DMA((2,))],
        compiler_params=_cparams("arbitrary", "arbitrary"),
        name="expert_ffn",
    )(idx3, idx3, hn, wg, wu, wd)


COMBINE_MAX_BLOCKS = N_EXPERTS * (COMBINE_TILE // SUBLANE + 1)
COMBINE_MAX_ROWS = -(-COMBINE_MAX_BLOCKS * SUBLANE // 256) * 256


def _combine_kernel(toff_ref, tcnt_ref, x_ref, pos_ref, aff_ref, ye_hbm, out_ref, buf, sem, *, cap, rows, chunk):
    r = pl.program_id(0)
    cur = r % 2

    def layout(tile):
        firsts, nblks, dsts = [], [], []
        dst = jnp.int32(0)
        for e in range(N_EXPERTS):
            off = toff_ref[e * rows + tile]
            cnt = tcnt_ref[e * rows + tile]
            first = (off >> 3) << 3
            firsts.append(first)
            nblks.append(jnp.where(cnt > 0, (off + cnt - first + SUBLANE - 1) >> 3, 0))
            dsts.append(dst)
            dst = dst + nblks[-1]
        return firsts, nblks, dsts, dst

    def gather(tile, buf_slot):
        firsts, nblks, dsts, _ = layout(tile)
        for e in range(N_EXPERTS):
            def issue(j, carry, e=e):
                src = pl.multiple_of(e * cap + firsts[e] + j * SUBLANE, SUBLANE)
                dst = pl.multiple_of((dsts[e] + j) * SUBLANE, SUBLANE)
                pltpu.make_async_copy(ye_hbm.at[pl.ds(src, SUBLANE)], buf.at[buf_slot, pl.ds(dst, SUBLANE)],
                                      sem.at[buf_slot]).start()
                return carry

            lax.fori_loop(0, nblks[e], issue, 0)

    @pl.when(r == 0)
    def _():
        gather(0, 0)

    @pl.when(r + 1 < rows)
    def _():
        gather(r + 1, 1 - cur)

    firsts, _, dsts, nblk_total = layout(r)
    total = nblk_total * SUBLANE
    for bit in range(COMBINE_MAX_BLOCKS.bit_length()):
        @pl.when(((nblk_total >> bit) & 1) == 1)
        def _(bit=bit):
            n = SUBLANE << bit
            pltpu.make_async_copy(ye_hbm.at[pl.ds(0, n)], buf.at[cur, pl.ds(0, n)], sem.at[cur]).wait()

    out_ref[...] = x_ref[...]
    pos = pos_ref[...]
    aff = aff_ref[...]
    tgt = [jnp.where(pos[e:e + 1] >= 0, pos[e:e + 1] - firsts[e] + dsts[e] * SUBLANE, -1)
           for e in range(N_EXPERTS)]

    def chunk_body(k, carry):
        base = pl.multiple_of(k * chunk, chunk)
        cidx = base + lax.broadcasted_iota(I32, (chunk, COMBINE_TILE), 0)
        wt = jnp.zeros((chunk, COMBINE_TILE), F32)
        for e in range(N_EXPERTS):
            wt = wt + jnp.where(cidx == tgt[e], aff[e:e + 1], 0.0)
        rowid = base + lax.broadcasted_iota(I32, (chunk, 1), 0)
        rowsv = jnp.where(rowid < total, buf[cur, pl.ds(base, chunk), :], 0.0).astype(BF16)
        wh, wl = _split_bf16(wt, 2)
        out_ref[...] += _dot_tn(wh, rowsv) + _dot_tn(wl, rowsv)
        return carry

    lax.fori_loop(0, (total + chunk - 1) // chunk, chunk_body, 0)


def _combine(toff, tcnt, x2, pos, aff_t, ye, cap):
    t = x2.shape[0]
    tile = COMBINE_TILE
    rows = t // tile
    chunk = 256
    grid_spec = pltpu.PrefetchScalarGridSpec(
        num_scalar_prefetch=2,
        grid=(rows,),
        in_specs=[pl.BlockSpec((tile, D_MODEL), lambda i, *_: (i, 0)),
                  pl.BlockSpec((N_EXPERTS, tile), lambda i, *_: (0, i)),
                  pl.BlockSpec((N_EXPERTS, tile), lambda i, *_: (0, i)),
                  pl.BlockSpec(memory_space=pl.ANY)],
        out_specs=pl.BlockSpec((tile, D_MODEL), lambda i, *_: (i, 0)),
        scratch_shapes=[pltpu.VMEM((2, COMBINE_MAX_ROWS, D_MODEL), F32), pltpu.SemaphoreType.DMA((2,))],
    )
    return pl.pallas_call(
        functools.partial(_combine_kernel, cap=cap, rows=rows, chunk=chunk),
        grid_spec=grid_spec,
        out_shape=jax.ShapeDtypeStruct((t, D_MODEL), F32),
        compiler_params=_cparams("arbitrary"),
        name="combine",
    )(toff, tcnt, x2, pos, aff_t, ye)


def _rope_tables(seq):
    inv = ROPE_THETA ** (-jnp.arange(0, MLA_ROPE, 2, dtype=F32) / MLA_ROPE)
    ang = jnp.arange(seq, dtype=F32)[:, None] * inv[None, :]
    cos, sin = jnp.cos(ang), jnp.sin(ang)
    one = jnp.ones((seq, 1), F32)
    zero = jnp.zeros((seq, 1), F32)
    rep = lambda v, n: jnp.broadcast_to(v, (seq, n))
    half = MLA_ROPE // 2
    mla = (jnp.concatenate([rep(one, MLA_NOPE), cos, cos, rep(one, LANE - MLA_QK)], axis=1),
           jnp.concatenate([rep(zero, MLA_NOPE + half), sin, rep(zero, LANE - MLA_QK)], axis=1),
           jnp.concatenate([rep(zero, MLA_NOPE), -sin, rep(zero, LANE - MLA_NOPE - half)], axis=1))
    nblk = LANE // DIFF_QK
    dif = (jnp.tile(jnp.concatenate([cos, cos], axis=1), (1, nblk)),
           jnp.tile(jnp.concatenate([rep(zero, half), sin], axis=1), (1, nblk)),
           jnp.tile(jnp.concatenate([-sin, rep(zero, half)], axis=1), (1, nblk)))
    return mla, dif


def _pad_cols(w, width):
    return jnp.pad(w, ((0, 0), (0, width - w.shape[1])))


def _layer_params(p, l):
    w_in = p['w_in'][l]
    z64 = jnp.zeros((D_MODEL, 64), F32)
    z32 = jnp.zeros((D_MODEL, 32), F32)
    w_proj = jnp.concatenate([
        w_in[:, :OFF_KPE], z64, w_in[:, OFF_KPE:OFF_Z], z32,
        w_in[:, OFF_Z:OFF_DT], _pad_cols(w_in[:, OFF_DT:OFF_DQ], LANE),
        w_in[:, OFF_DQ:]], axis=1).astype(BF16)
    wq = p['mla_w_uq'][l].reshape(MLA_Q_LORA, MLA_HEADS, MLA_QK)
    wq = jnp.pad(wq, ((0, 0), (0, 0), (0, LANE - MLA_QK))).reshape(MLA_Q_LORA, MLA_HEADS * LANE)
    wkv = p['mla_w_ukv'][l].reshape(MLA_KV_LORA, MLA_HEADS, MLA_NOPE + MLA_V)
    wk = jnp.pad(wkv[:, :, :MLA_NOPE], ((0, 0), (0, 0), (0, LANE - MLA_NOPE))).reshape(MLA_KV_LORA, MLA_HEADS * LANE)
    wv = wkv[:, :, MLA_NOPE:].reshape(MLA_KV_LORA, MLA_WIDTH)
    wr = _pad_cols(p['w_router'][l], LANE)
    wrh = wr.astype(BF16)
    wrl = (wr - wrh.astype(F32)).astype(BF16)
    pad_row = lambda v, n: jnp.pad(v, (0, n - v.shape[0]))[None, :]
    return dict(
        attn_norm=p['attn_norm'][l][None, :], w_proj=w_proj,
        q_norm=p['mla_q_norm'][l][None, :], kv_norm=p['mla_kv_norm'][l][None, :],
        wq=wq.astype(BF16), wk=wk.astype(BF16), wv=wv.astype(BF16),
        q_gain=pad_row(p['mla_q_gain'][l], LANE), k_gain=pad_row(p['mla_k_gain'][l], LANE),
        conv_w=p['ssd_conv_w'][l], conv_b=p['ssd_conv_b'][l][None, :],
        dt_bias=pad_row(p['ssd_dt_bias'][l].reshape(-1), LANE),
        a_log=pad_row(p['ssd_a_log'][l].reshape(-1), LANE),
        d_skip=jnp.repeat(p['ssd_d'][l], SSD_HEAD_DIM)[None, :], ssd_norm=p['ssd_norm'][l][None, :],
        dq_gain=jnp.tile(p['diff_q_gain'][l], 2 * DIFF_HEADS)[None, :],
        dk_gain=jnp.tile(p['diff_k_gain'][l], 2 * DIFF_HEADS)[None, :],
        lam=p['diff_lambda'][l], subln=p['diff_subln'][l][:, None],
        w_out=p['w_out'][l].astype(BF16), ffn_norm=p['ffn_norm'][l][None, :],
        wrh=wrh, wrl=wrl,
        w_gate=p['w_gate'][l].astype(BF16), w_up=p['w_up'][l].astype(BF16), w_down=p['w_down'][l].astype(BF16),
    )


def _block_diag_ones(n, blk):
    i = jnp.arange(n)
    return (i[:, None] // blk == i[None, :] // blk).astype(BF16)


def _trunk(x, p):
    batch, seq, _ = x.shape
    t = batch * seq
    cap = EC_CAPACITY * t // N_EXPERTS
    tk = min(ATTN_TK, seq)
    mla_tabs, dif_tabs = _rope_tables(seq)
    bd = _block_diag_ones(2 * DIFF_HEADS * DIFF_QK, DIFF_QK)
    x2 = x.reshape(t, D_MODEL)
    rows = t // LANE
    for l in range(DEPTH):
        lp = _layer_params(p, l)
        a_in, b_in, c_in = _inproj(x2, lp['attn_norm'], lp['w_proj'])
        q, k, vt = _mla_prep(a_in, lp['q_norm'], lp['kv_norm'], lp['wq'], lp['wk'], lp['wv'],
                             lp['q_gain'], lp['k_gain'], mla_tabs, seq)
        mla_bound = (MLA_QK ** 0.5 * LOG2E) * jnp.max(jnp.abs(p['mla_q_gain'][l])) * jnp.max(jnp.abs(p['mla_k_gain'][l]))
        mla_t = _attention(q.reshape(batch, seq, -1), k.reshape(batch, seq, -1),
                           vt.reshape(batch, seq // tk, MLA_HEADS * V_ROWS, tk),
                           MLA_HEADS, lambda h: h, lambda h: h, MLA_V, tk, mla_bound)
        dq, dk, dvt = _diff_prep(c_in, lp['dq_gain'], lp['dk_gain'], bd, dif_tabs, seq)
        dif_bound = (DIFF_QK ** 0.5 * LOG2E) * jnp.max(jnp.abs(p['diff_q_gain'][l])) * jnp.max(jnp.abs(p['diff_k_gain'][l]))
        dif_t = _attention(dq.reshape(batch, seq, -1), dk.reshape(batch, seq, -1),
                           dvt.reshape(batch, seq // tk, DIFF_HEADS * V_ROWS, tk),
                           2 * DIFF_HEADS, lambda h: h // (LANE // DIFF_QK), lambda h: h // 2, DIFF_V, tk,
                           dif_bound)
        yf = _ssd_pass(b_in, None, lp['conv_w'], lp['conv_b'], lp['dt_bias'], lp['a_log'],
                       None, None, batch, seq, rev=False)
        ssd = _ssd_pass(b_in, yf, lp['conv_w'], lp['conv_b'], lp['dt_bias'], lp['a_log'],
                        lp['d_skip'], lp['ssd_norm'], batch, seq, rev=True)
        lambda_init = 0.8 - 0.6 * math.exp(-0.3 * l)
        x2, hn, aff_t = _outproj(x2, mla_t, ssd, dif_t, lp['lam'], lp['subln'], lp['w_out'],
                                 lp['ffn_norm'], lp['wrh'], lp['wrl'], seq, lambda_init)
        pos3, idx, toff, tcnt = _select(aff_t.reshape(N_EXPERTS, rows, LANE), cap)
        slots = min(FFN_SLOTS, cap)
        ye = _ffn(idx.reshape(N_EXPERTS * cap // slots, 1, slots), hn,
                  lp['w_gate'], lp['w_up'], lp['w_down'], cap)
        x2 = _combine(toff.reshape(-1), tcnt.reshape(-1), x2, pos3.reshape(N_EXPERTS, t), aff_t, ye, cap)
    return x2.reshape(batch, seq, D_MODEL)


def kernel(x_prompt, x_sample, attn_norm, w_in, mla_q_norm, mla_kv_norm, mla_w_uq, mla_w_ukv, mla_q_gain,
           mla_k_gain, ssd_conv_w, ssd_conv_b, ssd_dt_bias, ssd_a_log, ssd_d, ssd_norm, diff_q_gain,
           diff_k_gain, diff_lambda, diff_subln, w_out, ffn_norm, w_router, w_gate, w_up, w_down):
    p = dict(attn_norm=attn_norm, w_in=w_in, mla_q_norm=mla_q_norm, mla_kv_norm=mla_kv_norm,
             mla_w_uq=mla_w_uq, mla_w_ukv=mla_w_ukv, mla_q_gain=mla_q_gain, mla_k_gain=mla_k_gain,
             ssd_conv_w=ssd_conv_w, ssd_conv_b=ssd_conv_b, ssd_dt_bias=ssd_dt_bias, ssd_a_log=ssd_a_log,
             ssd_d=ssd_d, ssd_norm=ssd_norm, diff_q_gain=diff_q_gain, diff_k_gain=diff_k_gain,
             diff_lambda=diff_lambda, diff_subln=diff_subln, w_out=w_out, ffn_norm=ffn_norm,
             w_router=w_router, w_gate=w_gate, w_up=w_up, w_down=w_down)
    return _trunk(x_prompt, p), _trunk(x_sample, p)
```

```python
import functools
import math

import jax
import jax.numpy as jnp
from jax import lax
from jax.experimental import pallas as pl
from jax.experimental.pallas import tpu as pltpu

F32 = jnp.float32
BF16 = jnp.bfloat16
I32 = jnp.int32

D_MODEL = 1024
DEPTH = 2
EPS = 1e-6
ROPE_THETA = 10000.0

MLA_HEADS = 6
MLA_Q_LORA = 256
MLA_KV_LORA = 128
MLA_NOPE = 64
MLA_ROPE = 32
MLA_QK = MLA_NOPE + MLA_ROPE
MLA_V = 64
MLA_WIDTH = MLA_HEADS * MLA_V

SSD_HEADS = 6
SSD_HEAD_DIM = 64
SSD_INNER = SSD_HEADS * SSD_HEAD_DIM
SSD_GROUPS = 2
SSD_STATE = 128
SSD_CONV = 5
SSD_CONV_DIM = SSD_INNER + 2 * SSD_GROUPS * SSD_STATE

DIFF_HEADS = 4
DIFF_QK = 32
DIFF_V = 2 * DIFF_QK
DIFF_WIDTH = DIFF_HEADS * DIFF_V

N_EXPERTS = 16
EC_CAPACITY = 2
EXPERT_FF = 1024

OFF_KV = MLA_Q_LORA
OFF_KPE = OFF_KV + MLA_KV_LORA
OFF_Z = OFF_KPE + MLA_ROPE
OFF_XBC = OFF_Z + SSD_INNER
OFF_DT = OFF_XBC + SSD_CONV_DIM
OFF_DQ = OFF_DT + 2 * SSD_HEADS
OFF_DK = OFF_DQ + DIFF_HEADS * 2 * DIFF_QK
OFF_DV = OFF_DK + DIFF_HEADS * 2 * DIFF_QK
IN_COLS = OFF_DV + DIFF_WIDTH

LANE = 128
SUBLANE = 8

MLA_IN = 512
SSD_IN = SSD_INNER + SSD_CONV_DIM + LANE
DIFF_IN = 768
PROJ_COLS = MLA_IN + SSD_IN + DIFF_IN

TOKEN_TILE = 512
ATTN_TQ = 512
ATTN_TK = 512
ATTN_GROUP = 16
SSD_T = 256
COMBINE_TILE = 128
V_ROWS = 80
LOG2E = math.log2(math.e)
NOSHIFT_MAX_LOG2 = 60.0
FFN_SLOTS = 256
IDX_SLOTS = 256
VMEM_LIMIT = 56 * 1024 * 1024


def _cparams(*sem):
    return pltpu.CompilerParams(dimension_semantics=sem, vmem_limit_bytes=VMEM_LIMIT)


def _split_bf16(a, terms):
    parts = []
    rem = a
    for _ in range(terms):
        p = rem.astype(BF16)
        parts.append(p)
        rem = rem - p.astype(F32)
    return parts


def _dot(a, b):
    return jnp.dot(a, b, preferred_element_type=F32)


def _dot_nt(a, b):
    return lax.dot_general(a, b, (((1,), (1,)), ((), ())), preferred_element_type=F32)


def _dot_tn(a, b):
    return lax.dot_general(a, b, (((0,), (0,)), ((), ())), preferred_element_type=F32)


def _dot_split_lhs(a_f32, b_bf16, terms):
    out = None
    for p in _split_bf16(a_f32, terms):
        d = _dot(p, b_bf16)
        out = d if out is None else out + d
    return out


def _dot_split_rhs(a_bf16, b_f32, terms):
    out = None
    for p in _split_bf16(b_f32, terms):
        d = _dot(a_bf16, p)
        out = d if out is None else out + d
    return out


def _silu(x):
    return x * jax.nn.sigmoid(x)


def _inproj_kernel(x_ref, g_ref, w_ref, a_ref, b_ref, c_ref):
    x = x_ref[...]
    ms = jnp.mean(x * x, axis=-1, keepdims=True)
    h = (x * lax.rsqrt(ms + EPS) * g_ref[...]).astype(BF16)
    y = _dot(h, w_ref[...])
    a_ref[...] = y[:, :MLA_IN]
    b_ref[...] = y[:, MLA_IN:MLA_IN + SSD_IN]
    c_ref[...] = y[:, MLA_IN + SSD_IN:]


def _inproj(x2, gain, w):
    t = x2.shape[0]
    tm = TOKEN_TILE
    return pl.pallas_call(
        _inproj_kernel,
        grid=(t // tm,),
        in_specs=[pl.BlockSpec((tm, D_MODEL), lambda i: (i, 0)),
                  pl.BlockSpec((1, D_MODEL), lambda i: (0, 0)),
                  pl.BlockSpec((D_MODEL, PROJ_COLS), lambda i: (0, 0))],
        out_specs=[pl.BlockSpec((tm, MLA_IN), lambda i: (i, 0)),
                   pl.BlockSpec((tm, SSD_IN), lambda i: (i, 0)),
                   pl.BlockSpec((tm, DIFF_IN), lambda i: (i, 0))],
        out_shape=[jax.ShapeDtypeStruct((t, MLA_IN), F32),
                   jax.ShapeDtypeStruct((t, SSD_IN), F32),
                   jax.ShapeDtypeStruct((t, DIFF_IN), F32)],
        compiler_params=_cparams("parallel"),
        name="inproj",
    )(x2, gain, w)


def _rope(x, c, sa, sb, shift):
    return x * c + pltpu.roll(x, shift, 1) * sa + pltpu.roll(x, LANE - shift, 1) * sb


def _mla_prep_kernel(a_ref, qn_ref, kvn_ref, wq_ref, wk_ref, wv_ref, qg_ref, kg_ref,
                     c_ref, sa_ref, sb_ref, q_ref, k_ref, vt_ref):
    a = a_ref[...]
    cq = a[:, :MLA_Q_LORA]
    ckv = a[:, MLA_Q_LORA:MLA_Q_LORA + MLA_KV_LORA]
    kpe = a[:, MLA_Q_LORA + MLA_KV_LORA:]
    cqn = (cq * lax.rsqrt(jnp.mean(cq * cq, axis=-1, keepdims=True) + EPS) * qn_ref[...]).astype(BF16)
    ckvn = (ckv * lax.rsqrt(jnp.mean(ckv * ckv, axis=-1, keepdims=True) + EPS) * kvn_ref[...]).astype(BF16)
    q = _dot(cqn, wq_ref[...])
    kn = _dot(ckvn, wk_ref[...])
    v = _dot(ckvn, wv_ref[...])
    c = c_ref[...]
    sa = sa_ref[...]
    sb = sb_ref[...]
    scale = MLA_QK ** -0.5 * LOG2E
    for h in range(MLA_HEADS):
        sl = slice(h * LANE, (h + 1) * LANE)
        qh = q[:, sl]
        qh = qh * lax.rsqrt(jnp.sum(qh * qh, axis=-1, keepdims=True) / MLA_QK + EPS) * qg_ref[...]
        q_ref[:, sl] = (_rope(qh, c, sa, sb, MLA_ROPE // 2) * scale).astype(BF16)
        kh = kn[:, sl] + kpe
        kh = kh * lax.rsqrt(jnp.sum(kh * kh, axis=-1, keepdims=True) / MLA_QK + EPS) * kg_ref[...]
        k_ref[:, sl] = _rope(kh, c, sa, sb, MLA_ROPE // 2).astype(BF16)
    vt = v.T.astype(BF16)
    ones = jnp.ones((V_ROWS - MLA_V, vt.shape[1]), BF16)
    for h in range(MLA_HEADS):
        vt_ref[0, h * V_ROWS:(h + 1) * V_ROWS, :] = jnp.concatenate([vt[h * MLA_V:(h + 1) * MLA_V], ones], axis=0)


def _mla_prep(a, qn, kvn, wq, wk, wv, qg, kg, tabs, seq):
    t = a.shape[0]
    tm = TOKEN_TILE
    nt = seq // tm
    full = lambda r, c: pl.BlockSpec((r, c), lambda i: (0, 0))
    tab = pl.BlockSpec((tm, LANE), lambda i: (i % nt, 0))
    return pl.pallas_call(
        _mla_prep_kernel,
        grid=(t // tm,),
        in_specs=[pl.BlockSpec((tm, MLA_IN), lambda i: (i, 0)),
                  full(1, MLA_Q_LORA), full(1, MLA_KV_LORA),
                  full(MLA_Q_LORA, MLA_HEADS * LANE), full(MLA_KV_LORA, MLA_HEADS * LANE),
                  full(MLA_KV_LORA, MLA_WIDTH), full(1, LANE), full(1, LANE), tab, tab, tab],
        out_specs=[pl.BlockSpec((tm, MLA_HEADS * LANE), lambda i: (i, 0)),
                   pl.BlockSpec((tm, MLA_HEADS * LANE), lambda i: (i, 0)),
                   pl.BlockSpec((1, MLA_HEADS * V_ROWS, tm), lambda i: (i, 0, 0))],
        out_shape=[jax.ShapeDtypeStruct((t, MLA_HEADS * LANE), BF16),
                   jax.ShapeDtypeStruct((t, MLA_HEADS * LANE), BF16),
                   jax.ShapeDtypeStruct((t // tm, MLA_HEADS * V_ROWS, tm), BF16)],
        compiler_params=_cparams("parallel"),
        name="mla_prep",
    )(a, qn, kvn, wq, wk, wv, qg, kg, *tabs)


def _diff_prep_kernel(cin_ref, qg_ref, kg_ref, bd_ref, c_ref, sa_ref, sb_ref, q_ref, k_ref, vt_ref):
    cin = cin_ref[...]
    bd = bd_ref[...]
    c = jnp.concatenate([c_ref[...], c_ref[...]], axis=1)
    sa = jnp.concatenate([sa_ref[...], sa_ref[...]], axis=1)
    sb = jnp.concatenate([sb_ref[...], sb_ref[...]], axis=1)
    lane = lax.broadcasted_iota(I32, (1, LANE), 1)

    def norm_rope(x, g):
        ms = _dot_split_lhs(x * x, bd, 3) / DIFF_QK
        x = x * lax.rsqrt(ms + EPS) * g
        halves = []
        for j in range(2):
            sl = slice(j * LANE, (j + 1) * LANE)
            halves.append(_rope(x[:, sl], c[:, sl], sa[:, sl], sb[:, sl], DIFF_QK // 2))
        return halves

    qh = norm_rope(cin[:, :256], qg_ref[...])
    kh = norm_rope(cin[:, 256:512], kg_ref[...])
    scale = DIFF_QK ** -0.5 * LOG2E
    for j in range(2 * DIFF_HEADS):
        grp, sub = divmod(j, LANE // DIFF_QK)
        keep = (lane >= sub * DIFF_QK) & (lane < (sub + 1) * DIFF_QK)
        q_ref[:, j * LANE:(j + 1) * LANE] = jnp.where(keep, qh[grp] * scale, 0.0).astype(BF16)
    k_ref[...] = jnp.concatenate(kh, axis=1).astype(BF16)
    vt = cin[:, 512:].T.astype(BF16)
    ones = jnp.ones((V_ROWS - DIFF_V, vt.shape[1]), BF16)
    for h in range(DIFF_HEADS):
        vt_ref[0, h * V_ROWS:(h + 1) * V_ROWS, :] = jnp.concatenate([vt[h * DIFF_V:(h + 1) * DIFF_V], ones], axis=0)


def _diff_prep(cin, qg, kg, bd, tabs, seq):
    t = cin.shape[0]
    tm = TOKEN_TILE
    nt = seq // tm
    full = lambda r, c: pl.BlockSpec((r, c), lambda i: (0, 0))
    tab = pl.BlockSpec((tm, LANE), lambda i: (i % nt, 0))
    nq = 2 * DIFF_HEADS * LANE
    return pl.pallas_call(
        _diff_prep_kernel,
        grid=(t // tm,),
        in_specs=[pl.BlockSpec((tm, DIFF_IN), lambda i: (i, 0)),
                  full(1, 256), full(1, 256), full(256, 256), tab, tab, tab],
        out_specs=[pl.BlockSpec((tm, nq), lambda i: (i, 0)),
                   pl.BlockSpec((tm, 256), lambda i: (i, 0)),
                   pl.BlockSpec((1, DIFF_HEADS * V_ROWS, tm), lambda i: (i, 0, 0))],
        out_shape=[jax.ShapeDtypeStruct((t, nq), BF16),
                   jax.ShapeDtypeStruct((t, 256), BF16),
                   jax.ShapeDtypeStruct((t // tm, DIFF_HEADS * V_ROWS, tm), BF16)],
        compiler_params=_cparams("parallel"),
        name="diff_prep",
    )(cin, qg, kg, bd, *tabs)


def _attn_noshift_kernel(q_ref, k_ref, vt_ref, o_ref, acc_scr, l_scr, pa_scr, pb_scr, *, nkc, tk, dv):
    q = q_ref[0]

    def probs(kc, p_ref):
        k = k_ref[0, pl.ds(pl.multiple_of(kc * tk, tk), tk), :]
        p = jnp.exp2(_dot_nt(k, q))
        p_ref[...] = p.astype(BF16)
        return jnp.sum(p, axis=0, keepdims=True)

    def pv(kc, p_ref):
        return _dot(vt_ref[0, kc, :dv, :], p_ref[...])

    bufs = (pa_scr, pb_scr)
    unroll = ATTN_GROUP

    def run(first, count, last):
        tot = None
        lsum = None
        for j in range(count):
            if not (last and j == count - 1):
                ls = probs(first + j + 1, bufs[(j + 1) % 2])
                lsum = ls if lsum is None else lsum + ls
            d = pv(first + j, bufs[j % 2])
            tot = d if tot is None else tot + d
        return tot, lsum

    l_scr[...] = probs(0, pa_scr)
    acc_scr[...] = jnp.zeros_like(acc_scr)
    trips = (nkc - 1) // unroll

    def body(i, carry):
        tot, lsum = run(i * unroll, unroll, False)
        acc_scr[...] += tot
        l_scr[...] += lsum
        return carry

    lax.fori_loop(0, trips, body, 0)
    tot, lsum = run(trips * unroll, nkc - trips * unroll, True)
    den = l_scr[...] if lsum is None else l_scr[...] + lsum
    o_ref[0] = (acc_scr[...] + tot) / den


def _attn_online_kernel(q_ref, k_ref, vt_ref, o_ref, m_scr, acc_scr, *, nkc, tk, dv):
    q = q_ref[0]
    m_scr[...] = jnp.full_like(m_scr, -jnp.inf)
    acc_scr[...] = jnp.zeros_like(acc_scr)

    def body(kc, carry):
        k = k_ref[0, pl.ds(pl.multiple_of(kc * tk, tk), tk), :]
        s = _dot_nt(k, q)
        m_prev = m_scr[...]
        m_new = jnp.maximum(m_prev, jnp.max(s, axis=0, keepdims=True))
        p = jnp.exp2(s - m_new).astype(BF16)
        acc_scr[...] = jnp.exp2(m_prev - m_new) * acc_scr[...] + _dot(vt_ref[0, kc], p)
        m_scr[...] = m_new
        return carry

    lax.fori_loop(0, nkc, body, 0)
    acc = acc_scr[...]
    o_ref[0] = acc[:dv] / acc[dv:dv + 1]


def _attention(q, k, vt, heads, kmap, vmap, dv, tk, score_bound):
    b, s, _ = q.shape
    tq = min(ATTN_TQ, s)
    nkc = s // tk

    def call(body, scratch, name):
        return pl.pallas_call(
            functools.partial(body, nkc=nkc, tk=tk, dv=dv),
            grid=(b, heads, s // tq),
            in_specs=[pl.BlockSpec((1, tq, LANE), lambda bi, h, qi: (bi, qi, h)),
                      pl.BlockSpec((1, s, LANE), lambda bi, h, qi: (bi, 0, kmap(h))),
                      pl.BlockSpec((1, nkc, V_ROWS, tk), lambda bi, h, qi: (bi, 0, vmap(h), 0))],
            out_specs=pl.BlockSpec((1, dv, tq), lambda bi, h, qi: (bi, h, qi)),
            out_shape=jax.ShapeDtypeStruct((b, heads * dv, s), F32),
            scratch_shapes=scratch,
            compiler_params=_cparams("parallel", "parallel", "arbitrary"),
            name=name,
        )

    acc = pltpu.VMEM((V_ROWS, tq), F32)
    pbuf = pltpu.VMEM((tk, tq), BF16)
    fast = call(_attn_noshift_kernel, [pltpu.VMEM((dv, tq), F32), pltpu.VMEM((1, tq), F32), pbuf, pbuf],
                "attention")
    safe = call(_attn_online_kernel, [pltpu.VMEM((1, tq), F32), acc], "attention_online")
    return lax.cond(score_bound < NOSHIFT_MAX_LOG2, fast, safe, q, k, vt)


def _ssd_kernel(*refs, rev, nct, t):
    if rev:
        (main_ref, prev_ref, next_ref, yf_ref, cw_ref, cb_ref, dtb_ref, alog_ref, dsk_ref, nw_ref,
         out_ref, h_scr) = refs
    else:
        (main_ref, prev_ref, next_ref, cw_ref, cb_ref, dtb_ref, alog_ref, out_ref, h_scr) = refs
    i = pl.program_id(1)
    c = (nct - 1 - i) if rev else i

    @pl.when(i == 0)
    def _():
        h_scr[...] = jnp.zeros_like(h_scr)

    main = main_ref[...]
    xlo, xhi = SSD_INNER, SSD_INNER + SSD_CONV_DIM
    prev = jnp.where(c > 0, prev_ref[:, xlo:xhi], 0.0)
    nxt = jnp.where(c < nct - 1, next_ref[:, xlo:xhi], 0.0)
    xp = jnp.concatenate([prev, main[:, xlo:xhi], nxt], axis=0)
    acc = jnp.broadcast_to(cb_ref[...], (t, SSD_CONV_DIM))
    for j in range(SSD_CONV):
        sh = (SSD_CONV // 2 - j) % (t + 2 * SUBLANE)
        r = xp if sh == 0 else pltpu.roll(xp, sh, 0)
        acc = acc + r[SUBLANE:SUBLANE + t] * cw_ref[j:j + 1, :]
    xc = _silu(acc)
    xs = xc[:, :SSD_INNER]
    gn = SSD_GROUPS * SSD_STATE
    bmat = xc[:, SSD_INNER:SSD_INNER + gn]
    cmat = xc[:, SSD_INNER + gn:]

    dtr = main[:, xhi:] + dtb_ref[...]
    dt = jnp.maximum(dtr, 0.0) + jnp.log1p(jnp.exp(-jnp.abs(dtr)))
    a = dt * (-jnp.exp(alog_ref[...]))

    row = lax.broadcasted_iota(I32, (t, t), 0)
    col = lax.broadcasted_iota(I32, (t, t), 1)
    lower = row >= col
    tri_l = jnp.where(lower, 1.0, 0.0).astype(BF16)
    tri_u = jnp.where(row <= col, 1.0, 0.0).astype(BF16)
    a_t = a.T
    cs = _dot_split_rhs(tri_l, a, 3)
    cs_t = _dot_split_lhs(a_t, tri_u, 3)
    tot = cs[t - 1:t, :]
    if rev:
        ecol = cs - a
        erow = cs_t - a_t
    lane = lax.broadcasted_iota(I32, (1, LANE), 1)
    hb = SSD_HEADS if rev else 0
    hpg = SSD_HEADS // SSD_GROUPS

    g = []
    bt = []
    for grp in range(SSD_GROUPS):
        bg = bmat[:, grp * SSD_STATE:(grp + 1) * SSD_STATE]
        cg = cmat[:, grp * SSD_STATE:(grp + 1) * SSD_STATE].astype(BF16)
        g.append((_dot_nt(cg, bg.astype(BF16)), cg))
        bt.append(bg.T.astype(BF16))

    for pair in range(SSD_HEADS // 2):
        xpair = xs[:, pair * LANE:(pair + 1) * LANE]
        hstate = h_scr[pair]
        hbf = hstate.astype(BF16)
        y = jnp.zeros((t, LANE), F32)
        hnew = jnp.zeros((SSD_STATE, LANE), F32)
        dec = jnp.zeros((1, LANE), F32)
        for sub in range(2):
            h = pair * 2 + sub
            grp = h // hpg
            gmat, cg = g[grp]
            keep = (lane >= sub * SSD_HEAD_DIM) & (lane < (sub + 1) * SSD_HEAD_DIM)
            hl = hb + h
            dtc = dt[:, hl:hl + 1]
            xm = jnp.where(keep, xpair * dtc, 0.0)
            if rev:
                dmat = jnp.exp(jnp.where(row <= col, erow[hl:hl + 1, :] - ecol[:, hl:hl + 1], -jnp.inf))
                off_scale = jnp.exp(tot[:, hl:hl + 1] - ecol[:, hl:hl + 1])
                st_w = jnp.exp(ecol[:, hl:hl + 1])
            else:
                dmat = jnp.exp(jnp.where(lower, cs[:, hl:hl + 1] - cs_t[hl:hl + 1, :], -jnp.inf))
                off_scale = jnp.exp(cs[:, hl:hl + 1])
                st_w = jnp.exp(tot[:, hl:hl + 1] - cs[:, hl:hl + 1])
            y = y + _dot((gmat * dmat).astype(BF16), xm.astype(BF16))
            y = y + jnp.where(keep, _dot(cg, hbf) * off_scale, 0.0)
            hnew = hnew + _dot(bt[grp], (xm * st_w).astype(BF16))
            dec = dec + jnp.where(keep, jnp.exp(tot[:, hl:hl + 1]), 0.0)
        h_scr[pair] = hstate * dec + hnew
        if rev:
            out_ref[:, pair * LANE:(pair + 1) * LANE] = y
        else:
            out_ref[:, pair * LANE:(pair + 1) * LANE] = y

    if rev:
        yall = out_ref[...] + yf_ref[...] + xs * dsk_ref[...]
        gt = yall * _silu(main[:, :SSD_INNER])
        out_ref[...] = gt * lax.rsqrt(jnp.mean(gt * gt, axis=-1, keepdims=True) + EPS) * nw_ref[...]


def _ssd_pass(bin_, yf, cw, cb, dtb, alog, dsk, nw, batch, seq, rev):
    t = SSD_T
    nct = seq // t
    hb = t // SUBLANE
    nrow8 = batch * seq // SUBLANE

    def cidx(i):
        return (nct - 1 - i) if rev else i

    main = pl.BlockSpec((t, SSD_IN), lambda b, i: (b * nct + cidx(i), 0))
    prev = pl.BlockSpec((SUBLANE, SSD_IN),
                        lambda b, i: (jnp.maximum((b * nct + cidx(i)) * hb - 1, 0), 0))
    nxt = pl.BlockSpec((SUBLANE, SSD_IN),
                       lambda b, i: (jnp.minimum((b * nct + cidx(i) + 1) * hb, nrow8 - 1), 0))
    full = lambda r, c: pl.BlockSpec((r, c), lambda b, i: (0, 0))
    yspec = pl.BlockSpec((t, SSD_INNER), lambda b, i: (b * nct + cidx(i), 0))
    if rev:
        in_specs = [main, prev, nxt, yspec, full(SSD_CONV, SSD_CONV_DIM), full(1, SSD_CONV_DIM),
                    full(1, LANE), full(1, LANE), full(1, SSD_INNER), full(1, SSD_INNER)]
        args = (bin_, bin_, bin_, yf, cw, cb, dtb, alog, dsk, nw)
    else:
        in_specs = [main, prev, nxt, full(SSD_CONV, SSD_CONV_DIM), full(1, SSD_CONV_DIM),
                    full(1, LANE), full(1, LANE)]
        args = (bin_, bin_, bin_, cw, cb, dtb, alog)
    return pl.pallas_call(
        functools.partial(_ssd_kernel, rev=rev, nct=nct, t=t),
        grid=(batch, nct),
        in_specs=in_specs,
        out_specs=yspec,
        out_shape=jax.ShapeDtypeStruct((batch * seq, SSD_INNER), F32),
        scratch_shapes=[pltpu.VMEM((SSD_HEADS // 2, SSD_STATE, LANE), F32)],
        compiler_params=_cparams("parallel", "arbitrary"),
        name="ssd_bwd" if rev else "ssd_fwd",
    )(*args)


def _outproj_kernel(x_ref, mla_ref, ssd_ref, dif_ref, lam_ref, sub_ref, wo_ref, fn_ref, wrh_ref, wrl_ref,
                    xo_ref, hn_ref, aff_ref, *, lambda_init):
    lam = lam_ref[...]
    lam_full = (jnp.exp(jnp.sum(lam[0:1] * lam[1:2], keepdims=True))
                - jnp.exp(jnp.sum(lam[2:3] * lam[3:4], keepdims=True)) + lambda_init)
    dif = dif_ref[0]
    outs = []
    for hd in range(DIFF_HEADS):
        o = dif[2 * hd * DIFF_V:(2 * hd + 1) * DIFF_V] - lam_full * dif[(2 * hd + 1) * DIFF_V:(2 * hd + 2) * DIFF_V]
        o = o * lax.rsqrt(jnp.mean(o * o, axis=0, keepdims=True) + EPS) * sub_ref[...] * (1.0 - lambda_init)
        outs.append(o)
    dt_ = jnp.concatenate(outs, axis=0).astype(BF16)
    wo = wo_ref[...]
    x = x_ref[...]
    x = x + _dot_tn(mla_ref[0].astype(BF16), wo[:MLA_WIDTH])
    x = x + _dot(ssd_ref[...].astype(BF16), wo[MLA_WIDTH:MLA_WIDTH + SSD_INNER])
    x = x + _dot_tn(dt_, wo[MLA_WIDTH + SSD_INNER:])
    xo_ref[...] = x
    hn = x * lax.rsqrt(jnp.mean(x * x, axis=-1, keepdims=True) + EPS) * fn_ref[...]
    hn_ref[...] = hn
    hh, hl = _split_bf16(hn, 2)
    logits = _dot(hh, wrh_ref[...]) + _dot(hl, wrh_ref[...]) + _dot(hh, wrl_ref[...])
    lane = lax.broadcasted_iota(I32, logits.shape, 1)
    logits = jnp.where(lane < N_EXPERTS, logits, -jnp.inf)
    e = jnp.exp(logits - jnp.max(logits, axis=-1, keepdims=True))
    aff = e / jnp.sum(e, axis=-1, keepdims=True)
    aff_ref[...] = aff.T[:N_EXPERTS]


def _outproj(x2, mla_t, ssd, dif_t, lam, sub, wo, fn, wrh, wrl, seq, lambda_init):
    t = x2.shape[0]
    tm = TOKEN_TILE
    nt = seq // tm
    full = lambda r, c: pl.BlockSpec((r, c), lambda i: (0, 0))
    row = lambda c: pl.BlockSpec((tm, c), lambda i: (i, 0))
    return pl.pallas_call(
        functools.partial(_outproj_kernel, lambda_init=lambda_init),
        grid=(t // tm,),
        in_specs=[row(D_MODEL),
                  pl.BlockSpec((1, MLA_WIDTH, tm), lambda i: (i // nt, 0, i % nt)),
                  row(SSD_INNER),
                  pl.BlockSpec((1, 2 * DIFF_WIDTH, tm), lambda i: (i // nt, 0, i % nt)),
                  full(4, DIFF_QK), full(DIFF_V, 1), full(D_MODEL, D_MODEL), full(1, D_MODEL),
                  full(D_MODEL, LANE), full(D_MODEL, LANE)],
        out_specs=[row(D_MODEL), row(D_MODEL), pl.BlockSpec((N_EXPERTS, tm), lambda i: (0, i))],
        out_shape=[jax.ShapeDtypeStruct((t, D_MODEL), F32),
                   jax.ShapeDtypeStruct((t, D_MODEL), F32),
                   jax.ShapeDtypeStruct((N_EXPERTS, t), F32)],
        compiler_params=_cparams("parallel"),
        name="outproj",
    )(x2, mla_t, ssd, dif_t, lam, sub, wo, fn, wrh, wrl)


def _cumsum_rowmajor(mask_f32, tri_u, tri_ls):
    local = _dot(mask_f32.astype(BF16), tri_u)
    rowtot = jnp.broadcast_to(local[:, LANE - 1:LANE], local.shape).astype(BF16)
    return local + _dot(tri_ls, rowtot)


def _select_kernel(aff_ref, pos_ref, idx_ref, toff_ref, tcnt_ref, cum_scr, *, cap, sb, rows):
    a = aff_ref[0]
    key = lax.bitcast_convert_type(a, I32)

    def bit_body(it, prefix):
        cand = prefix | lax.shift_left(jnp.int32(1), 30 - it)
        cnt = jnp.sum((key >= cand).astype(I32))
        return jnp.where(cnt >= cap, cand, prefix)

    thr = lax.fori_loop(0, 31, bit_body, jnp.int32(0))
    gt = key > thr
    eq = key == thr
    need = cap - jnp.sum(gt.astype(I32))
    r_i = lax.broadcasted_iota(I32, (LANE, LANE), 0)
    c_i = lax.broadcasted_iota(I32, (LANE, LANE), 1)
    tri_u = jnp.where(r_i <= c_i, 1.0, 0.0).astype(BF16)
    r_r = lax.broadcasted_iota(I32, (rows, rows), 0)
    c_r = lax.broadcasted_iota(I32, (rows, rows), 1)
    tri_ls = jnp.where(r_r > c_r, 1.0, 0.0).astype(BF16)
    eqf = jnp.where(eq, 1.0, 0.0)
    tie_rank = _cumsum_rowmajor(eqf, tri_u, tri_ls) - eqf
    sel = gt | (eq & (tie_rank < need.astype(F32)))
    self_ = jnp.where(sel, 1.0, 0.0)
    cum = _cumsum_rowmajor(self_, tri_u, tri_ls).astype(I32)
    pos_ref[0] = jnp.where(sel, cum - 1, -1)
    cum_scr[...] = cum
    rowend = cum[:, LANE - 1:LANE]
    rowstart = rowend - jnp.sum(self_, axis=-1, keepdims=True).astype(I32)
    toff_ref[0] = rowstart
    tcnt_ref[0] = rowend - rowstart

    def blk(bi, carry):
        p0 = bi * sb
        r_lo = jnp.sum((rowend <= p0).astype(I32))
        r_hi = jnp.sum((rowstart < p0 + sb).astype(I32))
        s_col = p0 + lax.broadcasted_iota(I32, (sb, LANE), 0)

        def rowbody(r, acc):
            return acc + (cum_scr[pl.ds(r, 1), :] <= s_col).astype(I32)

        acc = lax.fori_loop(r_lo, r_hi, rowbody, jnp.zeros((sb, LANE), I32))
        idx_ref[0, pl.ds(pl.multiple_of(p0, sb), sb), :] = jnp.sum(acc, axis=1, keepdims=True) + r_lo * LANE
        return carry

    lax.fori_loop(0, cap // sb, blk, 0)


def _select(aff3, cap):
    e, rows, _ = aff3.shape
    sb = min(IDX_SLOTS, cap)
    return pl.pallas_call(
        functools.partial(_select_kernel, cap=cap, sb=sb, rows=rows),
        grid=(e,),
        in_specs=[pl.BlockSpec((1, rows, LANE), lambda i: (i, 0, 0))],
        out_specs=[pl.BlockSpec((1, rows, LANE), lambda i: (i, 0, 0)),
                   pl.BlockSpec((1, cap, 1), lambda i: (i, 0, 0)),
                   pl.BlockSpec((1, rows, 1), lambda i: (i, 0, 0)),
                   pl.BlockSpec((1, rows, 1), lambda i: (i, 0, 0))],
        out_shape=[jax.ShapeDtypeStruct((e, rows, LANE), I32),
                   jax.ShapeDtypeStruct((e, cap, 1), I32),
                   jax.ShapeDtypeStruct((e, rows, 1), I32),
                   jax.ShapeDtypeStruct((e, rows, 1), I32)],
        scratch_shapes=[pltpu.VMEM((rows, LANE), I32)],
        compiler_params=_cparams("parallel"),
        name="select",
    )(aff3)


def _row_copy(src_hbm, src_row, dst_vmem, dst_row, sem):
    return pltpu.make_async_copy(src_hbm.at[pl.ds(src_row, 1)], dst_vmem.at[pl.ds(dst_row, 1)], sem)


def _ffn_kernel(idx_ref, nxt_ref, hn_hbm, wg_ref, wu_ref, wd_ref, ye_ref, xbuf, sem, *, slots, steps):
    n = pl.program_id(0) * pl.num_programs(1) + pl.program_id(1)
    cur = n % 2

    def gather(iref, buf_slot):
        for r in range(slots):
            _row_copy(hn_hbm, iref[0, 0, r], xbuf.at[buf_slot], r, sem.at[buf_slot]).start()

    def wait_all(buf_slot):
        pltpu.make_async_copy(hn_hbm.at[pl.ds(0, slots)], xbuf.at[buf_slot], sem.at[buf_slot]).wait()

    @pl.when(n == 0)
    def _():
        gather(idx_ref, 0)

    gather(nxt_ref, 1 - cur)
    wait_all(cur)
    x = xbuf[cur].astype(BF16)
    gate = _dot(x, wg_ref[0])
    up = _dot(x, wu_ref[0])
    hid = (_silu(gate) * up).astype(BF16)
    ye_ref[...] = _dot(hid, wd_ref[0])

    @pl.when(n == steps - 1)
    def _():
        wait_all(1 - cur)


def _ffn(idx3, hn, wg, wu, wd, cap):
    slots = idx3.shape[-1]
    nb = cap // slots
    steps = N_EXPERTS * nb
    wspec = lambda: pl.BlockSpec((1, D_MODEL, EXPERT_FF), lambda e, b: (e, 0, 0))
    ispec = lambda f: pl.BlockSpec((1, 1, slots), f, memory_space=pltpu.SMEM)
    return pl.pallas_call(
        functools.partial(_ffn_kernel, slots=slots, steps=steps),
        grid=(N_EXPERTS, nb),
        in_specs=[ispec(lambda e, b: (e * nb + b, 0, 0)),
                  ispec(lambda e, b: (jnp.minimum(e * nb + b + 1, steps - 1), 0, 0)),
                  pl.BlockSpec(memory_space=pl.ANY),
                  wspec(), wspec(),
                  pl.BlockSpec((1, EXPERT_FF, D_MODEL), lambda e, b: (e, 0, 0))],
        out_specs=pl.BlockSpec((slots, D_MODEL), lambda e, b: (e * nb + b, 0)),
        out_shape=jax.ShapeDtypeStruct((N_EXPERTS * cap, D_MODEL), F32),
        scratch_shapes=[pltpu.VMEM((2, slots, D_MODEL), F32), pltpu.SemaphoreType.DMA((2,))],
        compiler_params=_cparams("arbitrary", "arbitrary"),
        name="expert_ffn",
    )(idx3, idx3, hn, wg, wu, wd)


COMBINE_MAX_BLOCKS = N_EXPERTS * (COMBINE_TILE // SUBLANE + 1)
COMBINE_MAX_ROWS = -(-COMBINE_MAX_BLOCKS * SUBLANE // 256) * 256


def _combine_kernel(toff_ref, tcnt_ref, x_ref, pos_ref, aff_ref, ye_hbm, out_ref, buf, sem, *, cap, rows, chunk):
    r = pl.program_id(0)
    cur = r % 2

    def layout(tile):
        firsts, nblks, dsts = [], [], []
        dst = jnp.int32(0)
        for e in range(N_EXPERTS):
            off = toff_ref[e * rows + tile]
            cnt = tcnt_ref[e * rows + tile]
            first = (off >> 3) << 3
            firsts.append(first)
            nblks.append(jnp.where(cnt > 0, (off + cnt - first + SUBLANE - 1) >> 3, 0))
            dsts.append(dst)
            dst = dst + nblks[-1]
        return firsts, nblks, dsts, dst

    def gather(tile, buf_slot):
        firsts, nblks, dsts, _ = layout(tile)
        for e in range(N_EXPERTS):
            def issue(j, carry, e=e):
                src = pl.multiple_of(e * cap + firsts[e] + j * SUBLANE, SUBLANE)
                dst = pl.multiple_of((dsts[e] + j) * SUBLANE, SUBLANE)
                pltpu.make_async_copy(ye_hbm.at[pl.ds(src, SUBLANE)], buf.at[buf_slot, pl.ds(dst, SUBLANE)],
                                      sem.at[buf_slot]).start()
                return carry

            lax.fori_loop(0, nblks[e], issue, 0)

    @pl.when(r == 0)
    def _():
        gather(0, 0)

    @pl.when(r + 1 < rows)
    def _():
        gather(r + 1, 1 - cur)

    firsts, _, dsts, nblk_total = layout(r)
    total = nblk_total * SUBLANE
    for bit in range(COMBINE_MAX_BLOCKS.bit_length()):
        @pl.when(((nblk_total >> bit) & 1) == 1)
        def _(bit=bit):
            n = SUBLANE << bit
            pltpu.make_async_copy(ye_hbm.at[pl.ds(0, n)], buf.at[cur, pl.ds(0, n)], sem.at[cur]).wait()

    out_ref[...] = x_ref[...]
    pos = pos_ref[...]
    aff = aff_ref[...]
    tgt = [jnp.where(pos[e:e + 1] >= 0, pos[e:e + 1] - firsts[e] + dsts[e] * SUBLANE, -1)
           for e in range(N_EXPERTS)]

    def chunk_body(k, carry):
        base = pl.multiple_of(k * chunk, chunk)
        cidx = base + lax.broadcasted_iota(I32, (chunk, COMBINE_TILE), 0)
        wt = jnp.zeros((chunk, COMBINE_TILE), F32)
        for e in range(N_EXPERTS):
            wt = wt + jnp.where(cidx == tgt[e], aff[e:e + 1], 0.0)
        rowid = base + lax.broadcasted_iota(I32, (chunk, 1), 0)
        rowsv = jnp.where(rowid < total, buf[cur, pl.ds(base, chunk), :], 0.0).astype(BF16)
        wh, wl = _split_bf16(wt, 2)
        out_ref[...] += _dot_tn(wh, rowsv) + _dot_tn(wl, rowsv)
        return carry

    lax.fori_loop(0, (total + chunk - 1) // chunk, chunk_body, 0)


def _combine(toff, tcnt, x2, pos, aff_t, ye, cap):
    t = x2.shape[0]
    tile = COMBINE_TILE
    rows = t // tile
    chunk = 256
    grid_spec = pltpu.PrefetchScalarGridSpec(
        num_scalar_prefetch=2,
        grid=(rows,),
        in_specs=[pl.BlockSpec((tile, D_MODEL), lambda i, *_: (i, 0)),
                  pl.BlockSpec((N_EXPERTS, tile), lambda i, *_: (0, i)),
                  pl.BlockSpec((N_EXPERTS, tile), lambda i, *_: (0, i)),
                  pl.BlockSpec(memory_space=pl.ANY)],
        out_specs=pl.BlockSpec((tile, D_MODEL), lambda i, *_: (i, 0)),
        scratch_shapes=[pltpu.VMEM((2, COMBINE_MAX_ROWS, D_MODEL), F32), pltpu.SemaphoreType.DMA((2,))],
    )
    return pl.pallas_call(
        functools.partial(_combine_kernel, cap=cap, rows=rows, chunk=chunk),
        grid_spec=grid_spec,
        out_shape=jax.ShapeDtypeStruct((t, D_MODEL), F32),
        compiler_params=_cparams("arbitrary"),
        name="combine",
    )(toff, tcnt, x2, pos, aff_t, ye)


def _rope_tables(seq):
    inv = ROPE_THETA ** (-jnp.arange(0, MLA_ROPE, 2, dtype=F32) / MLA_ROPE)
    ang = jnp.arange(seq, dtype=F32)[:, None] * inv[None, :]
    cos, sin = jnp.cos(ang), jnp.sin(ang)
    one = jnp.ones((seq, 1), F32)
    zero = jnp.zeros((seq, 1), F32)
    rep = lambda v, n: jnp.broadcast_to(v, (seq, n))
    half = MLA_ROPE // 2
    mla = (jnp.concatenate([rep(one, MLA_NOPE), cos, cos, rep(one, LANE - MLA_QK)], axis=1),
           jnp.concatenate([rep(zero, MLA_NOPE + half), sin, rep(zero, LANE - MLA_QK)], axis=1),
           jnp.concatenate([rep(zero, MLA_NOPE), -sin, rep(zero, LANE - MLA_NOPE - half)], axis=1))
    nblk = LANE // DIFF_QK
    dif = (jnp.tile(jnp.concatenate([cos, cos], axis=1), (1, nblk)),
           jnp.tile(jnp.concatenate([rep(zero, half), sin], axis=1), (1, nblk)),
           jnp.tile(jnp.concatenate([-sin, rep(zero, half)], axis=1), (1, nblk)))
    return mla, dif


def _pad_cols(w, width):
    return jnp.pad(w, ((0, 0), (0, width - w.shape[1])))


def _layer_params(p, l):
    w_in = p['w_in'][l]
    z64 = jnp.zeros((D_MODEL, 64), F32)
    z32 = jnp.zeros((D_MODEL, 32), F32)
    w_proj = jnp.concatenate([
        w_in[:, :OFF_KPE], z64, w_in[:, OFF_KPE:OFF_Z], z32,
        w_in[:, OFF_Z:OFF_DT], _pad_cols(w_in[:, OFF_DT:OFF_DQ], LANE),
        w_in[:, OFF_DQ:]], axis=1).astype(BF16)
    wq = p['mla_w_uq'][l].reshape(MLA_Q_LORA, MLA_HEADS, MLA_QK)
    wq = jnp.pad(wq, ((0, 0), (0, 0), (0, LANE - MLA_QK))).reshape(MLA_Q_LORA, MLA_HEADS * LANE)
    wkv = p['mla_w_ukv'][l].reshape(MLA_KV_LORA, MLA_HEADS, MLA_NOPE + MLA_V)
    wk = jnp.pad(wkv[:, :, :MLA_NOPE], ((0, 0), (0, 0), (0, LANE - MLA_NOPE))).reshape(MLA_KV_LORA, MLA_HEADS * LANE)
    wv = wkv[:, :, MLA_NOPE:].reshape(MLA_KV_LORA, MLA_WIDTH)
    wr = _pad_cols(p['w_router'][l], LANE)
    wrh = wr.astype(BF16)
    wrl = (wr - wrh.astype(F32)).astype(BF16)
    pad_row = lambda v, n: jnp.pad(v, (0, n - v.shape[0]))[None, :]
    return dict(
        attn_norm=p['attn_norm'][l][None, :], w_proj=w_proj,
        q_norm=p['mla_q_norm'][l][None, :], kv_norm=p['mla_kv_norm'][l][None, :],
        wq=wq.astype(BF16), wk=wk.astype(BF16), wv=wv.astype(BF16),
        q_gain=pad_row(p['mla_q_gain'][l], LANE), k_gain=pad_row(p['mla_k_gain'][l], LANE),
        conv_w=p['ssd_conv_w'][l], conv_b=p['ssd_conv_b'][l][None, :],
        dt_bias=pad_row(p['ssd_dt_bias'][l].reshape(-1), LANE),
        a_log=pad_row(p['ssd_a_log'][l].reshape(-1), LANE),
        d_skip=jnp.repeat(p['ssd_d'][l], SSD_HEAD_DIM)[None, :], ssd_norm=p['ssd_norm'][l][None, :],
        dq_gain=jnp.tile(p['diff_q_gain'][l], 2 * DIFF_HEADS)[None, :],
        dk_gain=jnp.tile(p['diff_k_gain'][l], 2 * DIFF_HEADS)[None, :],
        lam=p['diff_lambda'][l], subln=p['diff_subln'][l][:, None],
        w_out=p['w_out'][l].astype(BF16), ffn_norm=p['ffn_norm'][l][None, :],
        wrh=wrh, wrl=wrl,
        w_gate=p['w_gate'][l].astype(BF16), w_up=p['w_up'][l].astype(BF16), w_down=p['w_down'][l].astype(BF16),
    )


def _block_diag_ones(n, blk):
    i = jnp.arange(n)
    return (i[:, None] // blk == i[None, :] // blk).astype(BF16)


def _trunk(x, p):
    batch, seq, _ = x.shape
    t = batch * seq
    cap = EC_CAPACITY * t // N_EXPERTS
    tk = min(ATTN_TK, seq)
    mla_tabs, dif_tabs = _rope_tables(seq)
    bd = _block_diag_ones(2 * DIFF_HEADS * DIFF_QK, DIFF_QK)
    x2 = x.reshape(t, D_MODEL)
    rows = t // LANE
    for l in range(DEPTH):
        lp = _layer_params(p, l)
        a_in, b_in, c_in = _inproj(x2, lp['attn_norm'], lp['w_proj'])
        q, k, vt = _mla_prep(a_in, lp['q_norm'], lp['kv_norm'], lp['wq'], lp['wk'], lp['wv'],
                             lp['q_gain'], lp['k_gain'], mla_tabs, seq)
        mla_bound = (MLA_QK ** 0.5 * LOG2E) * jnp.max(jnp.abs(p['mla_q_gain'][l])) * jnp.max(jnp.abs(p['mla_k_gain'][l]))
        mla_t = _attention(q.reshape(batch, seq, -1), k.reshape(batch, seq, -1),
                           vt.reshape(batch, seq // tk, MLA_HEADS * V_ROWS, tk),
                           MLA_HEADS, lambda h: h, lambda h: h, MLA_V, tk, mla_bound)
        dq, dk, dvt = _diff_prep(c_in, lp['dq_gain'], lp['dk_gain'], bd, dif_tabs, seq)
        dif_bound = (DIFF_QK ** 0.5 * LOG2E) * jnp.max(jnp.abs(p['diff_q_gain'][l])) * jnp.max(jnp.abs(p['diff_k_gain'][l]))
        dif_t = _attention(dq.reshape(batch, seq, -1), dk.reshape(batch, seq, -1),
                           dvt.reshape(batch, seq // tk, DIFF_HEADS * V_ROWS, tk),
                           2 * DIFF_HEADS, lambda h: h // (LANE // DIFF_QK), lambda h: h // 2, DIFF_V, tk,
                           dif_bound)
        yf = _ssd_pass(b_in, None, lp['conv_w'], lp['conv_b'], lp['dt_bias'], lp['a_log'],
                       None, None, batch, seq, rev=False)
        ssd = _ssd_pass(b_in, yf, lp['conv_w'], lp['conv_b'], lp['dt_bias'], lp['a_log'],
                        lp['d_skip'], lp['ssd_norm'], batch, seq, rev=True)
        lambda_init = 0.8 - 0.6 * math.exp(-0.3 * l)
        x2, hn, aff_t = _outproj(x2, mla_t, ssd, dif_t, lp['lam'], lp['subln'], lp['w_out'],
                                 lp['ffn_norm'], lp['wrh'], lp['wrl'], seq, lambda_init)
        pos3, idx, toff, tcnt = _select(aff_t.reshape(N_EXPERTS, rows, LANE), cap)
        slots = min(FFN_SLOTS, cap)
        ye = _ffn(idx.reshape(N_EXPERTS * cap // slots, 1, slots), hn,
                  lp['w_gate'], lp['w_up'], lp['w_down'], cap)
        x2 = _combine(toff.reshape(-1), tcnt.reshape(-1), x2, pos3.reshape(N_EXPERTS, t), aff_t, ye, cap)
    return x2.reshape(batch, seq, D_MODEL)


def kernel(x_prompt, x_sample, attn_norm, w_in, mla_q_norm, mla_kv_norm, mla_w_uq, mla_w_ukv, mla_q_gain,
           mla_k_gain, ssd_conv_w, ssd_conv_b, ssd_dt_bias, ssd_a_log, ssd_d, ssd_norm, diff_q_gain,
           diff_k_gain, diff_lambda, diff_subln, w_out, ffn_norm, w_router, w_gate, w_up, w_down):
    p = dict(attn_norm=attn_norm, w_in=w_in, mla_q_norm=mla_q_norm, mla_kv_norm=mla_kv_norm,
             mla_w_uq=mla_w_uq, mla_w_ukv=mla_w_ukv, mla_q_gain=mla_q_gain, mla_k_gain=mla_k_gain,
             ssd_conv_w=ssd_conv_w, ssd_conv_b=ssd_conv_b, ssd_dt_bias=ssd_dt_bias, ssd_a_log=ssd_a_log,
             ssd_d=ssd_d, ssd_norm=ssd_norm, diff_q_gain=diff_q_gain, diff_k_gain=diff_k_gain,
             diff_lambda=diff_lambda, diff_subln=diff_subln, w_out=w_out, ffn_norm=ffn_norm,
             w_router=w_router, w_gate=w_gate, w_up=w_up, w_down=w_down)
    return _trunk(x_prompt, p), _trunk(x_sample, p)
```

```python
import functools
import math

import jax
import jax.numpy as jnp
from jax import lax
from jax.experimental import pallas as pl
from jax.experimental.pallas import tpu as pltpu

F32 = jnp.float32
BF16 = jnp.bfloat16
I32 = jnp.int32

D_MODEL = 1024
DEPTH = 2
EPS = 1e-6
ROPE_THETA = 10000.0

MLA_HEADS = 6
MLA_Q_LORA = 256
MLA_KV_LORA = 128
MLA_NOPE = 64
MLA_ROPE = 32
MLA_QK = MLA_NOPE + MLA_ROPE
MLA_V = 64
MLA_WIDTH = MLA_HEADS * MLA_V

SSD_HEADS = 6
SSD_HEAD_DIM = 64
SSD_INNER = SSD_HEADS * SSD_HEAD_DIM
SSD_GROUPS = 2
SSD_STATE = 128
SSD_CONV = 5
SSD_CONV_DIM = SSD_INNER + 2 * SSD_GROUPS * SSD_STATE

DIFF_HEADS = 4
DIFF_QK = 32
DIFF_V = 2 * DIFF_QK
DIFF_WIDTH = DIFF_HEADS * DIFF_V

N_EXPERTS = 16
EC_CAPACITY = 2
EXPERT_FF = 1024

OFF_KV = MLA_Q_LORA
OFF_KPE = OFF_KV + MLA_KV_LORA
OFF_Z = OFF_KPE + MLA_ROPE
OFF_XBC = OFF_Z + SSD_INNER
OFF_DT = OFF_XBC + SSD_CONV_DIM
OFF_DQ = OFF_DT + 2 * SSD_HEADS
OFF_DK = OFF_DQ + DIFF_HEADS * 2 * DIFF_QK
OFF_DV = OFF_DK + DIFF_HEADS * 2 * DIFF_QK
IN_COLS = OFF_DV + DIFF_WIDTH

LANE = 128
SUBLANE = 8

MLA_IN = 512
SSD_IN = SSD_INNER + SSD_CONV_DIM + LANE
DIFF_IN = 768
PROJ_COLS = MLA_IN + SSD_IN + DIFF_IN

TOKEN_TILE = 512
ATTN_TQ = 512
ATTN_TK = 512
ATTN_GROUP = 16
SSD_T = 256
COMBINE_TILE = 128
V_ROWS = 80
LOG2E = math.log2(math.e)
NOSHIFT_MAX_LOG2 = 60.0
FFN_SLOTS = 256
IDX_SLOTS = 256
VMEM_LIMIT = 56 * 1024 * 1024


def _cparams(*sem):
    return pltpu.CompilerParams(dimension_semantics=sem, vmem_limit_bytes=VMEM_LIMIT)


def _split_bf16(a, terms):
    parts = []
    rem = a
    for _ in range(terms):
        p = rem.astype(BF16)
        parts.append(p)
        rem = rem - p.astype(F32)
    return parts


def _dot(a, b):
    return jnp.dot(a, b, preferred_element_type=F32)


def _dot_nt(a, b):
    return lax.dot_general(a, b, (((1,), (1,)), ((), ())), preferred_element_type=F32)


def _dot_tn(a, b):
    return lax.dot_general(a, b, (((0,), (0,)), ((), ())), preferred_element_type=F32)


def _dot_split_lhs(a_f32, b_bf16, terms):
    out = None
    for p in _split_bf16(a_f32, terms):
        d = _dot(p, b_bf16)
        out = d if out is None else out + d
    return out


def _dot_split_rhs(a_bf16, b_f32, terms):
    out = None
    for p in _split_bf16(b_f32, terms):
        d = _dot(a_bf16, p)
        out = d if out is None else out + d
    return out


def _silu(x):
    return x * jax.nn.sigmoid(x)


def _inproj_kernel(x_ref, g_ref, w_ref, a_ref, b_ref, c_ref):
    x = x_ref[...]
    ms = jnp.mean(x * x, axis=-1, keepdims=True)
    h = (x * lax.rsqrt(ms + EPS) * g_ref[...]).astype(BF16)
    y = _dot(h, w_ref[...])
    a_ref[...] = y[:, :MLA_IN]
    b_ref[...] = y[:, MLA_IN:MLA_IN + SSD_IN]
    c_ref[...] = y[:, MLA_IN + SSD_IN:]


def _inproj(x2, gain, w):
    t = x2.shape[0]
    tm = TOKEN_TILE
    return pl.pallas_call(
        _inproj_kernel,
        grid=(t // tm,),
        in_specs=[pl.BlockSpec((tm, D_MODEL), lambda i: (i, 0)),
                  pl.BlockSpec((1, D_MODEL), lambda i: (0, 0)),
                  pl.BlockSpec((D_MODEL, PROJ_COLS), lambda i: (0, 0))],
        out_specs=[pl.BlockSpec((tm, MLA_IN), lambda i: (i, 0)),
                   pl.BlockSpec((tm, SSD_IN), lambda i: (i, 0)),
                   pl.BlockSpec((tm, DIFF_IN), lambda i: (i, 0))],
        out_shape=[jax.ShapeDtypeStruct((t, MLA_IN), F32),
                   jax.ShapeDtypeStruct((t, SSD_IN), F32),
                   jax.ShapeDtypeStruct((t, DIFF_IN), F32)],
        compiler_params=_cparams("parallel"),
        name="inproj",
    )(x2, gain, w)


def _rope(x, c, sa, sb, shift):
    return x * c + pltpu.roll(x, shift, 1) * sa + pltpu.roll(x, LANE - shift, 1) * sb


def _mla_prep_kernel(a_ref, qn_ref, kvn_ref, wq_ref, wk_ref, wv_ref, qg_ref, kg_ref,
                     c_ref, sa_ref, sb_ref, q_ref, k_ref, vt_ref):
    a = a_ref[...]
    cq = a[:, :MLA_Q_LORA]
    ckv = a[:, MLA_Q_LORA:MLA_Q_LORA + MLA_KV_LORA]
    kpe = a[:, MLA_Q_LORA + MLA_KV_LORA:]
    cqn = (cq * lax.rsqrt(jnp.mean(cq * cq, axis=-1, keepdims=True) + EPS) * qn_ref[...]).astype(BF16)
    ckvn = (ckv * lax.rsqrt(jnp.mean(ckv * ckv, axis=-1, keepdims=True) + EPS) * kvn_ref[...]).astype(BF16)
    q = _dot(cqn, wq_ref[...])
    kn = _dot(ckvn, wk_ref[...])
    v = _dot(ckvn, wv_ref[...])
    c = c_ref[...]
    sa = sa_ref[...]
    sb = sb_ref[...]
    scale = MLA_QK ** -0.5 * LOG2E
    for h in range(MLA_HEADS):
        sl = slice(h * LANE, (h + 1) * LANE)
        qh = q[:, sl]
        qh = qh * lax.rsqrt(jnp.sum(qh * qh, axis=-1, keepdims=True) / MLA_QK + EPS) * qg_ref[...]
        q_ref[:, sl] = (_rope(qh, c, sa, sb, MLA_ROPE // 2) * scale).astype(BF16)
        kh = kn[:, sl] + kpe
        kh = kh * lax.rsqrt(jnp.sum(kh * kh, axis=-1, keepdims=True) / MLA_QK + EPS) * kg_ref[...]
        k_ref[:, sl] = _rope(kh, c, sa, sb, MLA_ROPE // 2).astype(BF16)
    vt = v.T.astype(BF16)
    ones = jnp.ones((V_ROWS - MLA_V, vt.shape[1]), BF16)
    for h in range(MLA_HEADS):
        vt_ref[0, h * V_ROWS:(h + 1) * V_ROWS, :] = jnp.concatenate([vt[h * MLA_V:(h + 1) * MLA_V], ones], axis=0)


def _mla_prep(a, qn, kvn, wq, wk, wv, qg, kg, tabs, seq):
    t = a.shape[0]
    tm = TOKEN_TILE
    nt = seq // tm
    full = lambda r, c: pl.BlockSpec((r, c), lambda i: (0, 0))
    tab = pl.BlockSpec((tm, LANE), lambda i: (i % nt, 0))
    return pl.pallas_call(
        _mla_prep_kernel,
        grid=(t // tm,),
        in_specs=[pl.BlockSpec((tm, MLA_IN), lambda i: (i, 0)),
                  full(1, MLA_Q_LORA), full(1, MLA_KV_LORA),
                  full(MLA_Q_LORA, MLA_HEADS * LANE), full(MLA_KV_LORA, MLA_HEADS * LANE),
                  full(MLA_KV_LORA, MLA_WIDTH), full(1, LANE), full(1, LANE), tab, tab, tab],
        out_specs=[pl.BlockSpec((tm, MLA_HEADS * LANE), lambda i: (i, 0)),
                   pl.BlockSpec((tm, MLA_HEADS * LANE), lambda i: (i, 0)),
                   pl.BlockSpec((1, MLA_HEADS * V_ROWS, tm), lambda i: (i, 0, 0))],
        out_shape=[jax.ShapeDtypeStruct((t, MLA_HEADS * LANE), BF16),
                   jax.ShapeDtypeStruct((t, MLA_HEADS * LANE), BF16),
                   jax.ShapeDtypeStruct((t // tm, MLA_HEADS * V_ROWS, tm), BF16)],
        compiler_params=_cparams("parallel"),
        name="mla_prep",
    )(a, qn, kvn, wq, wk, wv, qg, kg, *tabs)


def _diff_prep_kernel(cin_ref, qg_ref, kg_ref, bd_ref, c_ref, sa_ref, sb_ref, q_ref, k_ref, vt_ref):
    cin = cin_ref[...]
    bd = bd_ref[...]
    c = jnp.concatenate([c_ref[...], c_ref[...]], axis=1)
    sa = jnp.concatenate([sa_ref[...], sa_ref[...]], axis=1)
    sb = jnp.concatenate([sb_ref[...], sb_ref[...]], axis=1)
    lane = lax.broadcasted_iota(I32, (1, LANE), 1)

    def norm_rope(x, g):
        ms = _dot_split_lhs(x * x, bd, 3) / DIFF_QK
        x = x * lax.rsqrt(ms + EPS) * g
        halves = []
        for j in range(2):
            sl = slice(j * LANE, (j + 1) * LANE)
            halves.append(_rope(x[:, sl], c[:, sl], sa[:, sl], sb[:, sl], DIFF_QK // 2))
        return halves

    qh = norm_rope(cin[:, :256], qg_ref[...])
    kh = norm_rope(cin[:, 256:512], kg_ref[...])
    scale = DIFF_QK ** -0.5 * LOG2E
    for j in range(2 * DIFF_HEADS):
        grp, sub = divmod(j, LANE // DIFF_QK)
        keep = (lane >= sub * DIFF_QK) & (lane < (sub + 1) * DIFF_QK)
        q_ref[:, j * LANE:(j + 1) * LANE] = jnp.where(keep, qh[grp] * scale, 0.0).astype(BF16)
    k_ref[...] = jnp.concatenate(kh, axis=1).astype(BF16)
    vt = cin[:, 512:].T.astype(BF16)
    ones = jnp.ones((V_ROWS - DIFF_V, vt.shape[1]), BF16)
    for h in range(DIFF_HEADS):
        vt_ref[0, h * V_ROWS:(h + 1) * V_ROWS, :] = jnp.concatenate([vt[h * DIFF_V:(h + 1) * DIFF_V], ones], axis=0)


def _diff_prep(cin, qg, kg, bd, tabs, seq):
    t = cin.shape[0]
    tm = TOKEN_TILE
    nt = seq // tm
    full = lambda r, c: pl.BlockSpec((r, c), lambda i: (0, 0))
    tab = pl.BlockSpec((tm, LANE), lambda i: (i % nt, 0))
    nq = 2 * DIFF_HEADS * LANE
    return pl.pallas_call(
        _diff_prep_kernel,
        grid=(t // tm,),
        in_specs=[pl.BlockSpec((tm, DIFF_IN), lambda i: (i, 0)),
                  full(1, 256), full(1, 256), full(256, 256), tab, tab, tab],
        out_specs=[pl.BlockSpec((tm, nq), lambda i: (i, 0)),
                   pl.BlockSpec((tm, 256), lambda i: (i, 0)),
                   pl.BlockSpec((1, DIFF_HEADS * V_ROWS, tm), lambda i: (i, 0, 0))],
        out_shape=[jax.ShapeDtypeStruct((t, nq), BF16),
                   jax.ShapeDtypeStruct((t, 256), BF16),
                   jax.ShapeDtypeStruct((t // tm, DIFF_HEADS * V_ROWS, tm), BF16)],
        compiler_params=_cparams("parallel"),
        name="diff_prep",
    )(cin, qg, kg, bd, *tabs)


def _attn_noshift_kernel(q_ref, k_ref, vt_ref, o_ref, acc_scr, l_scr, pa_scr, pb_scr, *, nkc, tk, dv):
    q = q_ref[0]

    def probs(kc, p_ref):
        k = k_ref[0, pl.ds(pl.multiple_of(kc * tk, tk), tk), :]
        p = jnp.exp2(_dot_nt(k, q))
        p_ref[...] = p.astype(BF16)
        return jnp.sum(p, axis=0, keepdims=True)

    def pv(kc, p_ref):
        return _dot(vt_ref[0, kc, :dv, :], p_ref[...])

    bufs = (pa_scr, pb_scr)
    unroll = ATTN_GROUP

    def run(first, count, last):
        tot = None
        lsum = None
        for j in range(count):
            if not (last and j == count - 1):
                ls = probs(first + j + 1, bufs[(j + 1) % 2])
                lsum = ls if lsum is None else lsum + ls
            d = pv(first + j, bufs[j % 2])
            tot = d if tot is None else tot + d
        return tot, lsum

    l_scr[...] = probs(0, pa_scr)
    acc_scr[...] = jnp.zeros_like(acc_scr)
    trips = (nkc - 1) // unroll

    def body(i, carry):
        tot, lsum = run(i * unroll, unroll, False)
        acc_scr[...] += tot
        l_scr[...] += lsum
        return carry

    lax.fori_loop(0, trips, body, 0)
    tot, lsum = run(trips * unroll, nkc - trips * unroll, True)
    den = l_scr[...] if lsum is None else l_scr[...] + lsum
    o_ref[0] = ((acc_scr[...] + tot) / den).astype(o_ref.dtype)


def _attn_online_kernel(q_ref, k_ref, vt_ref, o_ref, m_scr, acc_scr, *, nkc, tk, dv):
    q = q_ref[0]
    m_scr[...] = jnp.full_like(m_scr, -jnp.inf)
    acc_scr[...] = jnp.zeros_like(acc_scr)

    def body(kc, carry):
        k = k_ref[0, pl.ds(pl.multiple_of(kc * tk, tk), tk), :]
        s = _dot_nt(k, q)
        m_prev = m_scr[...]
        m_new = jnp.maximum(m_prev, jnp.max(s, axis=0, keepdims=True))
        p = jnp.exp2(s - m_new).astype(BF16)
        acc_scr[...] = jnp.exp2(m_prev - m_new) * acc_scr[...] + _dot(vt_ref[0, kc], p)
        m_scr[...] = m_new
        return carry

    lax.fori_loop(0, nkc, body, 0)
    acc = acc_scr[...]
    o_ref[0] = (acc[:dv] / acc[dv:dv + 1]).astype(o_ref.dtype)


def _attention(q, k, vt, heads, kmap, vmap, dv, tk, score_bound, out_dtype):
    b, s, _ = q.shape
    tq = min(ATTN_TQ, s)
    nkc = s // tk

    def call(body, scratch, name):
        return pl.pallas_call(
            functools.partial(body, nkc=nkc, tk=tk, dv=dv),
            grid=(b, heads, s // tq),
            in_specs=[pl.BlockSpec((1, tq, LANE), lambda bi, h, qi: (bi, qi, h)),
                      pl.BlockSpec((1, s, LANE), lambda bi, h, qi: (bi, 0, kmap(h))),
                      pl.BlockSpec((1, nkc, V_ROWS, tk), lambda bi, h, qi: (bi, 0, vmap(h), 0))],
            out_specs=pl.BlockSpec((1, dv, tq), lambda bi, h, qi: (bi, h, qi)),
            out_shape=jax.ShapeDtypeStruct((b, heads * dv, s), out_dtype),
            scratch_shapes=scratch,
            compiler_params=_cparams("parallel", "parallel", "arbitrary"),
            name=name,
        )

    acc = pltpu.VMEM((V_ROWS, tq), F32)
    pbuf = pltpu.VMEM((tk, tq), BF16)
    fast = call(_attn_noshift_kernel, [pltpu.VMEM((dv, tq), F32), pltpu.VMEM((1, tq), F32), pbuf, pbuf],
                "attention")
    safe = call(_attn_online_kernel, [pltpu.VMEM((1, tq), F32), acc], "attention_online")
    return lax.cond(score_bound < NOSHIFT_MAX_LOG2, fast, safe, q, k, vt)


def _ssd_kernel(*refs, rev, nct, t):
    if rev:
        (main_ref, prev_ref, next_ref, yf_ref, cw_ref, cb_ref, dtb_ref, alog_ref, dsk_ref, nw_ref,
         out_ref, h_scr) = refs
    else:
        (main_ref, prev_ref, next_ref, cw_ref, cb_ref, dtb_ref, alog_ref, out_ref, h_scr) = refs
    i = pl.program_id(1)
    c = (nct - 1 - i) if rev else i

    @pl.when(i == 0)
    def _():
        h_scr[...] = jnp.zeros_like(h_scr)

    main = main_ref[...]
    xlo, xhi = SSD_INNER, SSD_INNER + SSD_CONV_DIM
    prev = jnp.where(c > 0, prev_ref[:, xlo:xhi], 0.0)
    nxt = jnp.where(c < nct - 1, next_ref[:, xlo:xhi], 0.0)
    xp = jnp.concatenate([prev, main[:, xlo:xhi], nxt], axis=0)
    acc = jnp.broadcast_to(cb_ref[...], (t, SSD_CONV_DIM))
    for j in range(SSD_CONV):
        sh = (SSD_CONV // 2 - j) % (t + 2 * SUBLANE)
        r = xp if sh == 0 else pltpu.roll(xp, sh, 0)
        acc = acc + r[SUBLANE:SUBLANE + t] * cw_ref[j:j + 1, :]
    xc = _silu(acc)
    xs = xc[:, :SSD_INNER]
    gn = SSD_GROUPS * SSD_STATE
    bmat = xc[:, SSD_INNER:SSD_INNER + gn]
    cmat = xc[:, SSD_INNER + gn:]

    dtr = main[:, xhi:] + dtb_ref[...]
    dt = jnp.maximum(dtr, 0.0) + jnp.log1p(jnp.exp(-jnp.abs(dtr)))
    a = dt * (-jnp.exp(alog_ref[...]))

    row = lax.broadcasted_iota(I32, (t, t), 0)
    col = lax.broadcasted_iota(I32, (t, t), 1)
    lower = row >= col
    tri_l = jnp.where(lower, 1.0, 0.0).astype(BF16)
    tri_u = jnp.where(row <= col, 1.0, 0.0).astype(BF16)
    a_t = a.T
    cs = _dot_split_rhs(tri_l, a, 3)
    cs_t = _dot_split_lhs(a_t, tri_u, 3)
    tot = cs[t - 1:t, :]
    if rev:
        ecol = cs - a
        erow = cs_t - a_t
    lane = lax.broadcasted_iota(I32, (1, LANE), 1)
    hb = SSD_HEADS if rev else 0
    hpg = SSD_HEADS // SSD_GROUPS

    g = []
    bt = []
    for grp in range(SSD_GROUPS):
        bg = bmat[:, grp * SSD_STATE:(grp + 1) * SSD_STATE]
        cg = cmat[:, grp * SSD_STATE:(grp + 1) * SSD_STATE].astype(BF16)
        g.append((_dot_nt(cg, bg.astype(BF16)), cg))
        bt.append(bg.T.astype(BF16))

    ys = []
    for pair in range(SSD_HEADS // 2):
        xpair = xs[:, pair * LANE:(pair + 1) * LANE]
        hstate = h_scr[pair]
        hbf = hstate.astype(BF16)
        y = jnp.zeros((t, LANE), F32)
        hnew = jnp.zeros((SSD_STATE, LANE), F32)
        dec = jnp.zeros((1, LANE), F32)
        for sub in range(2):
            h = pair * 2 + sub
            grp = h // hpg
            gmat, cg = g[grp]
            keep = (lane >= sub * SSD_HEAD_DIM) & (lane < (sub + 1) * SSD_HEAD_DIM)
            hl = hb + h
            dtc = dt[:, hl:hl + 1]
            xm = jnp.where(keep, xpair * dtc, 0.0)
            if rev:
                dmat = jnp.exp(jnp.where(row <= col, erow[hl:hl + 1, :] - ecol[:, hl:hl + 1], -jnp.inf))
                off_scale = jnp.exp(tot[:, hl:hl + 1] - ecol[:, hl:hl + 1])
                st_w = jnp.exp(ecol[:, hl:hl + 1])
            else:
                dmat = jnp.exp(jnp.where(lower, cs[:, hl:hl + 1] - cs_t[hl:hl + 1, :], -jnp.inf))
                off_scale = jnp.exp(cs[:, hl:hl + 1])
                st_w = jnp.exp(tot[:, hl:hl + 1] - cs[:, hl:hl + 1])
            y = y + _dot((gmat * dmat).astype(BF16), xm.astype(BF16))
            y = y + jnp.where(keep, _dot(cg, hbf) * off_scale, 0.0)
            hnew = hnew + _dot(bt[grp], (xm * st_w).astype(BF16))
            dec = dec + jnp.where(keep, jnp.exp(tot[:, hl:hl + 1]), 0.0)
        h_scr[pair] = hstate * dec + hnew
        ys.append(y)

    yall = jnp.concatenate(ys, axis=1)
    if rev:
        yall = yall + yf_ref[...] + xs * dsk_ref[...]
        gt = yall * _silu(main[:, :SSD_INNER])
        yall = gt * lax.rsqrt(jnp.mean(gt * gt, axis=-1, keepdims=True) + EPS) * nw_ref[...]
    out_ref[...] = yall.astype(out_ref.dtype)


def _ssd_pass(bin_, yf, cw, cb, dtb, alog, dsk, nw, batch, seq, rev):
    t = SSD_T
    nct = seq // t
    hb = t // SUBLANE
    nrow8 = batch * seq // SUBLANE

    def cidx(i):
        return (nct - 1 - i) if rev else i

    main = pl.BlockSpec((t, SSD_IN), lambda b, i: (b * nct + cidx(i), 0))
    prev = pl.BlockSpec((SUBLANE, SSD_IN),
                        lambda b, i: (jnp.maximum((b * nct + cidx(i)) * hb - 1, 0), 0))
    nxt = pl.BlockSpec((SUBLANE, SSD_IN),
                       lambda b, i: (jnp.minimum((b * nct + cidx(i) + 1) * hb, nrow8 - 1), 0))
    full = lambda r, c: pl.BlockSpec((r, c), lambda b, i: (0, 0))
    yspec = pl.BlockSpec((t, SSD_INNER), lambda b, i: (b * nct + cidx(i), 0))
    if rev:
        in_specs = [main, prev, nxt, yspec, full(SSD_CONV, SSD_CONV_DIM), full(1, SSD_CONV_DIM),
                    full(1, LANE), full(1, LANE), full(1, SSD_INNER), full(1, SSD_INNER)]
        args = (bin_, bin_, bin_, yf, cw, cb, dtb, alog, dsk, nw)
    else:
        in_specs = [main, prev, nxt, full(SSD_CONV, SSD_CONV_DIM), full(1, SSD_CONV_DIM),
                    full(1, LANE), full(1, LANE)]
        args = (bin_, bin_, bin_, cw, cb, dtb, alog)
    return pl.pallas_call(
        functools.partial(_ssd_kernel, rev=rev, nct=nct, t=t),
        grid=(batch, nct),
        in_specs=in_specs,
        out_specs=yspec,
        out_shape=jax.ShapeDtypeStruct((batch * seq, SSD_INNER), BF16 if rev else F32),
        scratch_shapes=[pltpu.VMEM((SSD_HEADS // 2, SSD_STATE, LANE), F32)],
        compiler_params=_cparams("parallel", "arbitrary"),
        name="ssd_bwd" if rev else "ssd_fwd",
    )(*args)


def _outproj_kernel(x_ref, mla_ref, ssd_ref, dif_ref, lam_ref, sub_ref, wo_ref, fn_ref, wrh_ref, wrl_ref,
                    xo_ref, hn_ref, aff_ref, *, lambda_init):
    lam = lam_ref[...]
    lam_full = (jnp.exp(jnp.sum(lam[0:1] * lam[1:2], keepdims=True))
                - jnp.exp(jnp.sum(lam[2:3] * lam[3:4], keepdims=True)) + lambda_init)
    dif = dif_ref[0]
    outs = []
    for hd in range(DIFF_HEADS):
        o = dif[2 * hd * DIFF_V:(2 * hd + 1) * DIFF_V] - lam_full * dif[(2 * hd + 1) * DIFF_V:(2 * hd + 2) * DIFF_V]
        o = o * lax.rsqrt(jnp.mean(o * o, axis=0, keepdims=True) + EPS) * sub_ref[...] * (1.0 - lambda_init)
        outs.append(o)
    dt_ = jnp.concatenate(outs, axis=0).astype(BF16)
    wo = wo_ref[...]
    x = x_ref[...]
    x = x + _dot_tn(mla_ref[0].astype(BF16), wo[:MLA_WIDTH])
    x = x + _dot(ssd_ref[...].astype(BF16), wo[MLA_WIDTH:MLA_WIDTH + SSD_INNER])
    x = x + _dot_tn(dt_, wo[MLA_WIDTH + SSD_INNER:])
    xo_ref[...] = x
    hn = x * lax.rsqrt(jnp.mean(x * x, axis=-1, keepdims=True) + EPS) * fn_ref[...]
    hn_ref[...] = hn
    hh, hl = _split_bf16(hn, 2)
    logits = _dot(hh, wrh_ref[...]) + _dot(hl, wrh_ref[...]) + _dot(hh, wrl_ref[...])
    lane = lax.broadcasted_iota(I32, logits.shape, 1)
    logits = jnp.where(lane < N_EXPERTS, logits, -jnp.inf)
    e = jnp.exp(logits - jnp.max(logits, axis=-1, keepdims=True))
    aff = e / jnp.sum(e, axis=-1, keepdims=True)
    aff_ref[...] = aff.T[:N_EXPERTS]


def _outproj(x2, mla_t, ssd, dif_t, lam, sub, wo, fn, wrh, wrl, seq, lambda_init):
    t = x2.shape[0]
    tm = TOKEN_TILE
    nt = seq // tm
    full = lambda r, c: pl.BlockSpec((r, c), lambda i: (0, 0))
    row = lambda c: pl.BlockSpec((tm, c), lambda i: (i, 0))
    return pl.pallas_call(
        functools.partial(_outproj_kernel, lambda_init=lambda_init),
        grid=(t // tm,),
        in_specs=[row(D_MODEL),
                  pl.BlockSpec((1, MLA_WIDTH, tm), lambda i: (i // nt, 0, i % nt)),
                  row(SSD_INNER),
                  pl.BlockSpec((1, 2 * DIFF_WIDTH, tm), lambda i: (i // nt, 0, i % nt)),
                  full(4, DIFF_QK), full(DIFF_V, 1), full(D_MODEL, D_MODEL), full(1, D_MODEL),
                  full(D_MODEL, LANE), full(D_MODEL, LANE)],
        out_specs=[row(D_MODEL), row(D_MODEL), pl.BlockSpec((N_EXPERTS, tm), lambda i: (0, i))],
        out_shape=[jax.ShapeDtypeStruct((t, D_MODEL), F32),
                   jax.ShapeDtypeStruct((t, D_MODEL), F32),
                   jax.ShapeDtypeStruct((N_EXPERTS, t), F32)],
        compiler_params=_cparams("parallel"),
        name="outproj",
    )(x2, mla_t, ssd, dif_t, lam, sub, wo, fn, wrh, wrl)


def _cumsum_rowmajor(mask_f32, tri_u, tri_ls):
    local = _dot(mask_f32.astype(BF16), tri_u)
    rowtot = jnp.broadcast_to(local[:, LANE - 1:LANE], local.shape).astype(BF16)
    return local + _dot(tri_ls, rowtot)


def _select_kernel(aff_ref, pos_ref, idx_ref, toff_ref, tcnt_ref, cum_scr, *, cap, sb, rows):
    a = aff_ref[0]
    key = lax.bitcast_convert_type(a, I32)

    def bit_body(it, prefix):
        cand = prefix | lax.shift_left(jnp.int32(1), 30 - it)
        cnt = jnp.sum((key >= cand).astype(I32))
        return jnp.where(cnt >= cap, cand, prefix)

    thr = lax.fori_loop(0, 31, bit_body, jnp.int32(0))
    gt = key > thr
    eq = key == thr
    need = cap - jnp.sum(gt.astype(I32))
    r_i = lax.broadcasted_iota(I32, (LANE, LANE), 0)
    c_i = lax.broadcasted_iota(I32, (LANE, LANE), 1)
    tri_u = jnp.where(r_i <= c_i, 1.0, 0.0).astype(BF16)
    r_r = lax.broadcasted_iota(I32, (rows, rows), 0)
    c_r = lax.broadcasted_iota(I32, (rows, rows), 1)
    tri_ls = jnp.where(r_r > c_r, 1.0, 0.0).astype(BF16)
    eqf = jnp.where(eq, 1.0, 0.0)
    tie_rank = _cumsum_rowmajor(eqf, tri_u, tri_ls) - eqf
    sel = gt | (eq & (tie_rank < need.astype(F32)))
    self_ = jnp.where(sel, 1.0, 0.0)
    cum = _cumsum_rowmajor(self_, tri_u, tri_ls).astype(I32)
    pos_ref[0] = jnp.where(sel, cum - 1, -1)
    cum_scr[...] = cum
    rowend = cum[:, LANE - 1:LANE]
    rowstart = rowend - jnp.sum(self_, axis=-1, keepdims=True).astype(I32)
    toff_ref[0] = rowstart
    tcnt_ref[0] = rowend - rowstart

    def blk(bi, carry):
        p0 = bi * sb
        r_lo = jnp.sum((rowend <= p0).astype(I32))
        r_hi = jnp.sum((rowstart < p0 + sb).astype(I32))
        s_col = p0 + lax.broadcasted_iota(I32, (sb, LANE), 0)

        def rowbody(r, acc):
            return acc + (cum_scr[pl.ds(r, 1), :] <= s_col).astype(I32)

        acc = lax.fori_loop(r_lo, r_hi, rowbody, jnp.zeros((sb, LANE), I32))
        idx_ref[0, pl.ds(pl.multiple_of(p0, sb), sb), :] = jnp.sum(acc, axis=1, keepdims=True) + r_lo * LANE
        return carry

    lax.fori_loop(0, cap // sb, blk, 0)


def _select(aff3, cap):
    e, rows, _ = aff3.shape
    sb = min(IDX_SLOTS, cap)
    return pl.pallas_call(
        functools.partial(_select_kernel, cap=cap, sb=sb, rows=rows),
        grid=(e,),
        in_specs=[pl.BlockSpec((1, rows, LANE), lambda i: (i, 0, 0))],
        out_specs=[pl.BlockSpec((1, rows, LANE), lambda i: (i, 0, 0)),
                   pl.BlockSpec((1, cap, 1), lambda i: (i, 0, 0)),
                   pl.BlockSpec((1, rows, 1), lambda i: (i, 0, 0)),
                   pl.BlockSpec((1, rows, 1), lambda i: (i, 0, 0))],
        out_shape=[jax.ShapeDtypeStruct((e, rows, LANE), I32),
                   jax.ShapeDtypeStruct((e, cap, 1), I32),
                   jax.ShapeDtypeStruct((e, rows, 1), I32),
                   jax.ShapeDtypeStruct((e, rows, 1), I32)],
        scratch_shapes=[pltpu.VMEM((rows, LANE), I32)],
        compiler_params=_cparams("parallel"),
        name="select",
    )(aff3)


def _row_copy(src_hbm, src_row, dst_vmem, dst_row, sem):
    return pltpu.make_async_copy(src_hbm.at[pl.ds(src_row, 1)], dst_vmem.at[pl.ds(dst_row, 1)], sem)


def _ffn_kernel(idx_ref, nxt_ref, hn_hbm, wg_ref, wu_ref, wd_ref, ye_ref, xbuf, sem, *, slots, steps):
    n = pl.program_id(0) * pl.num_programs(1) + pl.program_id(1)
    cur = n % 2

    def gather(iref, buf_slot):
        for r in range(slots):
            _row_copy(hn_hbm, iref[0, 0, r], xbuf.at[buf_slot], r, sem.at[buf_slot]).start()

    def wait_all(buf_slot):
        pltpu.make_async_copy(hn_hbm.at[pl.ds(0, slots)], xbuf.at[buf_slot], sem.at[buf_slot]).wait()

    @pl.when(n == 0)
    def _():
        gather(idx_ref, 0)

    gather(nxt_ref, 1 - cur)
    wait_all(cur)
    x = xbuf[cur].astype(BF16)
    gate = _dot(x, wg_ref[0])
    up = _dot(x, wu_ref[0])
    hid = (_silu(gate) * up).astype(BF16)
    ye_ref[...] = _dot(hid, wd_ref[0]).astype(BF16)

    @pl.when(n == steps - 1)
    def _():
        wait_all(1 - cur)


def _ffn(idx3, hn, wg, wu, wd, cap):
    slots = idx3.shape[-1]
    nb = cap // slots
    steps = N_EXPERTS * nb
    wspec = lambda: pl.BlockSpec((1, D_MODEL, EXPERT_FF), lambda e, b: (e, 0, 0))
    ispec = lambda f: pl.BlockSpec((1, 1, slots), f, memory_space=pltpu.SMEM)
    return pl.pallas_call(
        functools.partial(_ffn_kernel, slots=slots, steps=steps),
        grid=(N_EXPERTS, nb),
        in_specs=[ispec(lambda e, b: (e * nb + b, 0, 0)),
                  ispec(lambda e, b: (jnp.minimum(e * nb + b + 1, steps - 1), 0, 0)),
                  pl.BlockSpec(memory_space=pl.ANY),
                  wspec(), wspec(),
                  pl.BlockSpec((1, EXPERT_FF, D_MODEL), lambda e, b: (e, 0, 0))],
        out_specs=pl.BlockSpec((slots, D_MODEL), lambda e, b: (e * nb + b, 0)),
        out_shape=jax.ShapeDtypeStruct((N_EXPERTS * cap, D_MODEL), BF16),
        scratch_shapes=[pltpu.VMEM((2, slots, D_MODEL), F32), pltpu.SemaphoreType.DMA((2,))],
        compiler_params=_cparams("arbitrary", "arbitrary"),
        name="expert_ffn",
    )(idx3, idx3, hn, wg, wu, wd)


YE_SHIFT = 4
YE_BLOCK = 1 << YE_SHIFT
COMBINE_MAX_BLOCKS = N_EXPERTS * (COMBINE_TILE // YE_BLOCK + 1)
COMBINE_MAX_ROWS = -(-COMBINE_MAX_BLOCKS * YE_BLOCK // 256) * 256


def _combine_kernel(toff_ref, tcnt_ref, x_ref, pos_ref, aff_ref, ye_hbm, out_ref, buf, sem, *, cap, rows, chunk):
    r = pl.program_id(0)
    cur = r % 2

    def layout(tile):
        firsts, nblks, dsts = [], [], []
        dst = jnp.int32(0)
        for e in range(N_EXPERTS):
            off = toff_ref[e * rows + tile]
            cnt = tcnt_ref[e * rows + tile]
            first = (off >> YE_SHIFT) << YE_SHIFT
            firsts.append(first)
            nblks.append(jnp.where(cnt > 0, (off + cnt - first + YE_BLOCK - 1) >> YE_SHIFT, 0))
            dsts.append(dst)
            dst = dst + nblks[-1]
        return firsts, nblks, dsts, dst

    def gather(tile, buf_slot):
        firsts, nblks, dsts, _ = layout(tile)
        for e in range(N_EXPERTS):
            def issue(j, carry, e=e):
                src = pl.multiple_of(e * cap + firsts[e] + j * YE_BLOCK, YE_BLOCK)
                dst = pl.multiple_of((dsts[e] + j) * YE_BLOCK, YE_BLOCK)
                pltpu.make_async_copy(ye_hbm.at[pl.ds(src, YE_BLOCK)], buf.at[buf_slot, pl.ds(dst, YE_BLOCK)],
                                      sem.at[buf_slot]).start()
                return carry

            lax.fori_loop(0, nblks[e], issue, 0)

    @pl.when(r == 0)
    def _():
        gather(0, 0)

    @pl.when(r + 1 < rows)
    def _():
        gather(r + 1, 1 - cur)

    firsts, _, dsts, nblk_total = layout(r)
    total = nblk_total * YE_BLOCK
    for bit in range(COMBINE_MAX_BLOCKS.bit_length()):
        @pl.when(((nblk_total >> bit) & 1) == 1)
        def _(bit=bit):
            n = YE_BLOCK << bit
            pltpu.make_async_copy(ye_hbm.at[pl.ds(0, n)], buf.at[cur, pl.ds(0, n)], sem.at[cur]).wait()

    out_ref[...] = x_ref[...]
    pos = pos_ref[...]
    aff = aff_ref[...]
    tgt = [jnp.where(pos[e:e + 1] >= 0, pos[e:e + 1] - firsts[e] + dsts[e] * YE_BLOCK, -1)
           for e in range(N_EXPERTS)]

    def chunk_body(k, carry):
        base = pl.multiple_of(k * chunk, chunk)
        cidx = base + lax.broadcasted_iota(I32, (chunk, COMBINE_TILE), 0)
        wt = jnp.zeros((chunk, COMBINE_TILE), F32)
        for e in range(N_EXPERTS):
            wt = wt + jnp.where(cidx == tgt[e], aff[e:e + 1], 0.0)
        rowid = base + lax.broadcasted_iota(I32, (chunk, 1), 0)
        rowsv = jnp.where(rowid < total, buf[cur, pl.ds(base, chunk), :], jnp.zeros((), BF16))
        wh, wl = _split_bf16(wt, 2)
        out_ref[...] += _dot_tn(wh, rowsv) + _dot_tn(wl, rowsv)
        return carry

    lax.fori_loop(0, (total + chunk - 1) // chunk, chunk_body, 0)


def _combine(toff, tcnt, x2, pos, aff_t, ye, cap):
    t = x2.shape[0]
    tile = COMBINE_TILE
    rows = t // tile
    chunk = 256
    grid_spec = pltpu.PrefetchScalarGridSpec(
        num_scalar_prefetch=2,
        grid=(rows,),
        in_specs=[pl.BlockSpec((tile, D_MODEL), lambda i, *_: (i, 0)),
                  pl.BlockSpec((N_EXPERTS, tile), lambda i, *_: (0, i)),
                  pl.BlockSpec((N_EXPERTS, tile), lambda i, *_: (0, i)),
                  pl.BlockSpec(memory_space=pl.ANY)],
        out_specs=pl.BlockSpec((tile, D_MODEL), lambda i, *_: (i, 0)),
        scratch_shapes=[pltpu.VMEM((2, COMBINE_MAX_ROWS, D_MODEL), BF16), pltpu.SemaphoreType.DMA((2,))],
    )
    return pl.pallas_call(
        functools.partial(_combine_kernel, cap=cap, rows=rows, chunk=chunk),
        grid_spec=grid_spec,
        out_shape=jax.ShapeDtypeStruct((t, D_MODEL), F32),
        compiler_params=_cparams("arbitrary"),
        name="combine",
    )(toff, tcnt, x2, pos, aff_t, ye)


def _rope_tables(seq):
    inv = ROPE_THETA ** (-jnp.arange(0, MLA_ROPE, 2, dtype=F32) / MLA_ROPE)
    ang = jnp.arange(seq, dtype=F32)[:, None] * inv[None, :]
    cos, sin = jnp.cos(ang), jnp.sin(ang)
    one = jnp.ones((seq, 1), F32)
    zero = jnp.zeros((seq, 1), F32)
    rep = lambda v, n: jnp.broadcast_to(v, (seq, n))
    half = MLA_ROPE // 2
    mla = (jnp.concatenate([rep(one, MLA_NOPE), cos, cos, rep(one, LANE - MLA_QK)], axis=1),
           jnp.concatenate([rep(zero, MLA_NOPE + half), sin, rep(zero, LANE - MLA_QK)], axis=1),
           jnp.concatenate([rep(zero, MLA_NOPE), -sin, rep(zero, LANE - MLA_NOPE - half)], axis=1))
    nblk = LANE // DIFF_QK
    dif = (jnp.tile(jnp.concatenate([cos, cos], axis=1), (1, nblk)),
           jnp.tile(jnp.concatenate([rep(zero, half), sin], axis=1), (1, nblk)),
           jnp.tile(jnp.concatenate([-sin, rep(zero, half)], axis=1), (1, nblk)))
    return mla, dif


def _pad_cols(w, width):
    return jnp.pad(w, ((0, 0), (0, width - w.shape[1])))


def _layer_params(p, l):
    w_in = p['w_in'][l]
    z64 = jnp.zeros((D_MODEL, 64), F32)
    z32 = jnp.zeros((D_MODEL, 32), F32)
    w_proj = jnp.concatenate([
        w_in[:, :OFF_KPE], z64, w_in[:, OFF_KPE:OFF_Z], z32,
        w_in[:, OFF_Z:OFF_DT], _pad_cols(w_in[:, OFF_DT:OFF_DQ], LANE),
        w_in[:, OFF_DQ:]], axis=1).astype(BF16)
    wq = p['mla_w_uq'][l].reshape(MLA_Q_LORA, MLA_HEADS, MLA_QK)
    wq = jnp.pad(wq, ((0, 0), (0, 0), (0, LANE - MLA_QK))).reshape(MLA_Q_LORA, MLA_HEADS * LANE)
    wkv = p['mla_w_ukv'][l].reshape(MLA_KV_LORA, MLA_HEADS, MLA_NOPE + MLA_V)
    wk = jnp.pad(wkv[:, :, :MLA_NOPE], ((0, 0), (0, 0), (0, LANE - MLA_NOPE))).reshape(MLA_KV_LORA, MLA_HEADS * LANE)
    wv = wkv[:, :, MLA_NOPE:].reshape(MLA_KV_LORA, MLA_WIDTH)
    wr = _pad_cols(p['w_router'][l], LANE)
    wrh = wr.astype(BF16)
    wrl = (wr - wrh.astype(F32)).astype(BF16)
    pad_row = lambda v, n: jnp.pad(v, (0, n - v.shape[0]))[None, :]
    return dict(
        attn_norm=p['attn_norm'][l][None, :], w_proj=w_proj,
        q_norm=p['mla_q_norm'][l][None, :], kv_norm=p['mla_kv_norm'][l][None, :],
        wq=wq.astype(BF16), wk=wk.astype(BF16), wv=wv.astype(BF16),
        q_gain=pad_row(p['mla_q_gain'][l], LANE), k_gain=pad_row(p['mla_k_gain'][l], LANE),
        conv_w=p['ssd_conv_w'][l], conv_b=p['ssd_conv_b'][l][None, :],
        dt_bias=pad_row(p['ssd_dt_bias'][l].reshape(-1), LANE),
        a_log=pad_row(p['ssd_a_log'][l].reshape(-1), LANE),
        d_skip=jnp.repeat(p['ssd_d'][l], SSD_HEAD_DIM)[None, :], ssd_norm=p['ssd_norm'][l][None, :],
        dq_gain=jnp.tile(p['diff_q_gain'][l], 2 * DIFF_HEADS)[None, :],
        dk_gain=jnp.tile(p['diff_k_gain'][l], 2 * DIFF_HEADS)[None, :],
        lam=p['diff_lambda'][l], subln=p['diff_subln'][l][:, None],
        w_out=p['w_out'][l].astype(BF16), ffn_norm=p['ffn_norm'][l][None, :],
        wrh=wrh, wrl=wrl,
        w_gate=p['w_gate'][l].astype(BF16), w_up=p['w_up'][l].astype(BF16), w_down=p['w_down'][l].astype(BF16),
    )


def _block_diag_ones(n, blk):
    i = jnp.arange(n)
    return (i[:, None] // blk == i[None, :] // blk).astype(BF16)


def _trunk(x, p):
    batch, seq, _ = x.shape
    t = batch * seq
    cap = EC_CAPACITY * t // N_EXPERTS
    tk = min(ATTN_TK, seq)
    mla_tabs, dif_tabs = _rope_tables(seq)
    bd = _block_diag_ones(2 * DIFF_HEADS * DIFF_QK, DIFF_QK)
    x2 = x.reshape(t, D_MODEL)
    rows = t // LANE
    for l in range(DEPTH):
        lp = _layer_params(p, l)
        a_in, b_in, c_in = _inproj(x2, lp['attn_norm'], lp['w_proj'])
        q, k, vt = _mla_prep(a_in, lp['q_norm'], lp['kv_norm'], lp['wq'], lp['wk'], lp['wv'],
                             lp['q_gain'], lp['k_gain'], mla_tabs, seq)
        mla_bound = (MLA_QK ** 0.5 * LOG2E) * jnp.max(jnp.abs(p['mla_q_gain'][l])) * jnp.max(jnp.abs(p['mla_k_gain'][l]))
        mla_t = _attention(q.reshape(batch, seq, -1), k.reshape(batch, seq, -1),
                           vt.reshape(batch, seq // tk, MLA_HEADS * V_ROWS, tk),
                           MLA_HEADS, lambda h: h, lambda h: h, MLA_V, tk, mla_bound, BF16)
        dq, dk, dvt = _diff_prep(c_in, lp['dq_gain'], lp['dk_gain'], bd, dif_tabs, seq)
        dif_bound = (DIFF_QK ** 0.5 * LOG2E) * jnp.max(jnp.abs(p['diff_q_gain'][l])) * jnp.max(jnp.abs(p['diff_k_gain'][l]))
        dif_t = _attention(dq.reshape(batch, seq, -1), dk.reshape(batch, seq, -1),
                           dvt.reshape(batch, seq // tk, DIFF_HEADS * V_ROWS, tk),
                           2 * DIFF_HEADS, lambda h: h // (LANE // DIFF_QK), lambda h: h // 2, DIFF_V, tk,
                           dif_bound, F32)
        yf = _ssd_pass(b_in, None, lp['conv_w'], lp['conv_b'], lp['dt_bias'], lp['a_log'],
                       None, None, batch, seq, rev=False)
        ssd = _ssd_pass(b_in, yf, lp['conv_w'], lp['conv_b'], lp['dt_bias'], lp['a_log'],
                        lp['d_skip'], lp['ssd_norm'], batch, seq, rev=True)
        lambda_init = 0.8 - 0.6 * math.exp(-0.3 * l)
        x2, hn, aff_t = _outproj(x2, mla_t, ssd, dif_t, lp['lam'], lp['subln'], lp['w_out'],
                                 lp['ffn_norm'], lp['wrh'], lp['wrl'], seq, lambda_init)
        pos3, idx, toff, tcnt = _select(aff_t.reshape(N_EXPERTS, rows, LANE), cap)
        slots = min(FFN_SLOTS, cap)
        ye = _ffn(idx.reshape(N_EXPERTS * cap // slots, 1, slots), hn,
                  lp['w_gate'], lp['w_up'], lp['w_down'], cap)
        x2 = _combine(toff.reshape(-1), tcnt.reshape(-1), x2, pos3.reshape(N_EXPERTS, t), aff_t, ye, cap)
    return x2.reshape(batch, seq, D_MODEL)


def kernel(x_prompt, x_sample, attn_norm, w_in, mla_q_norm, mla_kv_norm, mla_w_uq, mla_w_ukv, mla_q_gain,
           mla_k_gain, ssd_conv_w, ssd_conv_b, ssd_dt_bias, ssd_a_log, ssd_d, ssd_norm, diff_q_gain,
           diff_k_gain, diff_lambda, diff_subln, w_out, ffn_norm, w_router, w_gate, w_up, w_down):
    p = dict(attn_norm=attn_norm, w_in=w_in, mla_q_norm=mla_q_norm, mla_kv_norm=mla_kv_norm,
             mla_w_uq=mla_w_uq, mla_w_ukv=mla_w_ukv, mla_q_gain=mla_q_gain, mla_k_gain=mla_k_gain,
             ssd_conv_w=ssd_conv_w, ssd_conv_b=ssd_conv_b, ssd_dt_bias=ssd_dt_bias, ssd_a_log=ssd_a_log,
             ssd_d=ssd_d, ssd_norm=ssd_norm, diff_q_gain=diff_q_gain, diff_k_gain=diff_k_gain,
             diff_lambda=diff_lambda, diff_subln=diff_subln, w_out=w_out, ffn_norm=ffn_norm,
             w_router=w_router, w_gate=w_gate, w_up=w_up, w_down=w_down)
    return _trunk(x_prompt, p), _trunk(x_sample, p)
```

```python
import functools
import math

import jax
import jax.numpy as jnp
from jax import lax
from jax.experimental import pallas as pl
from jax.experimental.pallas import tpu as pltpu

F32 = jnp.float32
BF16 = jnp.bfloat16
I32 = jnp.int32

D_MODEL = 1024
DEPTH = 2
EPS = 1e-6
ROPE_THETA = 10000.0

MLA_HEADS = 6
MLA_Q_LORA = 256
MLA_KV_LORA = 128
MLA_NOPE = 64
MLA_ROPE = 32
MLA_QK = MLA_NOPE + MLA_ROPE
MLA_V = 64
MLA_WIDTH = MLA_HEADS * MLA_V

SSD_HEADS = 6
SSD_HEAD_DIM = 64
SSD_INNER = SSD_HEADS * SSD_HEAD_DIM
SSD_GROUPS = 2
SSD_STATE = 128
SSD_CONV = 5
SSD_CONV_DIM = SSD_INNER + 2 * SSD_GROUPS * SSD_STATE

DIFF_HEADS = 4
DIFF_QK = 32
DIFF_V = 2 * DIFF_QK
DIFF_WIDTH = DIFF_HEADS * DIFF_V

N_EXPERTS = 16
EC_CAPACITY = 2
EXPERT_FF = 1024

OFF_KV = MLA_Q_LORA
OFF_KPE = OFF_KV + MLA_KV_LORA
OFF_Z = OFF_KPE + MLA_ROPE
OFF_XBC = OFF_Z + SSD_INNER
OFF_DT = OFF_XBC + SSD_CONV_DIM
OFF_DQ = OFF_DT + 2 * SSD_HEADS
OFF_DK = OFF_DQ + DIFF_HEADS * 2 * DIFF_QK
OFF_DV = OFF_DK + DIFF_HEADS * 2 * DIFF_QK
IN_COLS = OFF_DV + DIFF_WIDTH

LANE = 128
SUBLANE = 8

MLA_IN = 512
SSD_IN = SSD_INNER + SSD_CONV_DIM + LANE
DIFF_IN = 768
PROJ_COLS = MLA_IN + SSD_IN + DIFF_IN

TOKEN_TILE = 512
ATTN_TQ = 512
ATTN_TK = 512
ATTN_GROUP = 16
SSD_T = 256
COMBINE_TILE = 128
V_ROWS = 80
LOG2E = math.log2(math.e)
NOSHIFT_MAX_LOG2 = 60.0
FFN_SLOTS = 512
IDX_SLOTS = 256
VMEM_LIMIT = 56 * 1024 * 1024


def _cparams(*sem):
    return pltpu.CompilerParams(dimension_semantics=sem, vmem_limit_bytes=VMEM_LIMIT)


def _split_bf16(a, terms):
    parts = []
    rem = a
    for _ in range(terms):
        p = rem.astype(BF16)
        parts.append(p)
        rem = rem - p.astype(F32)
    return parts


def _dot(a, b):
    return jnp.dot(a, b, preferred_element_type=F32)


def _dot_nt(a, b):
    return lax.dot_general(a, b, (((1,), (1,)), ((), ())), preferred_element_type=F32)


def _dot_tn(a, b):
    return lax.dot_general(a, b, (((0,), (0,)), ((), ())), preferred_element_type=F32)


def _dot_split_lhs(a_f32, b_bf16, terms):
    out = None
    for p in _split_bf16(a_f32, terms):
        d = _dot(p, b_bf16)
        out = d if out is None else out + d
    return out


def _dot_split_rhs(a_bf16, b_f32, terms):
    out = None
    for p in _split_bf16(b_f32, terms):
        d = _dot(a_bf16, p)
        out = d if out is None else out + d
    return out


def _silu(x):
    return x * jax.nn.sigmoid(x)


def _inproj_kernel(x_ref, g_ref, w_ref, a_ref, b_ref, c_ref):
    x = x_ref[...]
    ms = jnp.mean(x * x, axis=-1, keepdims=True)
    h = (x * lax.rsqrt(ms + EPS) * g_ref[...]).astype(BF16)
    y = _dot(h, w_ref[...])
    a_ref[...] = y[:, :MLA_IN]
    b_ref[...] = y[:, MLA_IN:MLA_IN + SSD_IN]
    c_ref[...] = y[:, MLA_IN + SSD_IN:]


def _inproj(x2, gain, w):
    t = x2.shape[0]
    tm = TOKEN_TILE
    return pl.pallas_call(
        _inproj_kernel,
        grid=(t // tm,),
        in_specs=[pl.BlockSpec((tm, D_MODEL), lambda i: (i, 0)),
                  pl.BlockSpec((1, D_MODEL), lambda i: (0, 0)),
                  pl.BlockSpec((D_MODEL, PROJ_COLS), lambda i: (0, 0))],
        out_specs=[pl.BlockSpec((tm, MLA_IN), lambda i: (i, 0)),
                   pl.BlockSpec((tm, SSD_IN), lambda i: (i, 0)),
                   pl.BlockSpec((tm, DIFF_IN), lambda i: (i, 0))],
        out_shape=[jax.ShapeDtypeStruct((t, MLA_IN), F32),
                   jax.ShapeDtypeStruct((t, SSD_IN), F32),
                   jax.ShapeDtypeStruct((t, DIFF_IN), F32)],
        compiler_params=_cparams("parallel"),
        name="inproj",
    )(x2, gain, w)


def _rope(x, c, sa, sb, shift):
    return x * c + pltpu.roll(x, shift, 1) * sa + pltpu.roll(x, LANE - shift, 1) * sb


def _mla_prep_kernel(a_ref, qn_ref, kvn_ref, wq_ref, wk_ref, wv_ref, qg_ref, kg_ref,
                     c_ref, sa_ref, sb_ref, q_ref, k_ref, vt_ref):
    a = a_ref[...]
    cq = a[:, :MLA_Q_LORA]
    ckv = a[:, MLA_Q_LORA:MLA_Q_LORA + MLA_KV_LORA]
    kpe = a[:, MLA_Q_LORA + MLA_KV_LORA:]
    cqn = (cq * lax.rsqrt(jnp.mean(cq * cq, axis=-1, keepdims=True) + EPS) * qn_ref[...]).astype(BF16)
    ckvn = (ckv * lax.rsqrt(jnp.mean(ckv * ckv, axis=-1, keepdims=True) + EPS) * kvn_ref[...]).astype(BF16)
    q = _dot(cqn, wq_ref[...])
    kn = _dot(ckvn, wk_ref[...])
    v = _dot(ckvn, wv_ref[...])
    c = c_ref[...]
    sa = sa_ref[...]
    sb = sb_ref[...]
    scale = MLA_QK ** -0.5 * LOG2E
    for h in range(MLA_HEADS):
        sl = slice(h * LANE, (h + 1) * LANE)
        qh = q[:, sl]
        qh = qh * lax.rsqrt(jnp.sum(qh * qh, axis=-1, keepdims=True) / MLA_QK + EPS) * qg_ref[...]
        q_ref[:, sl] = (_rope(qh, c, sa, sb, MLA_ROPE // 2) * scale).astype(BF16)
        kh = kn[:, sl] + kpe
        kh = kh * lax.rsqrt(jnp.sum(kh * kh, axis=-1, keepdims=True) / MLA_QK + EPS) * kg_ref[...]
        k_ref[:, sl] = _rope(kh, c, sa, sb, MLA_ROPE // 2).astype(BF16)
    vt = v.T.astype(BF16)
    ones = jnp.ones((V_ROWS - MLA_V, vt.shape[1]), BF16)
    for h in range(MLA_HEADS):
        vt_ref[0, h * V_ROWS:(h + 1) * V_ROWS, :] = jnp.concatenate([vt[h * MLA_V:(h + 1) * MLA_V], ones], axis=0)


def _mla_prep(a, qn, kvn, wq, wk, wv, qg, kg, tabs, seq):
    t = a.shape[0]
    tm = TOKEN_TILE
    nt = seq // tm
    full = lambda r, c: pl.BlockSpec((r, c), lambda i: (0, 0))
    tab = pl.BlockSpec((tm, LANE), lambda i: (i % nt, 0))
    return pl.pallas_call(
        _mla_prep_kernel,
        grid=(t // tm,),
        in_specs=[pl.BlockSpec((tm, MLA_IN), lambda i: (i, 0)),
                  full(1, MLA_Q_LORA), full(1, MLA_KV_LORA),
                  full(MLA_Q_LORA, MLA_HEADS * LANE), full(MLA_KV_LORA, MLA_HEADS * LANE),
                  full(MLA_KV_LORA, MLA_WIDTH), full(1, LANE), full(1, LANE), tab, tab, tab],
        out_specs=[pl.BlockSpec((tm, MLA_HEADS * LANE), lambda i: (i, 0)),
                   pl.BlockSpec((tm, MLA_HEADS * LANE), lambda i: (i, 0)),
                   pl.BlockSpec((1, MLA_HEADS * V_ROWS, tm), lambda i: (i, 0, 0))],
        out_shape=[jax.ShapeDtypeStruct((t, MLA_HEADS * LANE), BF16),
                   jax.ShapeDtypeStruct((t, MLA_HEADS * LANE), BF16),
                   jax.ShapeDtypeStruct((t // tm, MLA_HEADS * V_ROWS, tm), BF16)],
        compiler_params=_cparams("parallel"),
        name="mla_prep",
    )(a, qn, kvn, wq, wk, wv, qg, kg, *tabs)


def _diff_prep_kernel(cin_ref, qg_ref, kg_ref, bd_ref, c_ref, sa_ref, sb_ref, q_ref, k_ref, vt_ref):
    cin = cin_ref[...]
    bd = bd_ref[...]
    c = jnp.concatenate([c_ref[...], c_ref[...]], axis=1)
    sa = jnp.concatenate([sa_ref[...], sa_ref[...]], axis=1)
    sb = jnp.concatenate([sb_ref[...], sb_ref[...]], axis=1)
    lane = lax.broadcasted_iota(I32, (1, LANE), 1)

    def norm_rope(x, g):
        ms = _dot_split_lhs(x * x, bd, 3) / DIFF_QK
        x = x * lax.rsqrt(ms + EPS) * g
        halves = []
        for j in range(2):
            sl = slice(j * LANE, (j + 1) * LANE)
            halves.append(_rope(x[:, sl], c[:, sl], sa[:, sl], sb[:, sl], DIFF_QK // 2))
        return halves

    qh = norm_rope(cin[:, :256], qg_ref[...])
    kh = norm_rope(cin[:, 256:512], kg_ref[...])
    scale = DIFF_QK ** -0.5 * LOG2E
    for j in range(2 * DIFF_HEADS):
        grp, sub = divmod(j, LANE // DIFF_QK)
        keep = (lane >= sub * DIFF_QK) & (lane < (sub + 1) * DIFF_QK)
        q_ref[:, j * LANE:(j + 1) * LANE] = jnp.where(keep, qh[grp] * scale, 0.0).astype(BF16)
    k_ref[...] = jnp.concatenate(kh, axis=1).astype(BF16)
    vt = cin[:, 512:].T.astype(BF16)
    ones = jnp.ones((V_ROWS - DIFF_V, vt.shape[1]), BF16)
    for h in range(DIFF_HEADS):
        vt_ref[0, h * V_ROWS:(h + 1) * V_ROWS, :] = jnp.concatenate([vt[h * DIFF_V:(h + 1) * DIFF_V], ones], axis=0)


def _diff_prep(cin, qg, kg, bd, tabs, seq):
    t = cin.shape[0]
    tm = TOKEN_TILE
    nt = seq // tm
    full = lambda r, c: pl.BlockSpec((r, c), lambda i: (0, 0))
    tab = pl.BlockSpec((tm, LANE), lambda i: (i % nt, 0))
    nq = 2 * DIFF_HEADS * LANE
    return pl.pallas_call(
        _diff_prep_kernel,
        grid=(t // tm,),
        in_specs=[pl.BlockSpec((tm, DIFF_IN), lambda i: (i, 0)),
                  full(1, 256), full(1, 256), full(256, 256), tab, tab, tab],
        out_specs=[pl.BlockSpec((tm, nq), lambda i: (i, 0)),
                   pl.BlockSpec((tm, 256), lambda i: (i, 0)),
                   pl.BlockSpec((1, DIFF_HEADS * V_ROWS, tm), lambda i: (i, 0, 0))],
        out_shape=[jax.ShapeDtypeStruct((t, nq), BF16),
                   jax.ShapeDtypeStruct((t, 256), BF16),
                   jax.ShapeDtypeStruct((t // tm, DIFF_HEADS * V_ROWS, tm), BF16)],
        compiler_params=_cparams("parallel"),
        name="diff_prep",
    )(cin, qg, kg, bd, *tabs)


def _attn_noshift_kernel(q_ref, k_ref, vt_ref, o_ref, acc_scr, l_scr, pa_scr, pb_scr, *, nkc, tk, dv):
    q = q_ref[0]

    def probs(kc, p_ref):
        k = k_ref[0, pl.ds(pl.multiple_of(kc * tk, tk), tk), :]
        p = jnp.exp2(_dot_nt(k, q))
        p_ref[...] = p.astype(BF16)
        return jnp.sum(p, axis=0, keepdims=True)

    def pv(kc, p_ref):
        return _dot(vt_ref[0, kc, :dv, :], p_ref[...])

    bufs = (pa_scr, pb_scr)
    unroll = ATTN_GROUP

    def run(first, count, last):
        tot = None
        lsum = None
        for j in range(count):
            if not (last and j == count - 1):
                ls = probs(first + j + 1, bufs[(j + 1) % 2])
                lsum = ls if lsum is None else lsum + ls
            d = pv(first + j, bufs[j % 2])
            tot = d if tot is None else tot + d
        return tot, lsum

    l_scr[...] = probs(0, pa_scr)
    acc_scr[...] = jnp.zeros_like(acc_scr)
    trips = (nkc - 1) // unroll

    def body(i, carry):
        tot, lsum = run(i * unroll, unroll, False)
        acc_scr[...] += tot
        l_scr[...] += lsum
        return carry

    lax.fori_loop(0, trips, body, 0)
    tot, lsum = run(trips * unroll, nkc - trips * unroll, True)
    den = l_scr[...] if lsum is None else l_scr[...] + lsum
    o_ref[0] = ((acc_scr[...] + tot) / den).astype(o_ref.dtype)


def _attn_online_kernel(q_ref, k_ref, vt_ref, o_ref, m_scr, acc_scr, *, nkc, tk, dv):
    q = q_ref[0]
    m_scr[...] = jnp.full_like(m_scr, -jnp.inf)
    acc_scr[...] = jnp.zeros_like(acc_scr)

    def body(kc, carry):
        k = k_ref[0, pl.ds(pl.multiple_of(kc * tk, tk), tk), :]
        s = _dot_nt(k, q)
        m_prev = m_scr[...]
        m_new = jnp.maximum(m_prev, jnp.max(s, axis=0, keepdims=True))
        p = jnp.exp2(s - m_new).astype(BF16)
        acc_scr[...] = jnp.exp2(m_prev - m_new) * acc_scr[...] + _dot(vt_ref[0, kc], p)
        m_scr[...] = m_new
        return carry

    lax.fori_loop(0, nkc, body, 0)
    acc = acc_scr[...]
    o_ref[0] = (acc[:dv] / acc[dv:dv + 1]).astype(o_ref.dtype)


def _attention(q, k, vt, heads, kmap, vmap, dv, tk, score_bound, out_dtype):
    b, s, _ = q.shape
    tq = min(ATTN_TQ, s)
    nkc = s // tk

    def call(body, scratch, name):
        return pl.pallas_call(
            functools.partial(body, nkc=nkc, tk=tk, dv=dv),
            grid=(b, heads, s // tq),
            in_specs=[pl.BlockSpec((1, tq, LANE), lambda bi, h, qi: (bi, qi, h)),
                      pl.BlockSpec((1, s, LANE), lambda bi, h, qi: (bi, 0, kmap(h))),
                      pl.BlockSpec((1, nkc, V_ROWS, tk), lambda bi, h, qi: (bi, 0, vmap(h), 0))],
            out_specs=pl.BlockSpec((1, dv, tq), lambda bi, h, qi: (bi, h, qi)),
            out_shape=jax.ShapeDtypeStruct((b, heads * dv, s), out_dtype),
            scratch_shapes=scratch,
            compiler_params=_cparams("parallel", "parallel", "arbitrary"),
            name=name,
        )

    acc = pltpu.VMEM((V_ROWS, tq), F32)
    pbuf = pltpu.VMEM((tk, tq), BF16)
    fast = call(_attn_noshift_kernel, [pltpu.VMEM((dv, tq), F32), pltpu.VMEM((1, tq), F32), pbuf, pbuf],
                "attention")
    safe = call(_attn_online_kernel, [pltpu.VMEM((1, tq), F32), acc], "attention_online")
    return lax.cond(score_bound < NOSHIFT_MAX_LOG2, fast, safe, q, k, vt)


def _ssd_kernel(*refs, rev, nct, t):
    if rev:
        (main_ref, prev_ref, next_ref, yf_ref, cw_ref, cb_ref, dtb_ref, alog_ref, dsk_ref, nw_ref,
         out_ref, h_scr) = refs
    else:
        (main_ref, prev_ref, next_ref, cw_ref, cb_ref, dtb_ref, alog_ref, out_ref, h_scr) = refs
    i = pl.program_id(1)
    c = (nct - 1 - i) if rev else i

    @pl.when(i == 0)
    def _():
        h_scr[...] = jnp.zeros_like(h_scr)

    main = main_ref[...]
    xlo, xhi = SSD_INNER, SSD_INNER + SSD_CONV_DIM
    prev = jnp.where(c > 0, prev_ref[:, xlo:xhi], 0.0)
    nxt = jnp.where(c < nct - 1, next_ref[:, xlo:xhi], 0.0)
    xp = jnp.concatenate([prev, main[:, xlo:xhi], nxt], axis=0)
    acc = jnp.broadcast_to(cb_ref[...], (t, SSD_CONV_DIM))
    for j in range(SSD_CONV):
        sh = (SSD_CONV // 2 - j) % (t + 2 * SUBLANE)
        r = xp if sh == 0 else pltpu.roll(xp, sh, 0)
        acc = acc + r[SUBLANE:SUBLANE + t] * cw_ref[j:j + 1, :]
    xc = _silu(acc)
    xs = xc[:, :SSD_INNER]
    gn = SSD_GROUPS * SSD_STATE
    bmat = xc[:, SSD_INNER:SSD_INNER + gn]
    cmat = xc[:, SSD_INNER + gn:]

    dtr = main[:, xhi:] + dtb_ref[...]
    dt = jnp.maximum(dtr, 0.0) + jnp.log1p(jnp.exp(-jnp.abs(dtr)))
    a = dt * (-jnp.exp(alog_ref[...]))

    row = lax.broadcasted_iota(I32, (t, t), 0)
    col = lax.broadcasted_iota(I32, (t, t), 1)
    lower = row >= col
    tri_l = jnp.where(lower, 1.0, 0.0).astype(BF16)
    tri_u = jnp.where(row <= col, 1.0, 0.0).astype(BF16)
    a_t = a.T
    cs = _dot_split_rhs(tri_l, a, 3)
    cs_t = _dot_split_lhs(a_t, tri_u, 3)
    tot = cs[t - 1:t, :]
    if rev:
        ecol = cs - a
        erow = cs_t - a_t
    lane = lax.broadcasted_iota(I32, (1, LANE), 1)
    hb = SSD_HEADS if rev else 0
    hpg = SSD_HEADS // SSD_GROUPS

    g = []
    bt = []
    for grp in range(SSD_GROUPS):
        bg = bmat[:, grp * SSD_STATE:(grp + 1) * SSD_STATE]
        cg = cmat[:, grp * SSD_STATE:(grp + 1) * SSD_STATE].astype(BF16)
        g.append((_dot_nt(cg, bg.astype(BF16)), cg))
        bt.append(bg.T.astype(BF16))

    ys = []
    for pair in range(SSD_HEADS // 2):
        xpair = xs[:, pair * LANE:(pair + 1) * LANE]
        hstate = h_scr[pair]
        hbf = hstate.astype(BF16)
        y = jnp.zeros((t, LANE), F32)
        hnew = jnp.zeros((SSD_STATE, LANE), F32)
        dec = jnp.zeros((1, LANE), F32)
        for sub in range(2):
            h = pair * 2 + sub
            grp = h // hpg
            gmat, cg = g[grp]
            keep = (lane >= sub * SSD_HEAD_DIM) & (lane < (sub + 1) * SSD_HEAD_DIM)
            hl = hb + h
            dtc = dt[:, hl:hl + 1]
            xm = jnp.where(keep, xpair * dtc, 0.0)
            if rev:
                dmat = jnp.exp(jnp.where(row <= col, erow[hl:hl + 1, :] - ecol[:, hl:hl + 1], -jnp.inf))
                off_scale = jnp.exp(tot[:, hl:hl + 1] - ecol[:, hl:hl + 1])
                st_w = jnp.exp(ecol[:, hl:hl + 1])
            else:
                dmat = jnp.exp(jnp.where(lower, cs[:, hl:hl + 1] - cs_t[hl:hl + 1, :], -jnp.inf))
                off_scale = jnp.exp(cs[:, hl:hl + 1])
                st_w = jnp.exp(tot[:, hl:hl + 1] - cs[:, hl:hl + 1])
            y = y + _dot((gmat * dmat).astype(BF16), xm.astype(BF16))
            y = y + jnp.where(keep, _dot(cg, hbf) * off_scale, 0.0)
            hnew = hnew + _dot(bt[grp], (xm * st_w).astype(BF16))
            dec = dec + jnp.where(keep, jnp.exp(tot[:, hl:hl + 1]), 0.0)
        h_scr[pair] = hstate * dec + hnew
        ys.append(y)

    yall = jnp.concatenate(ys, axis=1)
    if rev:
        yall = yall + yf_ref[...] + xs * dsk_ref[...]
        gt = yall * _silu(main[:, :SSD_INNER])
        yall = gt * lax.rsqrt(jnp.mean(gt * gt, axis=-1, keepdims=True) + EPS) * nw_ref[...]
    out_ref[...] = yall.astype(out_ref.dtype)


def _ssd_pass(bin_, yf, cw, cb, dtb, alog, dsk, nw, batch, seq, rev):
    t = SSD_T
    nct = seq // t
    hb = t // SUBLANE
    nrow8 = batch * seq // SUBLANE

    def cidx(i):
        return (nct - 1 - i) if rev else i

    main = pl.BlockSpec((t, SSD_IN), lambda b, i: (b * nct + cidx(i), 0))
    prev = pl.BlockSpec((SUBLANE, SSD_IN),
                        lambda b, i: (jnp.maximum((b * nct + cidx(i)) * hb - 1, 0), 0))
    nxt = pl.BlockSpec((SUBLANE, SSD_IN),
                       lambda b, i: (jnp.minimum((b * nct + cidx(i) + 1) * hb, nrow8 - 1), 0))
    full = lambda r, c: pl.BlockSpec((r, c), lambda b, i: (0, 0))
    yspec = pl.BlockSpec((t, SSD_INNER), lambda b, i: (b * nct + cidx(i), 0))
    if rev:
        in_specs = [main, prev, nxt, yspec, full(SSD_CONV, SSD_CONV_DIM), full(1, SSD_CONV_DIM),
                    full(1, LANE), full(1, LANE), full(1, SSD_INNER), full(1, SSD_INNER)]
        args = (bin_, bin_, bin_, yf, cw, cb, dtb, alog, dsk, nw)
    else:
        in_specs = [main, prev, nxt, full(SSD_CONV, SSD_CONV_DIM), full(1, SSD_CONV_DIM),
                    full(1, LANE), full(1, LANE)]
        args = (bin_, bin_, bin_, cw, cb, dtb, alog)
    return pl.pallas_call(
        functools.partial(_ssd_kernel, rev=rev, nct=nct, t=t),
        grid=(batch, nct),
        in_specs=in_specs,
        out_specs=yspec,
        out_shape=jax.ShapeDtypeStruct((batch * seq, SSD_INNER), BF16 if rev else F32),
        scratch_shapes=[pltpu.VMEM((SSD_HEADS // 2, SSD_STATE, LANE), F32)],
        compiler_params=_cparams("parallel", "arbitrary"),
        name="ssd_bwd" if rev else "ssd_fwd",
    )(*args)


def _outproj_kernel(x_ref, mla_ref, ssd_ref, dif_ref, lam_ref, sub_ref, wo_ref, fn_ref, wrh_ref,
                    xo_ref, hn_ref, aff_ref, *, lambda_init):
    lam = lam_ref[...]
    lam_full = (jnp.exp(jnp.sum(lam[0:1] * lam[1:2], keepdims=True))
                - jnp.exp(jnp.sum(lam[2:3] * lam[3:4], keepdims=True)) + lambda_init)
    dif = dif_ref[0]
    outs = []
    for hd in range(DIFF_HEADS):
        o = dif[2 * hd * DIFF_V:(2 * hd + 1) * DIFF_V] - lam_full * dif[(2 * hd + 1) * DIFF_V:(2 * hd + 2) * DIFF_V]
        o = o * lax.rsqrt(jnp.mean(o * o, axis=0, keepdims=True) + EPS) * sub_ref[...] * (1.0 - lambda_init)
        outs.append(o)
    dt_ = jnp.concatenate(outs, axis=0).astype(BF16)
    wo = wo_ref[...]
    x = x_ref[...]
    x = x + _dot_tn(mla_ref[0].astype(BF16), wo[:MLA_WIDTH])
    x = x + _dot(ssd_ref[...].astype(BF16), wo[MLA_WIDTH:MLA_WIDTH + SSD_INNER])
    x = x + _dot_tn(dt_, wo[MLA_WIDTH + SSD_INNER:])
    xo_ref[...] = x
    hn = x * lax.rsqrt(jnp.mean(x * x, axis=-1, keepdims=True) + EPS) * fn_ref[...]
    hn_ref[...] = hn
    hh, hl = _split_bf16(hn, 2)
    both = _dot(hh, wrh_ref[...])
    logits = both[:, :LANE] + both[:, LANE:] + _dot(hl, wrh_ref[:, :LANE])
    lane = lax.broadcasted_iota(I32, logits.shape, 1)
    logits = jnp.where(lane < N_EXPERTS, logits, -jnp.inf)
    e = jnp.exp(logits - jnp.max(logits, axis=-1, keepdims=True))
    aff = e / jnp.sum(e, axis=-1, keepdims=True)
    aff_ref[...] = aff.T[:N_EXPERTS]


def _outproj(x2, mla_t, ssd, dif_t, lam, sub, wo, fn, wr2, seq, lambda_init):
    t = x2.shape[0]
    tm = TOKEN_TILE
    nt = seq // tm
    full = lambda r, c: pl.BlockSpec((r, c), lambda i: (0, 0))
    row = lambda c: pl.BlockSpec((tm, c), lambda i: (i, 0))
    return pl.pallas_call(
        functools.partial(_outproj_kernel, lambda_init=lambda_init),
        grid=(t // tm,),
        in_specs=[row(D_MODEL),
                  pl.BlockSpec((1, MLA_WIDTH, tm), lambda i: (i // nt, 0, i % nt)),
                  row(SSD_INNER),
                  pl.BlockSpec((1, 2 * DIFF_WIDTH, tm), lambda i: (i // nt, 0, i % nt)),
                  full(4, DIFF_QK), full(DIFF_V, 1), full(D_MODEL, D_MODEL), full(1, D_MODEL),
                  full(D_MODEL, 2 * LANE)],
        out_specs=[row(D_MODEL), row(D_MODEL), pl.BlockSpec((N_EXPERTS, tm), lambda i: (0, i))],
        out_shape=[jax.ShapeDtypeStruct((t, D_MODEL), F32),
                   jax.ShapeDtypeStruct((t, D_MODEL), F32),
                   jax.ShapeDtypeStruct((N_EXPERTS, t), F32)],
        compiler_params=_cparams("parallel"),
        name="outproj",
    )(x2, mla_t, ssd, dif_t, lam, sub, wo, fn, wr2)


def _cumsum_rowmajor(mask_f32, tri_u, tri_ls):
    local = _dot(mask_f32.astype(BF16), tri_u)
    rowtot = jnp.broadcast_to(local[:, LANE - 1:LANE], local.shape).astype(BF16)
    return local + _dot(tri_ls, rowtot)


def _select_kernel(aff_ref, pos_ref, idx_ref, toff_ref, tcnt_ref, cum_scr, *, cap, sb, rows):
    a = aff_ref[0]
    key = lax.bitcast_convert_type(a, I32)

    def bit_body(it, prefix):
        cand = prefix | lax.shift_left(jnp.int32(1), 30 - it)
        cnt = jnp.sum((key >= cand).astype(I32))
        return jnp.where(cnt >= cap, cand, prefix)

    thr = lax.fori_loop(0, 31, bit_body, jnp.int32(0))
    gt = key > thr
    eq = key == thr
    need = cap - jnp.sum(gt.astype(I32))
    r_i = lax.broadcasted_iota(I32, (LANE, LANE), 0)
    c_i = lax.broadcasted_iota(I32, (LANE, LANE), 1)
    tri_u = jnp.where(r_i <= c_i, 1.0, 0.0).astype(BF16)
    r_r = lax.broadcasted_iota(I32, (rows, rows), 0)
    c_r = lax.broadcasted_iota(I32, (rows, rows), 1)
    tri_ls = jnp.where(r_r > c_r, 1.0, 0.0).astype(BF16)
    eqf = jnp.where(eq, 1.0, 0.0)
    tie_rank = _cumsum_rowmajor(eqf, tri_u, tri_ls) - eqf
    sel = gt | (eq & (tie_rank < need.astype(F32)))
    self_ = jnp.where(sel, 1.0, 0.0)
    cum = _cumsum_rowmajor(self_, tri_u, tri_ls).astype(I32)
    pos_ref[0] = jnp.where(sel, cum - 1, -1)
    cum_scr[...] = cum
    rowend = cum[:, LANE - 1:LANE]
    rowstart = rowend - jnp.sum(self_, axis=-1, keepdims=True).astype(I32)
    toff_ref[0] = rowstart
    tcnt_ref[0] = rowend - rowstart

    def blk(bi, carry):
        p0 = bi * sb
        r_lo = jnp.sum((rowend <= p0).astype(I32))
        r_hi = jnp.sum((rowstart < p0 + sb).astype(I32))
        s_col = p0 + lax.broadcasted_iota(I32, (sb, LANE), 0)

        def rowbody(r, acc):
            return acc + (cum_scr[pl.ds(r, 1), :] <= s_col).astype(I32)

        acc = lax.fori_loop(r_lo, r_hi, rowbody, jnp.zeros((sb, LANE), I32))
        idx_ref[0, pl.ds(pl.multiple_of(p0, sb), sb), :] = jnp.sum(acc, axis=1, keepdims=True) + r_lo * LANE
        return carry

    lax.fori_loop(0, cap // sb, blk, 0)


def _select(aff3, cap):
    e, rows, _ = aff3.shape
    sb = min(IDX_SLOTS, cap)
    return pl.pallas_call(
        functools.partial(_select_kernel, cap=cap, sb=sb, rows=rows),
        grid=(e,),
        in_specs=[pl.BlockSpec((1, rows, LANE), lambda i: (i, 0, 0))],
        out_specs=[pl.BlockSpec((1, rows, LANE), lambda i: (i, 0, 0)),
                   pl.BlockSpec((1, cap, 1), lambda i: (i, 0, 0)),
                   pl.BlockSpec((1, rows, 1), lambda i: (i, 0, 0)),
                   pl.BlockSpec((1, rows, 1), lambda i: (i, 0, 0))],
        out_shape=[jax.ShapeDtypeStruct((e, rows, LANE), I32),
                   jax.ShapeDtypeStruct((e, cap, 1), I32),
                   jax.ShapeDtypeStruct((e, rows, 1), I32),
                   jax.ShapeDtypeStruct((e, rows, 1), I32)],
        scratch_shapes=[pltpu.VMEM((rows, LANE), I32)],
        compiler_params=_cparams("parallel"),
        name="select",
    )(aff3)


def _row_copy(src_hbm, src_row, dst_vmem, dst_row, sem):
    return pltpu.make_async_copy(src_hbm.at[pl.ds(src_row, 1)], dst_vmem.at[pl.ds(dst_row, 1)], sem)


def _ffn_kernel(idx_ref, nxt_ref, hn_hbm, wg_ref, wu_ref, wd_ref, ye_ref, xbuf, sem, *, slots, steps):
    n = pl.program_id(0) * pl.num_programs(1) + pl.program_id(1)
    cur = n % 2

    def gather(iref, buf_slot):
        for r in range(slots):
            _row_copy(hn_hbm, iref[0, 0, r], xbuf.at[buf_slot], r, sem.at[buf_slot]).start()

    def wait_all(buf_slot):
        pltpu.make_async_copy(hn_hbm.at[pl.ds(0, slots)], xbuf.at[buf_slot], sem.at[buf_slot]).wait()

    @pl.when(n == 0)
    def _():
        gather(idx_ref, 0)

    gather(nxt_ref, 1 - cur)
    wait_all(cur)
    x = xbuf[cur].astype(BF16)
    gate = _dot(x, wg_ref[0])
    up = _dot(x, wu_ref[0])
    hid = (_silu(gate) * up).astype(BF16)
    ye_ref[...] = _dot(hid, wd_ref[0]).astype(BF16)

    @pl.when(n == steps - 1)
    def _():
        wait_all(1 - cur)


def _ffn(idx3, hn, wg, wu, wd, cap):
    slots = idx3.shape[-1]
    nb = cap // slots
    steps = N_EXPERTS * nb
    wspec = lambda: pl.BlockSpec((1, D_MODEL, EXPERT_FF), lambda e, b: (e, 0, 0))
    ispec = lambda f: pl.BlockSpec((1, 1, slots), f, memory_space=pltpu.SMEM)
    return pl.pallas_call(
        functools.partial(_ffn_kernel, slots=slots, steps=steps),
        grid=(N_EXPERTS, nb),
        in_specs=[ispec(lambda e, b: (e * nb + b, 0, 0)),
                  ispec(lambda e, b: (jnp.minimum(e * nb + b + 1, steps - 1), 0, 0)),
                  pl.BlockSpec(memory_space=pl.ANY),
                  wspec(), wspec(),
                  pl.BlockSpec((1, EXPERT_FF, D_MODEL), lambda e, b: (e, 0, 0))],
        out_specs=pl.BlockSpec((slots, D_MODEL), lambda e, b: (e * nb + b, 0)),
        out_shape=jax.ShapeDtypeStruct((N_EXPERTS * cap, D_MODEL), BF16),
        scratch_shapes=[pltpu.VMEM((2, slots, D_MODEL), F32), pltpu.SemaphoreType.DMA((2,))],
        compiler_params=_cparams("arbitrary", "arbitrary"),
        name="expert_ffn",
    )(idx3, idx3, hn, wg, wu, wd)


YE_SHIFT = 4
YE_BLOCK = 1 << YE_SHIFT
COMBINE_MAX_BLOCKS = N_EXPERTS * (COMBINE_TILE // YE_BLOCK + 1)
COMBINE_CHUNK = 512
COMBINE_MAX_ROWS = -(-COMBINE_MAX_BLOCKS * YE_BLOCK // COMBINE_CHUNK) * COMBINE_CHUNK


def _combine_kernel(toff_ref, tcnt_ref, x_ref, pos_ref, aff_ref, ye_hbm, out_ref, buf, sem, *, cap, rows, chunk):
    r = pl.program_id(0)
    cur = r % 2

    def layout(tile):
        firsts, nblks, dsts = [], [], []
        dst = jnp.int32(0)
        for e in range(N_EXPERTS):
            off = toff_ref[e * rows + tile]
            cnt = tcnt_ref[e * rows + tile]
            first = (off >> YE_SHIFT) << YE_SHIFT
            firsts.append(first)
            nblks.append(jnp.where(cnt > 0, (off + cnt - first + YE_BLOCK - 1) >> YE_SHIFT, 0))
            dsts.append(dst)
            dst = dst + nblks[-1]
        return firsts, nblks, dsts, dst

    def gather(tile, buf_slot):
        firsts, nblks, dsts, _ = layout(tile)
        for e in range(N_EXPERTS):
            def issue(j, carry, e=e):
                src = pl.multiple_of(e * cap + firsts[e] + j * YE_BLOCK, YE_BLOCK)
                dst = pl.multiple_of((dsts[e] + j) * YE_BLOCK, YE_BLOCK)
                pltpu.make_async_copy(ye_hbm.at[pl.ds(src, YE_BLOCK)], buf.at[buf_slot, pl.ds(dst, YE_BLOCK)],
                                      sem.at[buf_slot]).start()
                return carry

            lax.fori_loop(0, nblks[e], issue, 0)

    @pl.when(r == 0)
    def _():
        gather(0, 0)

    @pl.when(r + 1 < rows)
    def _():
        gather(r + 1, 1 - cur)

    firsts, _, dsts, nblk_total = layout(r)
    total = nblk_total * YE_BLOCK
    for bit in range(COMBINE_MAX_BLOCKS.bit_length()):
        @pl.when(((nblk_total >> bit) & 1) == 1)
        def _(bit=bit):
            n = YE_BLOCK << bit
            pltpu.make_async_copy(ye_hbm.at[pl.ds(0, n)], buf.at[cur, pl.ds(0, n)], sem.at[cur]).wait()

    out_ref[...] = x_ref[...]
    pos = pos_ref[...]
    aff = aff_ref[...]
    tgt = [jnp.where(pos[e:e + 1] >= 0, pos[e:e + 1] - firsts[e] + dsts[e] * YE_BLOCK, -1)
           for e in range(N_EXPERTS)]

    def chunk_body(k, carry):
        base = pl.multiple_of(k * chunk, chunk)
        cidx = base + lax.broadcasted_iota(I32, (chunk, COMBINE_TILE), 0)
        wt = jnp.zeros((chunk, COMBINE_TILE), F32)
        for e in range(N_EXPERTS):
            wt = wt + jnp.where(cidx == tgt[e], aff[e:e + 1], 0.0)
        rowid = base + lax.broadcasted_iota(I32, (chunk, 1), 0)
        rowsv = jnp.where(rowid < total, buf[cur, pl.ds(base, chunk), :], jnp.zeros((), BF16))
        wh, wl = _split_bf16(wt, 2)
        out_ref[...] += _dot_tn(wh, rowsv) + _dot_tn(wl, rowsv)
        return carry

    lax.fori_loop(0, (total + chunk - 1) // chunk, chunk_body, 0)


def _combine(toff, tcnt, x2, pos, aff_t, ye, cap):
    t = x2.shape[0]
    tile = COMBINE_TILE
    rows = t // tile
    chunk = COMBINE_CHUNK
    grid_spec = pltpu.PrefetchScalarGridSpec(
        num_scalar_prefetch=2,
        grid=(rows,),
        in_specs=[pl.BlockSpec((tile, D_MODEL), lambda i, *_: (i, 0)),
                  pl.BlockSpec((N_EXPERTS, tile), lambda i, *_: (0, i)),
                  pl.BlockSpec((N_EXPERTS, tile), lambda i, *_: (0, i)),
                  pl.BlockSpec(memory_space=pl.ANY)],
        out_specs=pl.BlockSpec((tile, D_MODEL), lambda i, *_: (i, 0)),
        scratch_shapes=[pltpu.VMEM((2, COMBINE_MAX_ROWS, D_MODEL), BF16), pltpu.SemaphoreType.DMA((2,))],
    )
    return pl.pallas_call(
        functools.partial(_combine_kernel, cap=cap, rows=rows, chunk=chunk),
        grid_spec=grid_spec,
        out_shape=jax.ShapeDtypeStruct((t, D_MODEL), F32),
        compiler_params=_cparams("arbitrary"),
        name="combine",
    )(toff, tcnt, x2, pos, aff_t, ye)


def _rope_tables(seq):
    inv = ROPE_THETA ** (-jnp.arange(0, MLA_ROPE, 2, dtype=F32) / MLA_ROPE)
    ang = jnp.arange(seq, dtype=F32)[:, None] * inv[None, :]
    cos, sin = jnp.cos(ang), jnp.sin(ang)
    one = jnp.ones((seq, 1), F32)
    zero = jnp.zeros((seq, 1), F32)
    rep = lambda v, n: jnp.broadcast_to(v, (seq, n))
    half = MLA_ROPE // 2
    mla = (jnp.concatenate([rep(one, MLA_NOPE), cos, cos, rep(one, LANE - MLA_QK)], axis=1),
           jnp.concatenate([rep(zero, MLA_NOPE + half), sin, rep(zero, LANE - MLA_QK)], axis=1),
           jnp.concatenate([rep(zero, MLA_NOPE), -sin, rep(zero, LANE - MLA_NOPE - half)], axis=1))
    nblk = LANE // DIFF_QK
    dif = (jnp.tile(jnp.concatenate([cos, cos], axis=1), (1, nblk)),
           jnp.tile(jnp.concatenate([rep(zero, half), sin], axis=1), (1, nblk)),
           jnp.tile(jnp.concatenate([-sin, rep(zero, half)], axis=1), (1, nblk)))
    return mla, dif


def _pad_cols(w, width):
    return jnp.pad(w, ((0, 0), (0, width - w.shape[1])))


def _layer_params(p, l):
    w_in = p['w_in'][l]
    z64 = jnp.zeros((D_MODEL, 64), F32)
    z32 = jnp.zeros((D_MODEL, 32), F32)
    w_proj = jnp.concatenate([
        w_in[:, :OFF_KPE], z64, w_in[:, OFF_KPE:OFF_Z], z32,
        w_in[:, OFF_Z:OFF_DT], _pad_cols(w_in[:, OFF_DT:OFF_DQ], LANE),
        w_in[:, OFF_DQ:]], axis=1).astype(BF16)
    wq = p['mla_w_uq'][l].reshape(MLA_Q_LORA, MLA_HEADS, MLA_QK)
    wq = jnp.pad(wq, ((0, 0), (0, 0), (0, LANE - MLA_QK))).reshape(MLA_Q_LORA, MLA_HEADS * LANE)
    wkv = p['mla_w_ukv'][l].reshape(MLA_KV_LORA, MLA_HEADS, MLA_NOPE + MLA_V)
    wk = jnp.pad(wkv[:, :, :MLA_NOPE], ((0, 0), (0, 0), (0, LANE - MLA_NOPE))).reshape(MLA_KV_LORA, MLA_HEADS * LANE)
    wv = wkv[:, :, MLA_NOPE:].reshape(MLA_KV_LORA, MLA_WIDTH)
    wr = _pad_cols(p['w_router'][l], LANE)
    wrh = wr.astype(BF16)
    w_router2 = jnp.concatenate([wrh, (wr - wrh.astype(F32)).astype(BF16)], axis=1)
    pad_row = lambda v, n: jnp.pad(v, (0, n - v.shape[0]))[None, :]
    return dict(
        attn_norm=p['attn_norm'][l][None, :], w_proj=w_proj,
        q_norm=p['mla_q_norm'][l][None, :], kv_norm=p['mla_kv_norm'][l][None, :],
        wq=wq.astype(BF16), wk=wk.astype(BF16), wv=wv.astype(BF16),
        q_gain=pad_row(p['mla_q_gain'][l], LANE), k_gain=pad_row(p['mla_k_gain'][l], LANE),
        conv_w=p['ssd_conv_w'][l], conv_b=p['ssd_conv_b'][l][None, :],
        dt_bias=pad_row(p['ssd_dt_bias'][l].reshape(-1), LANE),
        a_log=pad_row(p['ssd_a_log'][l].reshape(-1), LANE),
        d_skip=jnp.repeat(p['ssd_d'][l], SSD_HEAD_DIM)[None, :], ssd_norm=p['ssd_norm'][l][None, :],
        dq_gain=jnp.tile(p['diff_q_gain'][l], 2 * DIFF_HEADS)[None, :],
        dk_gain=jnp.tile(p['diff_k_gain'][l], 2 * DIFF_HEADS)[None, :],
        lam=p['diff_lambda'][l], subln=p['diff_subln'][l][:, None],
        w_out=p['w_out'][l].astype(BF16), ffn_norm=p['ffn_norm'][l][None, :],
        w_router2=w_router2,
        w_gate=p['w_gate'][l].astype(BF16), w_up=p['w_up'][l].astype(BF16), w_down=p['w_down'][l].astype(BF16),
    )


def _block_diag_ones(n, blk):
    i = jnp.arange(n)
    return (i[:, None] // blk == i[None, :] // blk).astype(BF16)


def _trunk(x, p):
    batch, seq, _ = x.shape
    t = batch * seq
    cap = EC_CAPACITY * t // N_EXPERTS
    tk = min(ATTN_TK, seq)
    mla_tabs, dif_tabs = _rope_tables(seq)
    bd = _block_diag_ones(2 * DIFF_HEADS * DIFF_QK, DIFF_QK)
    x2 = x.reshape(t, D_MODEL)
    rows = t // LANE
    for l in range(DEPTH):
        lp = _layer_params(p, l)
        a_in, b_in, c_in = _inproj(x2, lp['attn_norm'], lp['w_proj'])
        q, k, vt = _mla_prep(a_in, lp['q_norm'], lp['kv_norm'], lp['wq'], lp['wk'], lp['wv'],
                             lp['q_gain'], lp['k_gain'], mla_tabs, seq)
        mla_bound = (MLA_QK ** 0.5 * LOG2E) * jnp.max(jnp.abs(p['mla_q_gain'][l])) * jnp.max(jnp.abs(p['mla_k_gain'][l]))
        mla_t = _attention(q.reshape(batch, seq, -1), k.reshape(batch, seq, -1),
                           vt.reshape(batch, seq // tk, MLA_HEADS * V_ROWS, tk),
                           MLA_HEADS, lambda h: h, lambda h: h, MLA_V, tk, mla_bound, BF16)
        dq, dk, dvt = _diff_prep(c_in, lp['dq_gain'], lp['dk_gain'], bd, dif_tabs, seq)
        dif_bound = (DIFF_QK ** 0.5 * LOG2E) * jnp.max(jnp.abs(p['diff_q_gain'][l])) * jnp.max(jnp.abs(p['diff_k_gain'][l]))
        dif_t = _attention(dq.reshape(batch, seq, -1), dk.reshape(batch, seq, -1),
                           dvt.reshape(batch, seq // tk, DIFF_HEADS * V_ROWS, tk),
                           2 * DIFF_HEADS, lambda h: h // (LANE // DIFF_QK), lambda h: h // 2, DIFF_V, tk,
                           dif_bound, F32)
        yf = _ssd_pass(b_in, None, lp['conv_w'], lp['conv_b'], lp['dt_bias'], lp['a_log'],
                       None, None, batch, seq, rev=False)
        ssd = _ssd_pass(b_in, yf, lp['conv_w'], lp['conv_b'], lp['dt_bias'], lp['a_log'],
                        lp['d_skip'], lp['ssd_norm'], batch, seq, rev=True)
        lambda_init = 0.8 - 0.6 * math.exp(-0.3 * l)
        x2, hn, aff_t = _outproj(x2, mla_t, ssd, dif_t, lp['lam'], lp['subln'], lp['w_out'],
                                 lp['ffn_norm'], lp['w_router2'], seq, lambda_init)
        pos3, idx, toff, tcnt = _select(aff_t.reshape(N_EXPERTS, rows, LANE), cap)
        slots = min(FFN_SLOTS, cap)
        ye = _ffn(idx.reshape(N_EXPERTS * cap // slots, 1, slots), hn,
                  lp['w_gate'], lp['w_up'], lp['w_down'], cap)
        x2 = _combine(toff.reshape(-1), tcnt.reshape(-1), x2, pos3.reshape(N_EXPERTS, t), aff_t, ye, cap)
    return x2.reshape(batch, seq, D_MODEL)


def kernel(x_prompt, x_sample, attn_norm, w_in, mla_q_norm, mla_kv_norm, mla_w_uq, mla_w_ukv, mla_q_gain,
           mla_k_gain, ssd_conv_w, ssd_conv_b, ssd_dt_bias, ssd_a_log, ssd_d, ssd_norm, diff_q_gain,
           diff_k_gain, diff_lambda, diff_subln, w_out, ffn_norm, w_router, w_gate, w_up, w_down):
    p = dict(attn_norm=attn_norm, w_in=w_in, mla_q_norm=mla_q_norm, mla_kv_norm=mla_kv_norm,
             mla_w_uq=mla_w_uq, mla_w_ukv=mla_w_ukv, mla_q_gain=mla_q_gain, mla_k_gain=mla_k_gain,
             ssd_conv_w=ssd_conv_w, ssd_conv_b=ssd_conv_b, ssd_dt_bias=ssd_dt_bias, ssd_a_log=ssd_a_log,
             ssd_d=ssd_d, ssd_norm=ssd_norm, diff_q_gain=diff_q_gain, diff_k_gain=diff_k_gain,
             diff_lambda=diff_lambda, diff_subln=diff_subln, w_out=w_out, ffn_norm=ffn_norm,
             w_router=w_router, w_gate=w_gate, w_up=w_up, w_down=w_down)
    return _trunk(x_prompt, p), _trunk(x_sample, p)
```

```python
import functools
import math

import jax
import jax.numpy as jnp
from jax import lax
from jax.experimental import pallas as pl
from jax.experimental.pallas import tpu as pltpu

F32 = jnp.float32
BF16 = jnp.bfloat16
I32 = jnp.int32

D_MODEL = 1024
DEPTH = 2
EPS = 1e-6
ROPE_THETA = 10000.0

MLA_HEADS = 6
MLA_Q_LORA = 256
MLA_KV_LORA = 128
MLA_NOPE = 64
MLA_ROPE = 32
MLA_QK = MLA_NOPE + MLA_ROPE
MLA_V = 64
MLA_WIDTH = MLA_HEADS * MLA_V

SSD_HEADS = 6
SSD_HEAD_DIM = 64
SSD_INNER = SSD_HEADS * SSD_HEAD_DIM
SSD_GROUPS = 2
SSD_STATE = 128
SSD_CONV = 5
SSD_CONV_DIM = SSD_INNER + 2 * SSD_GROUPS * SSD_STATE

DIFF_HEADS = 4
DIFF_QK = 32
DIFF_V = 2 * DIFF_QK
DIFF_WIDTH = DIFF_HEADS * DIFF_V

N_EXPERTS = 16
EC_CAPACITY = 2
EXPERT_FF = 1024

OFF_KV = MLA_Q_LORA
OFF_KPE = OFF_KV + MLA_KV_LORA
OFF_Z = OFF_KPE + MLA_ROPE
OFF_XBC = OFF_Z + SSD_INNER
OFF_DT = OFF_XBC + SSD_CONV_DIM
OFF_DQ = OFF_DT + 2 * SSD_HEADS
OFF_DK = OFF_DQ + DIFF_HEADS * 2 * DIFF_QK
OFF_DV = OFF_DK + DIFF_HEADS * 2 * DIFF_QK
IN_COLS = OFF_DV + DIFF_WIDTH

LANE = 128
SUBLANE = 8

MLA_IN = 512
SSD_IN = SSD_INNER + SSD_CONV_DIM + LANE
DIFF_IN = 768
PROJ_COLS = MLA_IN + SSD_IN + DIFF_IN

TOKEN_TILE = 512
ATTN_TQ = 1024
ATTN_TK = 512
ATTN_GROUP = 16
SSD_T = 256
COMBINE_TILE = 128
V_ROWS = 80
LOG2E = math.log2(math.e)
NOSHIFT_MAX_LOG2 = 60.0
FFN_SLOTS = 512
IDX_SLOTS = 256
VMEM_LIMIT = 56 * 1024 * 1024


def _cparams(*sem):
    return pltpu.CompilerParams(dimension_semantics=sem, vmem_limit_bytes=VMEM_LIMIT)


def _split_bf16(a, terms):
    parts = []
    rem = a
    for _ in range(terms):
        p = rem.astype(BF16)
        parts.append(p)
        rem = rem - p.astype(F32)
    return parts


def _dot(a, b):
    return jnp.dot(a, b, preferred_element_type=F32)


def _dot_nt(a, b):
    return lax.dot_general(a, b, (((1,), (1,)), ((), ())), preferred_element_type=F32)


def _dot_tn(a, b):
    return lax.dot_general(a, b, (((0,), (0,)), ((), ())), preferred_element_type=F32)


def _dot_split_lhs(a_f32, b_bf16, terms):
    out = None
    for p in _split_bf16(a_f32, terms):
        d = _dot(p, b_bf16)
        out = d if out is None else out + d
    return out


def _dot_split_rhs(a_bf16, b_f32, terms):
    out = None
    for p in _split_bf16(b_f32, terms):
        d = _dot(a_bf16, p)
        out = d if out is None else out + d
    return out


def _silu(x):
    return x * jax.nn.sigmoid(x)


def _inproj_kernel(x_ref, g_ref, w_ref, a_ref, b_ref, c_ref):
    x = x_ref[...]
    ms = jnp.mean(x * x, axis=-1, keepdims=True)
    h = (x * lax.rsqrt(ms + EPS) * g_ref[...]).astype(BF16)
    y = _dot(h, w_ref[...])
    a_ref[...] = y[:, :MLA_IN]
    b_ref[...] = y[:, MLA_IN:MLA_IN + SSD_IN]
    c_ref[...] = y[:, MLA_IN + SSD_IN:]


def _inproj(x2, gain, w):
    t = x2.shape[0]
    tm = TOKEN_TILE
    return pl.pallas_call(
        _inproj_kernel,
        grid=(t // tm,),
        in_specs=[pl.BlockSpec((tm, D_MODEL), lambda i: (i, 0)),
                  pl.BlockSpec((1, D_MODEL), lambda i: (0, 0)),
                  pl.BlockSpec((D_MODEL, PROJ_COLS), lambda i: (0, 0))],
        out_specs=[pl.BlockSpec((tm, MLA_IN), lambda i: (i, 0)),
                   pl.BlockSpec((tm, SSD_IN), lambda i: (i, 0)),
                   pl.BlockSpec((tm, DIFF_IN), lambda i: (i, 0))],
        out_shape=[jax.ShapeDtypeStruct((t, MLA_IN), F32),
                   jax.ShapeDtypeStruct((t, SSD_IN), F32),
                   jax.ShapeDtypeStruct((t, DIFF_IN), F32)],
        compiler_params=_cparams("parallel"),
        name="inproj",
    )(x2, gain, w)


def _rope(x, c, sa, sb, shift):
    return x * c + pltpu.roll(x, shift, 1) * sa + pltpu.roll(x, LANE - shift, 1) * sb


def _mla_prep_kernel(a_ref, qn_ref, kvn_ref, wq_ref, wk_ref, wv_ref, qg_ref, kg_ref,
                     c_ref, sa_ref, sb_ref, q_ref, k_ref, vt_ref):
    a = a_ref[...]
    cq = a[:, :MLA_Q_LORA]
    ckv = a[:, MLA_Q_LORA:MLA_Q_LORA + MLA_KV_LORA]
    kpe = a[:, MLA_Q_LORA + MLA_KV_LORA:]
    cqn = (cq * lax.rsqrt(jnp.mean(cq * cq, axis=-1, keepdims=True) + EPS) * qn_ref[...]).astype(BF16)
    ckvn = (ckv * lax.rsqrt(jnp.mean(ckv * ckv, axis=-1, keepdims=True) + EPS) * kvn_ref[...]).astype(BF16)
    q = _dot(cqn, wq_ref[...])
    kn = _dot(ckvn, wk_ref[...])
    v = _dot(ckvn, wv_ref[...])
    c = c_ref[...]
    sa = sa_ref[...]
    sb = sb_ref[...]
    scale = MLA_QK ** -0.5 * LOG2E
    for h in range(MLA_HEADS):
        sl = slice(h * LANE, (h + 1) * LANE)
        qh = q[:, sl]
        qh = qh * lax.rsqrt(jnp.sum(qh * qh, axis=-1, keepdims=True) / MLA_QK + EPS) * qg_ref[...]
        q_ref[:, sl] = (_rope(qh, c, sa, sb, MLA_ROPE // 2) * scale).astype(BF16)
        kh = kn[:, sl] + kpe
        kh = kh * lax.rsqrt(jnp.sum(kh * kh, axis=-1, keepdims=True) / MLA_QK + EPS) * kg_ref[...]
        k_ref[:, sl] = _rope(kh, c, sa, sb, MLA_ROPE // 2).astype(BF16)
    vt = v.T.astype(BF16)
    ones = jnp.ones((V_ROWS - MLA_V, vt.shape[1]), BF16)
    for h in range(MLA_HEADS):
        vt_ref[0, h * V_ROWS:(h + 1) * V_ROWS, :] = jnp.concatenate([vt[h * MLA_V:(h + 1) * MLA_V], ones], axis=0)


def _mla_prep(a, qn, kvn, wq, wk, wv, qg, kg, tabs, seq):
    t = a.shape[0]
    tm = TOKEN_TILE
    nt = seq // tm
    full = lambda r, c: pl.BlockSpec((r, c), lambda i: (0, 0))
    tab = pl.BlockSpec((tm, LANE), lambda i: (i % nt, 0))
    return pl.pallas_call(
        _mla_prep_kernel,
        grid=(t // tm,),
        in_specs=[pl.BlockSpec((tm, MLA_IN), lambda i: (i, 0)),
                  full(1, MLA_Q_LORA), full(1, MLA_KV_LORA),
                  full(MLA_Q_LORA, MLA_HEADS * LANE), full(MLA_KV_LORA, MLA_HEADS * LANE),
                  full(MLA_KV_LORA, MLA_WIDTH), full(1, LANE), full(1, LANE), tab, tab, tab],
        out_specs=[pl.BlockSpec((tm, MLA_HEADS * LANE), lambda i: (i, 0)),
                   pl.BlockSpec((tm, MLA_HEADS * LANE), lambda i: (i, 0)),
                   pl.BlockSpec((1, MLA_HEADS * V_ROWS, tm), lambda i: (i, 0, 0))],
        out_shape=[jax.ShapeDtypeStruct((t, MLA_HEADS * LANE), BF16),
                   jax.ShapeDtypeStruct((t, MLA_HEADS * LANE), BF16),
                   jax.ShapeDtypeStruct((t // tm, MLA_HEADS * V_ROWS, tm), BF16)],
        compiler_params=_cparams("parallel"),
        name="mla_prep",
    )(a, qn, kvn, wq, wk, wv, qg, kg, *tabs)


def _diff_prep_kernel(cin_ref, qg_ref, kg_ref, bd_ref, c_ref, sa_ref, sb_ref, q_ref, k_ref, vt_ref):
    cin = cin_ref[...]
    bd = bd_ref[...]
    c = jnp.concatenate([c_ref[...], c_ref[...]], axis=1)
    sa = jnp.concatenate([sa_ref[...], sa_ref[...]], axis=1)
    sb = jnp.concatenate([sb_ref[...], sb_ref[...]], axis=1)
    lane = lax.broadcasted_iota(I32, (1, LANE), 1)

    def norm_rope(x, g):
        ms = _dot_split_lhs(x * x, bd, 3) / DIFF_QK
        x = x * lax.rsqrt(ms + EPS) * g
        halves = []
        for j in range(2):
            sl = slice(j * LANE, (j + 1) * LANE)
            halves.append(_rope(x[:, sl], c[:, sl], sa[:, sl], sb[:, sl], DIFF_QK // 2))
        return halves

    qh = norm_rope(cin[:, :256], qg_ref[...])
    kh = norm_rope(cin[:, 256:512], kg_ref[...])
    scale = DIFF_QK ** -0.5 * LOG2E
    for j in range(2 * DIFF_HEADS):
        grp, sub = divmod(j, LANE // DIFF_QK)
        keep = (lane >= sub * DIFF_QK) & (lane < (sub + 1) * DIFF_QK)
        q_ref[:, j * LANE:(j + 1) * LANE] = jnp.where(keep, qh[grp] * scale, 0.0).astype(BF16)
    k_ref[...] = jnp.concatenate(kh, axis=1).astype(BF16)
    vt = cin[:, 512:].T.astype(BF16)
    ones = jnp.ones((V_ROWS - DIFF_V, vt.shape[1]), BF16)
    for h in range(DIFF_HEADS):
        vt_ref[0, h * V_ROWS:(h + 1) * V_ROWS, :] = jnp.concatenate([vt[h * DIFF_V:(h + 1) * DIFF_V], ones], axis=0)


def _diff_prep(cin, qg, kg, bd, tabs, seq):
    t = cin.shape[0]
    tm = TOKEN_TILE
    nt = seq // tm
    full = lambda r, c: pl.BlockSpec((r, c), lambda i: (0, 0))
    tab = pl.BlockSpec((tm, LANE), lambda i: (i % nt, 0))
    nq = 2 * DIFF_HEADS * LANE
    return pl.pallas_call(
        _diff_prep_kernel,
        grid=(t // tm,),
        in_specs=[pl.BlockSpec((tm, DIFF_IN), lambda i: (i, 0)),
                  full(1, 256), full(1, 256), full(256, 256), tab, tab, tab],
        out_specs=[pl.BlockSpec((tm, nq), lambda i: (i, 0)),
                   pl.BlockSpec((tm, 256), lambda i: (i, 0)),
                   pl.BlockSpec((1, DIFF_HEADS * V_ROWS, tm), lambda i: (i, 0, 0))],
        out_shape=[jax.ShapeDtypeStruct((t, nq), BF16),
                   jax.ShapeDtypeStruct((t, 256), BF16),
                   jax.ShapeDtypeStruct((t // tm, DIFF_HEADS * V_ROWS, tm), BF16)],
        compiler_params=_cparams("parallel"),
        name="diff_prep",
    )(cin, qg, kg, bd, *tabs)


def _attn_noshift_kernel(q_ref, k_ref, vt_ref, o_ref, acc_scr, l_scr, pa_scr, pb_scr, *, nkc, tk, dv):
    q = q_ref[0]

    def probs(kc, p_ref):
        k = k_ref[0, pl.ds(pl.multiple_of(kc * tk, tk), tk), :]
        p = jnp.exp2(_dot_nt(k, q))
        p_ref[...] = p.astype(BF16)
        return jnp.sum(p, axis=0, keepdims=True)

    def pv(kc, p_ref):
        return _dot(vt_ref[0, kc, :dv, :], p_ref[...])

    bufs = (pa_scr, pb_scr)
    unroll = ATTN_GROUP

    def run(first, count, last):
        tot = None
        lsum = None
        for j in range(count):
            if not (last and j == count - 1):
                ls = probs(first + j + 1, bufs[(j + 1) % 2])
                lsum = ls if lsum is None else lsum + ls
            d = pv(first + j, bufs[j % 2])
            tot = d if tot is None else tot + d
        return tot, lsum

    l_scr[...] = probs(0, pa_scr)
    acc_scr[...] = jnp.zeros_like(acc_scr)
    trips = (nkc - 1) // unroll

    def body(i, carry):
        tot, lsum = run(i * unroll, unroll, False)
        acc_scr[...] += tot
        l_scr[...] += lsum
        return carry

    lax.fori_loop(0, trips, body, 0)
    tot, lsum = run(trips * unroll, nkc - trips * unroll, True)
    den = l_scr[...] if lsum is None else l_scr[...] + lsum
    o_ref[0] = ((acc_scr[...] + tot) / den).astype(o_ref.dtype)


def _attn_online_kernel(q_ref, k_ref, vt_ref, o_ref, m_scr, acc_scr, *, nkc, tk, dv):
    q = q_ref[0]
    m_scr[...] = jnp.full_like(m_scr, -jnp.inf)
    acc_scr[...] = jnp.zeros_like(acc_scr)

    def body(kc, carry):
        k = k_ref[0, pl.ds(pl.multiple_of(kc * tk, tk), tk), :]
        s = _dot_nt(k, q)
        m_prev = m_scr[...]
        m_new = jnp.maximum(m_prev, jnp.max(s, axis=0, keepdims=True))
        p = jnp.exp2(s - m_new).astype(BF16)
        acc_scr[...] = jnp.exp2(m_prev - m_new) * acc_scr[...] + _dot(vt_ref[0, kc], p)
        m_scr[...] = m_new
        return carry

    lax.fori_loop(0, nkc, body, 0)
    acc = acc_scr[...]
    o_ref[0] = (acc[:dv] / acc[dv:dv + 1]).astype(o_ref.dtype)


def _attention(q, k, vt, heads, kmap, vmap, dv, tk, score_bound, out_dtype):
    b, s, _ = q.shape
    tq = min(ATTN_TQ, s)
    nkc = s // tk

    def call(body, scratch, name):
        return pl.pallas_call(
            functools.partial(body, nkc=nkc, tk=tk, dv=dv),
            grid=(b, heads, s // tq),
            in_specs=[pl.BlockSpec((1, tq, LANE), lambda bi, h, qi: (bi, qi, h)),
                      pl.BlockSpec((1, s, LANE), lambda bi, h, qi: (bi, 0, kmap(h))),
                      pl.BlockSpec((1, nkc, V_ROWS, tk), lambda bi, h, qi: (bi, 0, vmap(h), 0))],
            out_specs=pl.BlockSpec((1, dv, tq), lambda bi, h, qi: (bi, h, qi)),
            out_shape=jax.ShapeDtypeStruct((b, heads * dv, s), out_dtype),
            scratch_shapes=scratch,
            compiler_params=_cparams("parallel", "parallel", "arbitrary"),
            name=name,
        )

    acc = pltpu.VMEM((V_ROWS, tq), F32)
    pbuf = pltpu.VMEM((tk, tq), BF16)
    fast = call(_attn_noshift_kernel, [pltpu.VMEM((dv, tq), F32), pltpu.VMEM((1, tq), F32), pbuf, pbuf],
                "attention")
    safe = call(_attn_online_kernel, [pltpu.VMEM((1, tq), F32), acc], "attention_online")
    return lax.cond(score_bound < NOSHIFT_MAX_LOG2, fast, safe, q, k, vt)


def _ssd_kernel(*refs, rev, nct, t):
    if rev:
        (main_ref, prev_ref, next_ref, yf_ref, cw_ref, cb_ref, dtb_ref, alog_ref, dsk_ref, nw_ref,
         out_ref, h_scr) = refs
    else:
        (main_ref, prev_ref, next_ref, cw_ref, cb_ref, dtb_ref, alog_ref, out_ref, h_scr) = refs
    i = pl.program_id(1)
    c = (nct - 1 - i) if rev else i

    @pl.when(i == 0)
    def _():
        h_scr[...] = jnp.zeros_like(h_scr)

    main = main_ref[...]
    xlo, xhi = SSD_INNER, SSD_INNER + SSD_CONV_DIM
    prev = jnp.where(c > 0, prev_ref[:, xlo:xhi], 0.0)
    nxt = jnp.where(c < nct - 1, next_ref[:, xlo:xhi], 0.0)
    xp = jnp.concatenate([prev, main[:, xlo:xhi], nxt], axis=0)
    acc = jnp.broadcast_to(cb_ref[...], (t, SSD_CONV_DIM))
    for j in range(SSD_CONV):
        sh = (SSD_CONV // 2 - j) % (t + 2 * SUBLANE)
        r = xp if sh == 0 else pltpu.roll(xp, sh, 0)
        acc = acc + r[SUBLANE:SUBLANE + t] * cw_ref[j:j + 1, :]
    xc = _silu(acc)
    xs = xc[:, :SSD_INNER]
    gn = SSD_GROUPS * SSD_STATE
    bmat = xc[:, SSD_INNER:SSD_INNER + gn]
    cmat = xc[:, SSD_INNER + gn:]

    dtr = main[:, xhi:] + dtb_ref[...]
    dt = jnp.maximum(dtr, 0.0) + jnp.log1p(jnp.exp(-jnp.abs(dtr)))
    a = dt * (-jnp.exp(alog_ref[...]))

    row = lax.broadcasted_iota(I32, (t, t), 0)
    col = lax.broadcasted_iota(I32, (t, t), 1)
    lower = row >= col
    tri_l = jnp.where(lower, 1.0, 0.0).astype(BF16)
    tri_u = jnp.where(row <= col, 1.0, 0.0).astype(BF16)
    a_t = a.T
    cs = _dot_split_rhs(tri_l, a, 3)
    cs_t = _dot_split_lhs(a_t, tri_u, 3)
    tot = cs[t - 1:t, :]
    if rev:
        ecol = cs - a
        erow = cs_t - a_t
    lane = lax.broadcasted_iota(I32, (1, LANE), 1)
    hb = SSD_HEADS if rev else 0
    hpg = SSD_HEADS // SSD_GROUPS

    g = []
    bt = []
    for grp in range(SSD_GROUPS):
        bg = bmat[:, grp * SSD_STATE:(grp + 1) * SSD_STATE]
        cg = cmat[:, grp * SSD_STATE:(grp + 1) * SSD_STATE].astype(BF16)
        g.append((_dot_nt(cg, bg.astype(BF16)), cg))
        bt.append(bg.T.astype(BF16))

    ys = []
    for pair in range(SSD_HEADS // 2):
        xpair = xs[:, pair * LANE:(pair + 1) * LANE]
        hstate = h_scr[pair]
        hbf = hstate.astype(BF16)
        y = jnp.zeros((t, LANE), F32)
        hnew = jnp.zeros((SSD_STATE, LANE), F32)
        dec = jnp.zeros((1, LANE), F32)
        for sub in range(2):
            h = pair * 2 + sub
            grp = h // hpg
            gmat, cg = g[grp]
            keep = (lane >= sub * SSD_HEAD_DIM) & (lane < (sub + 1) * SSD_HEAD_DIM)
            hl = hb + h
            dtc = dt[:, hl:hl + 1]
            xm = jnp.where(keep, xpair * dtc, 0.0)
            if rev:
                dmat = jnp.exp(jnp.where(row <= col, erow[hl:hl + 1, :] - ecol[:, hl:hl + 1], -jnp.inf))
                off_scale = jnp.exp(tot[:, hl:hl + 1] - ecol[:, hl:hl + 1])
                st_w = jnp.exp(ecol[:, hl:hl + 1])
            else:
                dmat = jnp.exp(jnp.where(lower, cs[:, hl:hl + 1] - cs_t[hl:hl + 1, :], -jnp.inf))
                off_scale = jnp.exp(cs[:, hl:hl + 1])
                st_w = jnp.exp(tot[:, hl:hl + 1] - cs[:, hl:hl + 1])
            y = y + _dot((gmat * dmat).astype(BF16), xm.astype(BF16))
            y = y + jnp.where(keep, _dot(cg, hbf) * off_scale, 0.0)
            hnew = hnew + _dot(bt[grp], (xm * st_w).astype(BF16))
            dec = dec + jnp.where(keep, jnp.exp(tot[:, hl:hl + 1]), 0.0)
        h_scr[pair] = hstate * dec + hnew
        ys.append(y)

    yall = jnp.concatenate(ys, axis=1)
    if rev:
        yall = yall + yf_ref[...] + xs * dsk_ref[...]
        gt = yall * _silu(main[:, :SSD_INNER])
        yall = gt * lax.rsqrt(jnp.mean(gt * gt, axis=-1, keepdims=True) + EPS) * nw_ref[...]
    out_ref[...] = yall.astype(out_ref.dtype)


def _ssd_pass(bin_, yf, cw, cb, dtb, alog, dsk, nw, batch, seq, rev):
    t = SSD_T
    nct = seq // t
    hb = t // SUBLANE
    nrow8 = batch * seq // SUBLANE

    def cidx(i):
        return (nct - 1 - i) if rev else i

    main = pl.BlockSpec((t, SSD_IN), lambda b, i: (b * nct + cidx(i), 0))
    prev = pl.BlockSpec((SUBLANE, SSD_IN),
                        lambda b, i: (jnp.maximum((b * nct + cidx(i)) * hb - 1, 0), 0))
    nxt = pl.BlockSpec((SUBLANE, SSD_IN),
                       lambda b, i: (jnp.minimum((b * nct + cidx(i) + 1) * hb, nrow8 - 1), 0))
    full = lambda r, c: pl.BlockSpec((r, c), lambda b, i: (0, 0))
    yspec = pl.BlockSpec((t, SSD_INNER), lambda b, i: (b * nct + cidx(i), 0))
    if rev:
        in_specs = [main, prev, nxt, yspec, full(SSD_CONV, SSD_CONV_DIM), full(1, SSD_CONV_DIM),
                    full(1, LANE), full(1, LANE), full(1, SSD_INNER), full(1, SSD_INNER)]
        args = (bin_, bin_, bin_, yf, cw, cb, dtb, alog, dsk, nw)
    else:
        in_specs = [main, prev, nxt, full(SSD_CONV, SSD_CONV_DIM), full(1, SSD_CONV_DIM),
                    full(1, LANE), full(1, LANE)]
        args = (bin_, bin_, bin_, cw, cb, dtb, alog)
    return pl.pallas_call(
        functools.partial(_ssd_kernel, rev=rev, nct=nct, t=t),
        grid=(batch, nct),
        in_specs=in_specs,
        out_specs=yspec,
        out_shape=jax.ShapeDtypeStruct((batch * seq, SSD_INNER), BF16 if rev else F32),
        scratch_shapes=[pltpu.VMEM((SSD_HEADS // 2, SSD_STATE, LANE), F32)],
        compiler_params=_cparams("parallel", "arbitrary"),
        name="ssd_bwd" if rev else "ssd_fwd",
    )(*args)


def _outproj_kernel(x_ref, mla_ref, ssd_ref, dif_ref, lam_ref, sub_ref, wo_ref, fn_ref, wrh_ref,
                    xo_ref, hn_ref, aff_ref, *, lambda_init):
    lam = lam_ref[...]
    lam_full = (jnp.exp(jnp.sum(lam[0:1] * lam[1:2], keepdims=True))
                - jnp.exp(jnp.sum(lam[2:3] * lam[3:4], keepdims=True)) + lambda_init)
    dif = dif_ref[0]
    outs = []
    for hd in range(DIFF_HEADS):
        o = dif[2 * hd * DIFF_V:(2 * hd + 1) * DIFF_V] - lam_full * dif[(2 * hd + 1) * DIFF_V:(2 * hd + 2) * DIFF_V]
        o = o * lax.rsqrt(jnp.mean(o * o, axis=0, keepdims=True) + EPS) * sub_ref[...] * (1.0 - lambda_init)
        outs.append(o)
    dt_ = jnp.concatenate(outs, axis=0).astype(BF16)
    wo = wo_ref[...]
    x = x_ref[...]
    x = x + _dot_tn(mla_ref[0].astype(BF16), wo[:MLA_WIDTH])
    x = x + _dot(ssd_ref[...].astype(BF16), wo[MLA_WIDTH:MLA_WIDTH + SSD_INNER])
    x = x + _dot_tn(dt_, wo[MLA_WIDTH + SSD_INNER:])
    xo_ref[...] = x
    hn = x * lax.rsqrt(jnp.mean(x * x, axis=-1, keepdims=True) + EPS) * fn_ref[...]
    hn_ref[...] = hn
    hh, hl = _split_bf16(hn, 2)
    both = _dot(hh, wrh_ref[...])
    logits = both[:, :LANE] + both[:, LANE:] + _dot(hl, wrh_ref[:, :LANE])
    lane = lax.broadcasted_iota(I32, logits.shape, 1)
    logits = jnp.where(lane < N_EXPERTS, logits, -jnp.inf)
    e = jnp.exp(logits - jnp.max(logits, axis=-1, keepdims=True))
    aff = e / jnp.sum(e, axis=-1, keepdims=True)
    aff_ref[...] = aff.T[:N_EXPERTS]


def _outproj(x2, mla_t, ssd, dif_t, lam, sub, wo, fn, wr2, seq, lambda_init):
    t = x2.shape[0]
    tm = TOKEN_TILE
    nt = seq // tm
    full = lambda r, c: pl.BlockSpec((r, c), lambda i: (0, 0))
    row = lambda c: pl.BlockSpec((tm, c), lambda i: (i, 0))
    return pl.pallas_call(
        functools.partial(_outproj_kernel, lambda_init=lambda_init),
        grid=(t // tm,),
        in_specs=[row(D_MODEL),
                  pl.BlockSpec((1, MLA_WIDTH, tm), lambda i: (i // nt, 0, i % nt)),
                  row(SSD_INNER),
                  pl.BlockSpec((1, 2 * DIFF_WIDTH, tm), lambda i: (i // nt, 0, i % nt)),
                  full(4, DIFF_QK), full(DIFF_V, 1), full(D_MODEL, D_MODEL), full(1, D_MODEL),
                  full(D_MODEL, 2 * LANE)],
        out_specs=[row(D_MODEL), row(D_MODEL), pl.BlockSpec((N_EXPERTS, tm), lambda i: (0, i))],
        out_shape=[jax.ShapeDtypeStruct((t, D_MODEL), F32),
                   jax.ShapeDtypeStruct((t, D_MODEL), F32),
                   jax.ShapeDtypeStruct((N_EXPERTS, t), F32)],
        compiler_params=_cparams("parallel"),
        name="outproj",
    )(x2, mla_t, ssd, dif_t, lam, sub, wo, fn, wr2)


def _cumsum_rowmajor(mask_f32, tri_u, tri_ls):
    local = _dot(mask_f32.astype(BF16), tri_u)
    rowtot = jnp.broadcast_to(local[:, LANE - 1:LANE], local.shape).astype(BF16)
    return local + _dot(tri_ls, rowtot)


def _select_kernel(aff_ref, pos_ref, idx_ref, toff_ref, tcnt_ref, cum_scr, *, cap, sb, rows):
    a = aff_ref[0]
    key = lax.bitcast_convert_type(a, I32)

    def bit_body(it, prefix):
        cand = prefix | lax.shift_left(jnp.int32(1), 30 - it)
        cnt = jnp.sum((key >= cand).astype(I32))
        return jnp.where(cnt >= cap, cand, prefix)

    thr = lax.fori_loop(0, 31, bit_body, jnp.int32(0))
    gt = key > thr
    eq = key == thr
    need = cap - jnp.sum(gt.astype(I32))
    r_i = lax.broadcasted_iota(I32, (LANE, LANE), 0)
    c_i = lax.broadcasted_iota(I32, (LANE, LANE), 1)
    tri_u = jnp.where(r_i <= c_i, 1.0, 0.0).astype(BF16)
    r_r = lax.broadcasted_iota(I32, (rows, rows), 0)
    c_r = lax.broadcasted_iota(I32, (rows, rows), 1)
    tri_ls = jnp.where(r_r > c_r, 1.0, 0.0).astype(BF16)
    eqf = jnp.where(eq, 1.0, 0.0)
    tie_rank = _cumsum_rowmajor(eqf, tri_u, tri_ls) - eqf
    sel = gt | (eq & (tie_rank < need.astype(F32)))
    self_ = jnp.where(sel, 1.0, 0.0)
    cum = _cumsum_rowmajor(self_, tri_u, tri_ls).astype(I32)
    pos_ref[0] = jnp.where(sel, cum - 1, -1)
    cum_scr[...] = cum
    rowend = cum[:, LANE - 1:LANE]
    rowstart = rowend - jnp.sum(self_, axis=-1, keepdims=True).astype(I32)
    toff_ref[0] = rowstart
    tcnt_ref[0] = rowend - rowstart

    def blk(bi, carry):
        p0 = bi * sb
        r_lo = jnp.sum((rowend <= p0).astype(I32))
        r_hi = jnp.sum((rowstart < p0 + sb).astype(I32))
        s_col = p0 + lax.broadcasted_iota(I32, (sb, LANE), 0)

        def rowbody(r, acc):
            return acc + (cum_scr[pl.ds(r, 1), :] <= s_col).astype(I32)

        acc = lax.fori_loop(r_lo, r_hi, rowbody, jnp.zeros((sb, LANE), I32))
        idx_ref[0, pl.ds(pl.multiple_of(p0, sb), sb), :] = jnp.sum(acc, axis=1, keepdims=True) + r_lo * LANE
        return carry

    lax.fori_loop(0, cap // sb, blk, 0)


def _select(aff3, cap):
    e, rows, _ = aff3.shape
    sb = min(IDX_SLOTS, cap)
    return pl.pallas_call(
        functools.partial(_select_kernel, cap=cap, sb=sb, rows=rows),
        grid=(e,),
        in_specs=[pl.BlockSpec((1, rows, LANE), lambda i: (i, 0, 0))],
        out_specs=[pl.BlockSpec((1, rows, LANE), lambda i: (i, 0, 0)),
                   pl.BlockSpec((1, cap, 1), lambda i: (i, 0, 0)),
                   pl.BlockSpec((1, rows, 1), lambda i: (i, 0, 0)),
                   pl.BlockSpec((1, rows, 1), lambda i: (i, 0, 0))],
        out_shape=[jax.ShapeDtypeStruct((e, rows, LANE), I32),
                   jax.ShapeDtypeStruct((e, cap, 1), I32),
                   jax.ShapeDtypeStruct((e, rows, 1), I32),
                   jax.ShapeDtypeStruct((e, rows, 1), I32)],
        scratch_shapes=[pltpu.VMEM((rows, LANE), I32)],
        compiler_params=_cparams("parallel"),
        name="select",
    )(aff3)


def _row_copy(src_hbm, src_row, dst_vmem, dst_row, sem):
    return pltpu.make_async_copy(src_hbm.at[pl.ds(src_row, 1)], dst_vmem.at[pl.ds(dst_row, 1)], sem)


def _ffn_kernel(idx_ref, nxt_ref, hn_hbm, wg_ref, wu_ref, wd_ref, ye_ref, xbuf, sem, *, slots, steps):
    n = pl.program_id(0) * pl.num_programs(1) + pl.program_id(1)
    cur = n % 2

    def gather(iref, buf_slot):
        for r in range(slots):
            _row_copy(hn_hbm, iref[0, 0, r], xbuf.at[buf_slot], r, sem.at[buf_slot]).start()

    def wait_all(buf_slot):
        pltpu.make_async_copy(hn_hbm.at[pl.ds(0, slots)], xbuf.at[buf_slot], sem.at[buf_slot]).wait()

    @pl.when(n == 0)
    def _():
        gather(idx_ref, 0)

    gather(nxt_ref, 1 - cur)
    wait_all(cur)
    x = xbuf[cur].astype(BF16)
    gate = _dot(x, wg_ref[0])
    up = _dot(x, wu_ref[0])
    hid = (_silu(gate) * up).astype(BF16)
    ye_ref[...] = _dot(hid, wd_ref[0]).astype(BF16)

    @pl.when(n == steps - 1)
    def _():
        wait_all(1 - cur)


def _ffn(idx3, hn, wg, wu, wd, cap):
    slots = idx3.shape[-1]
    nb = cap // slots
    steps = N_EXPERTS * nb
    wspec = lambda: pl.BlockSpec((1, D_MODEL, EXPERT_FF), lambda e, b: (e, 0, 0))
    ispec = lambda f: pl.BlockSpec((1, 1, slots), f, memory_space=pltpu.SMEM)
    return pl.pallas_call(
        functools.partial(_ffn_kernel, slots=slots, steps=steps),
        grid=(N_EXPERTS, nb),
        in_specs=[ispec(lambda e, b: (e * nb + b, 0, 0)),
                  ispec(lambda e, b: (jnp.minimum(e * nb + b + 1, steps - 1), 0, 0)),
                  pl.BlockSpec(memory_space=pl.ANY),
                  wspec(), wspec(),
                  pl.BlockSpec((1, EXPERT_FF, D_MODEL), lambda e, b: (e, 0, 0))],
        out_specs=pl.BlockSpec((slots, D_MODEL), lambda e, b: (e * nb + b, 0)),
        out_shape=jax.ShapeDtypeStruct((N_EXPERTS * cap, D_MODEL), BF16),
        scratch_shapes=[pltpu.VMEM((2, slots, D_MODEL), F32), pltpu.SemaphoreType.DMA((2,))],
        compiler_params=_cparams("arbitrary", "arbitrary"),
        name="expert_ffn",
    )(idx3, idx3, hn, wg, wu, wd)


YE_SHIFT = 4
YE_BLOCK = 1 << YE_SHIFT
COMBINE_MAX_BLOCKS = N_EXPERTS * (COMBINE_TILE // YE_BLOCK + 1)
COMBINE_CHUNK = 512
COMBINE_MAX_ROWS = -(-COMBINE_MAX_BLOCKS * YE_BLOCK // COMBINE_CHUNK) * COMBINE_CHUNK


def _combine_kernel(toff_ref, tcnt_ref, x_ref, pos_ref, aff_ref, ye_hbm, out_ref, buf, sem, *, cap, rows, chunk):
    r = pl.program_id(0)
    cur = r % 2

    def layout(tile):
        firsts, nblks, dsts = [], [], []
        dst = jnp.int32(0)
        for e in range(N_EXPERTS):
            off = toff_ref[e * rows + tile]
            cnt = tcnt_ref[e * rows + tile]
            first = (off >> YE_SHIFT) << YE_SHIFT
            firsts.append(first)
            nblks.append(jnp.where(cnt > 0, (off + cnt - first + YE_BLOCK - 1) >> YE_SHIFT, 0))
            dsts.append(dst)
            dst = dst + nblks[-1]
        return firsts, nblks, dsts, dst

    def gather(tile, buf_slot):
        firsts, nblks, dsts, _ = layout(tile)
        for e in range(N_EXPERTS):
            def issue(j, carry, e=e):
                src = pl.multiple_of(e * cap + firsts[e] + j * YE_BLOCK, YE_BLOCK)
                dst = pl.multiple_of((dsts[e] + j) * YE_BLOCK, YE_BLOCK)
                pltpu.make_async_copy(ye_hbm.at[pl.ds(src, YE_BLOCK)], buf.at[buf_slot, pl.ds(dst, YE_BLOCK)],
                                      sem.at[buf_slot]).start()
                return carry

            lax.fori_loop(0, nblks[e], issue, 0)

    @pl.when(r == 0)
    def _():
        gather(0, 0)

    @pl.when(r + 1 < rows)
    def _():
        gather(r + 1, 1 - cur)

    firsts, _, dsts, nblk_total = layout(r)
    total = nblk_total * YE_BLOCK
    for bit in range(COMBINE_MAX_BLOCKS.bit_length()):
        @pl.when(((nblk_total >> bit) & 1) == 1)
        def _(bit=bit):
            n = YE_BLOCK << bit
            pltpu.make_async_copy(ye_hbm.at[pl.ds(0, n)], buf.at[cur, pl.ds(0, n)], sem.at[cur]).wait()

    out_ref[...] = x_ref[...]
    pos = pos_ref[...]
    aff = aff_ref[...]
    tgt = [jnp.where(pos[e:e + 1] >= 0, pos[e:e + 1] - firsts[e] + dsts[e] * YE_BLOCK, -1)
           for e in range(N_EXPERTS)]

    def chunk_body(k, carry):
        base = pl.multiple_of(k * chunk, chunk)
        cidx = base + lax.broadcasted_iota(I32, (chunk, COMBINE_TILE), 0)
        wt = jnp.zeros((chunk, COMBINE_TILE), F32)
        for e in range(N_EXPERTS):
            wt = wt + jnp.where(cidx == tgt[e], aff[e:e + 1], 0.0)
        rowid = base + lax.broadcasted_iota(I32, (chunk, 1), 0)
        rowsv = jnp.where(rowid < total, buf[cur, pl.ds(base, chunk), :], jnp.zeros((), BF16))
        wh, wl = _split_bf16(wt, 2)
        out_ref[...] += _dot_tn(wh, rowsv) + _dot_tn(wl, rowsv)
        return carry

    lax.fori_loop(0, (total + chunk - 1) // chunk, chunk_body, 0)


def _combine(toff, tcnt, x2, pos, aff_t, ye, cap):
    t = x2.shape[0]
    tile = COMBINE_TILE
    rows = t // tile
    chunk = COMBINE_CHUNK
    grid_spec = pltpu.PrefetchScalarGridSpec(
        num_scalar_prefetch=2,
        grid=(rows,),
        in_specs=[pl.BlockSpec((tile, D_MODEL), lambda i, *_: (i, 0)),
                  pl.BlockSpec((N_EXPERTS, tile), lambda i, *_: (0, i)),
                  pl.BlockSpec((N_EXPERTS, tile), lambda i, *_: (0, i)),
                  pl.BlockSpec(memory_space=pl.ANY)],
        out_specs=pl.BlockSpec((tile, D_MODEL), lambda i, *_: (i, 0)),
        scratch_shapes=[pltpu.VMEM((2, COMBINE_MAX_ROWS, D_MODEL), BF16), pltpu.SemaphoreType.DMA((2,))],
    )
    return pl.pallas_call(
        functools.partial(_combine_kernel, cap=cap, rows=rows, chunk=chunk),
        grid_spec=grid_spec,
        out_shape=jax.ShapeDtypeStruct((t, D_MODEL), F32),
        compiler_params=_cparams("arbitrary"),
        name="combine",
    )(toff, tcnt, x2, pos, aff_t, ye)


def _rope_tables(seq):
    inv = ROPE_THETA ** (-jnp.arange(0, MLA_ROPE, 2, dtype=F32) / MLA_ROPE)
    ang = jnp.arange(seq, dtype=F32)[:, None] * inv[None, :]
    cos, sin = jnp.cos(ang), jnp.sin(ang)
    one = jnp.ones((seq, 1), F32)
    zero = jnp.zeros((seq, 1), F32)
    rep = lambda v, n: jnp.broadcast_to(v, (seq, n))
    half = MLA_ROPE // 2
    mla = (jnp.concatenate([rep(one, MLA_NOPE), cos, cos, rep(one, LANE - MLA_QK)], axis=1),
           jnp.concatenate([rep(zero, MLA_NOPE + half), sin, rep(zero, LANE - MLA_QK)], axis=1),
           jnp.concatenate([rep(zero, MLA_NOPE), -sin, rep(zero, LANE - MLA_NOPE - half)], axis=1))
    nblk = LANE // DIFF_QK
    dif = (jnp.tile(jnp.concatenate([cos, cos], axis=1), (1, nblk)),
           jnp.tile(jnp.concatenate([rep(zero, half), sin], axis=1), (1, nblk)),
           jnp.tile(jnp.concatenate([-sin, rep(zero, half)], axis=1), (1, nblk)))
    return mla, dif


def _pad_cols(w, width):
    return jnp.pad(w, ((0, 0), (0, width - w.shape[1])))


def _layer_params(p, l):
    w_in = p['w_in'][l]
    z64 = jnp.zeros((D_MODEL, 64), F32)
    z32 = jnp.zeros((D_MODEL, 32), F32)
    w_proj = jnp.concatenate([
        w_in[:, :OFF_KPE], z64, w_in[:, OFF_KPE:OFF_Z], z32,
        w_in[:, OFF_Z:OFF_DT], _pad_cols(w_in[:, OFF_DT:OFF_DQ], LANE),
        w_in[:, OFF_DQ:]], axis=1).astype(BF16)
    wq = p['mla_w_uq'][l].reshape(MLA_Q_LORA, MLA_HEADS, MLA_QK)
    wq = jnp.pad(wq, ((0, 0), (0, 0), (0, LANE - MLA_QK))).reshape(MLA_Q_LORA, MLA_HEADS * LANE)
    wkv = p['mla_w_ukv'][l].reshape(MLA_KV_LORA, MLA_HEADS, MLA_NOPE + MLA_V)
    wk = jnp.pad(wkv[:, :, :MLA_NOPE], ((0, 0), (0, 0), (0, LANE - MLA_NOPE))).reshape(MLA_KV_LORA, MLA_HEADS * LANE)
    wv = wkv[:, :, MLA_NOPE:].reshape(MLA_KV_LORA, MLA_WIDTH)
    wr = _pad_cols(p['w_router'][l], LANE)
    wrh = wr.astype(BF16)
    w_router2 = jnp.concatenate([wrh, (wr - wrh.astype(F32)).astype(BF16)], axis=1)
    pad_row = lambda v, n: jnp.pad(v, (0, n - v.shape[0]))[None, :]
    return dict(
        attn_norm=p['attn_norm'][l][None, :], w_proj=w_proj,
        q_norm=p['mla_q_norm'][l][None, :], kv_norm=p['mla_kv_norm'][l][None, :],
        wq=wq.astype(BF16), wk=wk.astype(BF16), wv=wv.astype(BF16),
        q_gain=pad_row(p['mla_q_gain'][l], LANE), k_gain=pad_row(p['mla_k_gain'][l], LANE),
        conv_w=p['ssd_conv_w'][l], conv_b=p['ssd_conv_b'][l][None, :],
        dt_bias=pad_row(p['ssd_dt_bias'][l].reshape(-1), LANE),
        a_log=pad_row(p['ssd_a_log'][l].reshape(-1), LANE),
        d_skip=jnp.repeat(p['ssd_d'][l], SSD_HEAD_DIM)[None, :], ssd_norm=p['ssd_norm'][l][None, :],
        dq_gain=jnp.tile(p['diff_q_gain'][l], 2 * DIFF_HEADS)[None, :],
        dk_gain=jnp.tile(p['diff_k_gain'][l], 2 * DIFF_HEADS)[None, :],
        lam=p['diff_lambda'][l], subln=p['diff_subln'][l][:, None],
        w_out=p['w_out'][l].astype(BF16), ffn_norm=p['ffn_norm'][l][None, :],
        w_router2=w_router2,
        w_gate=p['w_gate'][l].astype(BF16), w_up=p['w_up'][l].astype(BF16), w_down=p['w_down'][l].astype(BF16),
    )


def _block_diag_ones(n, blk):
    i = jnp.arange(n)
    return (i[:, None] // blk == i[None, :] // blk).astype(BF16)


def _trunk(x, p):
    batch, seq, _ = x.shape
    t = batch * seq
    cap = EC_CAPACITY * t // N_EXPERTS
    tk = min(ATTN_TK, seq)
    mla_tabs, dif_tabs = _rope_tables(seq)
    bd = _block_diag_ones(2 * DIFF_HEADS * DIFF_QK, DIFF_QK)
    x2 = x.reshape(t, D_MODEL)
    rows = t // LANE
    for l in range(DEPTH):
        lp = _layer_params(p, l)
        a_in, b_in, c_in = _inproj(x2, lp['attn_norm'], lp['w_proj'])
        q, k, vt = _mla_prep(a_in, lp['q_norm'], lp['kv_norm'], lp['wq'], lp['wk'], lp['wv'],
                             lp['q_gain'], lp['k_gain'], mla_tabs, seq)
        mla_bound = (MLA_QK ** 0.5 * LOG2E) * jnp.max(jnp.abs(p['mla_q_gain'][l])) * jnp.max(jnp.abs(p['mla_k_gain'][l]))
        mla_t = _attention(q.reshape(batch, seq, -1), k.reshape(batch, seq, -1),
                           vt.reshape(batch, seq // tk, MLA_HEADS * V_ROWS, tk),
                           MLA_HEADS, lambda h: h, lambda h: h, MLA_V, tk, mla_bound, BF16)
        dq, dk, dvt = _diff_prep(c_in, lp['dq_gain'], lp['dk_gain'], bd, dif_tabs, seq)
        dif_bound = (DIFF_QK ** 0.5 * LOG2E) * jnp.max(jnp.abs(p['diff_q_gain'][l])) * jnp.max(jnp.abs(p['diff_k_gain'][l]))
        dif_t = _attention(dq.reshape(batch, seq, -1), dk.reshape(batch, seq, -1),
                           dvt.reshape(batch, seq // tk, DIFF_HEADS * V_ROWS, tk),
                           2 * DIFF_HEADS, lambda h: h // (LANE // DIFF_QK), lambda h: h // 2, DIFF_V, tk,
                           dif_bound, F32)
        yf = _ssd_pass(b_in, None, lp['conv_w'], lp['conv_b'], lp['dt_bias'], lp['a_log'],
                       None, None, batch, seq, rev=False)
        ssd = _ssd_pass(b_in, yf, lp['conv_w'], lp['conv_b'], lp['dt_bias'], lp['a_log'],
                        lp['d_skip'], lp['ssd_norm'], batch, seq, rev=True)
        lambda_init = 0.8 - 0.6 * math.exp(-0.3 * l)
        x2, hn, aff_t = _outproj(x2, mla_t, ssd, dif_t, lp['lam'], lp['subln'], lp['w_out'],
                                 lp['ffn_norm'], lp['w_router2'], seq, lambda_init)
        pos3, idx, toff, tcnt = _select(aff_t.reshape(N_EXPERTS, rows, LANE), cap)
        slots = min(FFN_SLOTS, cap)
        ye = _ffn(idx.reshape(N_EXPERTS * cap // slots, 1, slots), hn,
                  lp['w_gate'], lp['w_up'], lp['w_down'], cap)
        x2 = _combine(toff.reshape(-1), tcnt.reshape(-1), x2, pos3.reshape(N_EXPERTS, t), aff_t, ye, cap)
    return x2.reshape(batch, seq, D_MODEL)


def kernel(x_prompt, x_sample, attn_norm, w_in, mla_q_norm, mla_kv_norm, mla_w_uq, mla_w_ukv, mla_q_gain,
           mla_k_gain, ssd_conv_w, ssd_conv_b, ssd_dt_bias, ssd_a_log, ssd_d, ssd_norm, diff_q_gain,
           diff_k_gain, diff_lambda, diff_subln, w_out, ffn_norm, w_router, w_gate, w_up, w_down):
    p = dict(attn_norm=attn_norm, w_in=w_in, mla_q_norm=mla_q_norm, mla_kv_norm=mla_kv_norm,
             mla_w_uq=mla_w_uq, mla_w_ukv=mla_w_ukv, mla_q_gain=mla_q_gain, mla_k_gain=mla_k_gain,
             ssd_conv_w=ssd_conv_w, ssd_conv_b=ssd_conv_b, ssd_dt_bias=ssd_dt_bias, ssd_a_log=ssd_a_log,
             ssd_d=ssd_d, ssd_norm=ssd_norm, diff_q_gain=diff_q_gain, diff_k_gain=diff_k_gain,
             diff_lambda=diff_lambda, diff_subln=diff_subln, w_out=w_out, ffn_norm=ffn_norm,
             w_router=w_router, w_gate=w_gate, w_up=w_up, w_down=w_down)
    return _trunk(x_prompt, p), _trunk(x_sample, p)
```

```python
import functools
import math

import jax
import jax.numpy as jnp
from jax import lax
from jax.experimental import pallas as pl
from jax.experimental.pallas import tpu as pltpu

F32 = jnp.float32
BF16 = jnp.bfloat16
I32 = jnp.int32

D_MODEL = 1024
DEPTH = 2
EPS = 1e-6
ROPE_THETA = 10000.0

MLA_HEADS = 6
MLA_Q_LORA = 256
MLA_KV_LORA = 128
MLA_NOPE = 64
MLA_ROPE = 32
MLA_QK = MLA_NOPE + MLA_ROPE
MLA_V = 64
MLA_WIDTH = MLA_HEADS * MLA_V

SSD_HEADS = 6
SSD_HEAD_DIM = 64
SSD_INNER = SSD_HEADS * SSD_HEAD_DIM
SSD_GROUPS = 2
SSD_STATE = 128
SSD_CONV = 5
SSD_CONV_DIM = SSD_INNER + 2 * SSD_GROUPS * SSD_STATE

DIFF_HEADS = 4
DIFF_QK = 32
DIFF_V = 2 * DIFF_QK
DIFF_WIDTH = DIFF_HEADS * DIFF_V

N_EXPERTS = 16
EC_CAPACITY = 2
EXPERT_FF = 1024

OFF_KV = MLA_Q_LORA
OFF_KPE = OFF_KV + MLA_KV_LORA
OFF_Z = OFF_KPE + MLA_ROPE
OFF_XBC = OFF_Z + SSD_INNER
OFF_DT = OFF_XBC + SSD_CONV_DIM
OFF_DQ = OFF_DT + 2 * SSD_HEADS
OFF_DK = OFF_DQ + DIFF_HEADS * 2 * DIFF_QK
OFF_DV = OFF_DK + DIFF_HEADS * 2 * DIFF_QK
IN_COLS = OFF_DV + DIFF_WIDTH

LANE = 128
SUBLANE = 8

MLA_IN = 512
SSD_IN = SSD_INNER + SSD_CONV_DIM + LANE
DIFF_IN = 768
PROJ_COLS = MLA_IN + SSD_IN + DIFF_IN

TOKEN_TILE = 512
ATTN_TQ = 2048
ATTN_TK = 512
ATTN_GROUP = 8
SSD_T = 256
COMBINE_TILE = 128
V_ROWS = 80
LOG2E = math.log2(math.e)
NOSHIFT_MAX_LOG2 = 60.0
FFN_SLOTS = 512
IDX_SLOTS = 256
VMEM_LIMIT = 56 * 1024 * 1024


def _cparams(*sem):
    return pltpu.CompilerParams(dimension_semantics=sem, vmem_limit_bytes=VMEM_LIMIT)


def _split_bf16(a, terms):
    parts = []
    rem = a
    for _ in range(terms):
        p = rem.astype(BF16)
        parts.append(p)
        rem = rem - p.astype(F32)
    return parts


def _dot(a, b):
    return jnp.dot(a, b, preferred_element_type=F32)


def _dot_nt(a, b):
    return lax.dot_general(a, b, (((1,), (1,)), ((), ())), preferred_element_type=F32)


def _dot_tn(a, b):
    return lax.dot_general(a, b, (((0,), (0,)), ((), ())), preferred_element_type=F32)


def _dot_split_lhs(a_f32, b_bf16, terms):
    out = None
    for p in _split_bf16(a_f32, terms):
        d = _dot(p, b_bf16)
        out = d if out is None else out + d
    return out


def _dot_split_rhs(a_bf16, b_f32, terms):
    out = None
    for p in _split_bf16(b_f32, terms):
        d = _dot(a_bf16, p)
        out = d if out is None else out + d
    return out


def _silu(x):
    return x * jax.nn.sigmoid(x)


def _inproj_kernel(x_ref, g_ref, w_ref, a_ref, b_ref, c_ref):
    x = x_ref[...]
    ms = jnp.mean(x * x, axis=-1, keepdims=True)
    h = (x * lax.rsqrt(ms + EPS) * g_ref[...]).astype(BF16)
    y = _dot(h, w_ref[...])
    a_ref[...] = y[:, :MLA_IN]
    b_ref[...] = y[:, MLA_IN:MLA_IN + SSD_IN]
    c_ref[...] = y[:, MLA_IN + SSD_IN:]


def _inproj(x2, gain, w):
    t = x2.shape[0]
    tm = TOKEN_TILE
    return pl.pallas_call(
        _inproj_kernel,
        grid=(t // tm,),
        in_specs=[pl.BlockSpec((tm, D_MODEL), lambda i: (i, 0)),
                  pl.BlockSpec((1, D_MODEL), lambda i: (0, 0)),
                  pl.BlockSpec((D_MODEL, PROJ_COLS), lambda i: (0, 0))],
        out_specs=[pl.BlockSpec((tm, MLA_IN), lambda i: (i, 0)),
                   pl.BlockSpec((tm, SSD_IN), lambda i: (i, 0)),
                   pl.BlockSpec((tm, DIFF_IN), lambda i: (i, 0))],
        out_shape=[jax.ShapeDtypeStruct((t, MLA_IN), F32),
                   jax.ShapeDtypeStruct((t, SSD_IN), F32),
                   jax.ShapeDtypeStruct((t, DIFF_IN), F32)],
        compiler_params=_cparams("parallel"),
        name="inproj",
    )(x2, gain, w)


def _rope(x, c, sa, sb, shift):
    return x * c + pltpu.roll(x, shift, 1) * sa + pltpu.roll(x, LANE - shift, 1) * sb


def _mla_prep_kernel(a_ref, qn_ref, kvn_ref, wq_ref, wk_ref, wv_ref, qg_ref, kg_ref,
                     c_ref, sa_ref, sb_ref, q_ref, k_ref, vt_ref):
    a = a_ref[...]
    cq = a[:, :MLA_Q_LORA]
    ckv = a[:, MLA_Q_LORA:MLA_Q_LORA + MLA_KV_LORA]
    kpe = a[:, MLA_Q_LORA + MLA_KV_LORA:]
    cqn = (cq * lax.rsqrt(jnp.mean(cq * cq, axis=-1, keepdims=True) + EPS) * qn_ref[...]).astype(BF16)
    ckvn = (ckv * lax.rsqrt(jnp.mean(ckv * ckv, axis=-1, keepdims=True) + EPS) * kvn_ref[...]).astype(BF16)
    q = _dot(cqn, wq_ref[...])
    kn = _dot(ckvn, wk_ref[...])
    v = _dot(ckvn, wv_ref[...])
    c = c_ref[...]
    sa = sa_ref[...]
    sb = sb_ref[...]
    scale = MLA_QK ** -0.5 * LOG2E
    for h in range(MLA_HEADS):
        sl = slice(h * LANE, (h + 1) * LANE)
        qh = q[:, sl]
        qh = qh * lax.rsqrt(jnp.sum(qh * qh, axis=-1, keepdims=True) / MLA_QK + EPS) * qg_ref[...]
        q_ref[:, sl] = (_rope(qh, c, sa, sb, MLA_ROPE // 2) * scale).astype(BF16)
        kh = kn[:, sl] + kpe
        kh = kh * lax.rsqrt(jnp.sum(kh * kh, axis=-1, keepdims=True) / MLA_QK + EPS) * kg_ref[...]
        k_ref[:, sl] = _rope(kh, c, sa, sb, MLA_ROPE // 2).astype(BF16)
    vt = v.T.astype(BF16)
    ones = jnp.ones((V_ROWS - MLA_V, vt.shape[1]), BF16)
    for h in range(MLA_HEADS):
        vt_ref[0, h * V_ROWS:(h + 1) * V_ROWS, :] = jnp.concatenate([vt[h * MLA_V:(h + 1) * MLA_V], ones], axis=0)


def _mla_prep(a, qn, kvn, wq, wk, wv, qg, kg, tabs, seq):
    t = a.shape[0]
    tm = TOKEN_TILE
    nt = seq // tm
    full = lambda r, c: pl.BlockSpec((r, c), lambda i: (0, 0))
    tab = pl.BlockSpec((tm, LANE), lambda i: (i % nt, 0))
    return pl.pallas_call(
        _mla_prep_kernel,
        grid=(t // tm,),
        in_specs=[pl.BlockSpec((tm, MLA_IN), lambda i: (i, 0)),
                  full(1, MLA_Q_LORA), full(1, MLA_KV_LORA),
                  full(MLA_Q_LORA, MLA_HEADS * LANE), full(MLA_KV_LORA, MLA_HEADS * LANE),
                  full(MLA_KV_LORA, MLA_WIDTH), full(1, LANE), full(1, LANE), tab, tab, tab],
        out_specs=[pl.BlockSpec((tm, MLA_HEADS * LANE), lambda i: (i, 0)),
                   pl.BlockSpec((tm, MLA_HEADS * LANE), lambda i: (i, 0)),
                   pl.BlockSpec((1, MLA_HEADS * V_ROWS, tm), lambda i: (i, 0, 0))],
        out_shape=[jax.ShapeDtypeStruct((t, MLA_HEADS * LANE), BF16),
                   jax.ShapeDtypeStruct((t, MLA_HEADS * LANE), BF16),
                   jax.ShapeDtypeStruct((t // tm, MLA_HEADS * V_ROWS, tm), BF16)],
        compiler_params=_cparams("parallel"),
        name="mla_prep",
    )(a, qn, kvn, wq, wk, wv, qg, kg, *tabs)


def _diff_prep_kernel(cin_ref, qg_ref, kg_ref, bd_ref, c_ref, sa_ref, sb_ref, q_ref, k_ref, vt_ref):
    cin = cin_ref[...]
    bd = bd_ref[...]
    c = jnp.concatenate([c_ref[...], c_ref[...]], axis=1)
    sa = jnp.concatenate([sa_ref[...], sa_ref[...]], axis=1)
    sb = jnp.concatenate([sb_ref[...], sb_ref[...]], axis=1)
    lane = lax.broadcasted_iota(I32, (1, LANE), 1)

    def norm_rope(x, g):
        ms = _dot_split_lhs(x * x, bd, 3) / DIFF_QK
        x = x * lax.rsqrt(ms + EPS) * g
        halves = []
        for j in range(2):
            sl = slice(j * LANE, (j + 1) * LANE)
            halves.append(_rope(x[:, sl], c[:, sl], sa[:, sl], sb[:, sl], DIFF_QK // 2))
        return halves

    qh = norm_rope(cin[:, :256], qg_ref[...])
    kh = norm_rope(cin[:, 256:512], kg_ref[...])
    scale = DIFF_QK ** -0.5 * LOG2E
    for j in range(2 * DIFF_HEADS):
        grp, sub = divmod(j, LANE // DIFF_QK)
        keep = (lane >= sub * DIFF_QK) & (lane < (sub + 1) * DIFF_QK)
        q_ref[:, j * LANE:(j + 1) * LANE] = jnp.where(keep, qh[grp] * scale, 0.0).astype(BF16)
    k_ref[...] = jnp.concatenate(kh, axis=1).astype(BF16)
    vt = cin[:, 512:].T.astype(BF16)
    ones = jnp.ones((V_ROWS - DIFF_V, vt.shape[1]), BF16)
    for h in range(DIFF_HEADS):
        vt_ref[0, h * V_ROWS:(h + 1) * V_ROWS, :] = jnp.concatenate([vt[h * DIFF_V:(h + 1) * DIFF_V], ones], axis=0)


def _diff_prep(cin, qg, kg, bd, tabs, seq):
    t = cin.shape[0]
    tm = TOKEN_TILE
    nt = seq // tm
    full = lambda r, c: pl.BlockSpec((r, c), lambda i: (0, 0))
    tab = pl.BlockSpec((tm, LANE), lambda i: (i % nt, 0))
    nq = 2 * DIFF_HEADS * LANE
    return pl.pallas_call(
        _diff_prep_kernel,
        grid=(t // tm,),
        in_specs=[pl.BlockSpec((tm, DIFF_IN), lambda i: (i, 0)),
                  full(1, 256), full(1, 256), full(256, 256), tab, tab, tab],
        out_specs=[pl.BlockSpec((tm, nq), lambda i: (i, 0)),
                   pl.BlockSpec((tm, 256), lambda i: (i, 0)),
                   pl.BlockSpec((1, DIFF_HEADS * V_ROWS, tm), lambda i: (i, 0, 0))],
        out_shape=[jax.ShapeDtypeStruct((t, nq), BF16),
                   jax.ShapeDtypeStruct((t, 256), BF16),
                   jax.ShapeDtypeStruct((t // tm, DIFF_HEADS * V_ROWS, tm), BF16)],
        compiler_params=_cparams("parallel"),
        name="diff_prep",
    )(cin, qg, kg, bd, *tabs)


def _attn_noshift_kernel(q_ref, k_ref, vt_ref, o_ref, acc_scr, l_scr, pa_scr, pb_scr, *, nkc, tk, dv):
    q = q_ref[0]

    def probs(kc, p_ref):
        k = k_ref[0, pl.ds(pl.multiple_of(kc * tk, tk), tk), :]
        p = jnp.exp2(_dot_nt(k, q))
        p_ref[...] = p.astype(BF16)
        return jnp.sum(p, axis=0, keepdims=True)

    def pv(kc, p_ref):
        return _dot(vt_ref[0, kc, :dv, :], p_ref[...])

    bufs = (pa_scr, pb_scr)
    unroll = ATTN_GROUP

    def run(first, count, last):
        tot = None
        lsum = None
        for j in range(count):
            if not (last and j == count - 1):
                ls = probs(first + j + 1, bufs[(j + 1) % 2])
                lsum = ls if lsum is None else lsum + ls
            d = pv(first + j, bufs[j % 2])
            tot = d if tot is None else tot + d
        return tot, lsum

    l_scr[...] = probs(0, pa_scr)
    acc_scr[...] = jnp.zeros_like(acc_scr)
    trips = (nkc - 1) // unroll

    def body(i, carry):
        tot, lsum = run(i * unroll, unroll, False)
        acc_scr[...] += tot
        l_scr[...] += lsum
        return carry

    lax.fori_loop(0, trips, body, 0)
    tot, lsum = run(trips * unroll, nkc - trips * unroll, True)
    den = l_scr[...] if lsum is None else l_scr[...] + lsum
    o_ref[0] = ((acc_scr[...] + tot) / den).astype(o_ref.dtype)


def _attn_online_kernel(q_ref, k_ref, vt_ref, o_ref, m_scr, acc_scr, *, nkc, tk, dv):
    q = q_ref[0]
    m_scr[...] = jnp.full_like(m_scr, -jnp.inf)
    acc_scr[...] = jnp.zeros_like(acc_scr)

    def body(kc, carry):
        k = k_ref[0, pl.ds(pl.multiple_of(kc * tk, tk), tk), :]
        s = _dot_nt(k, q)
        m_prev = m_scr[...]
        m_new = jnp.maximum(m_prev, jnp.max(s, axis=0, keepdims=True))
        p = jnp.exp2(s - m_new).astype(BF16)
        acc_scr[...] = jnp.exp2(m_prev - m_new) * acc_scr[...] + _dot(vt_ref[0, kc], p)
        m_scr[...] = m_new
        return carry

    lax.fori_loop(0, nkc, body, 0)
    acc = acc_scr[...]
    o_ref[0] = (acc[:dv] / acc[dv:dv + 1]).astype(o_ref.dtype)


def _attention(q, k, vt, heads, kmap, vmap, dv, tk, score_bound, out_dtype):
    b, s, _ = q.shape
    tq = min(ATTN_TQ, s)
    nkc = s // tk

    def call(body, scratch, name):
        return pl.pallas_call(
            functools.partial(body, nkc=nkc, tk=tk, dv=dv),
            grid=(b, heads, s // tq),
            in_specs=[pl.BlockSpec((1, tq, LANE), lambda bi, h, qi: (bi, qi, h)),
                      pl.BlockSpec((1, s, LANE), lambda bi, h, qi: (bi, 0, kmap(h))),
                      pl.BlockSpec((1, nkc, V_ROWS, tk), lambda bi, h, qi: (bi, 0, vmap(h), 0))],
            out_specs=pl.BlockSpec((1, dv, tq), lambda bi, h, qi: (bi, h, qi)),
            out_shape=jax.ShapeDtypeStruct((b, heads * dv, s), out_dtype),
            scratch_shapes=scratch,
            compiler_params=_cparams("parallel", "parallel", "arbitrary"),
            name=name,
        )

    acc = pltpu.VMEM((V_ROWS, tq), F32)
    pbuf = pltpu.VMEM((tk, tq), BF16)
    fast = call(_attn_noshift_kernel, [pltpu.VMEM((dv, tq), F32), pltpu.VMEM((1, tq), F32), pbuf, pbuf],
                "attention")
    safe = call(_attn_online_kernel, [pltpu.VMEM((1, tq), F32), acc], "attention_online")
    return lax.cond(score_bound < NOSHIFT_MAX_LOG2, fast, safe, q, k, vt)


def _ssd_kernel(*refs, rev, nct, t):
    if rev:
        (main_ref, prev_ref, next_ref, yf_ref, cw_ref, cb_ref, dtb_ref, alog_ref, dsk_ref, nw_ref,
         out_ref, h_scr) = refs
    else:
        (main_ref, prev_ref, next_ref, cw_ref, cb_ref, dtb_ref, alog_ref, out_ref, h_scr) = refs
    i = pl.program_id(1)
    c = (nct - 1 - i) if rev else i

    @pl.when(i == 0)
    def _():
        h_scr[...] = jnp.zeros_like(h_scr)

    main = main_ref[...]
    xlo, xhi = SSD_INNER, SSD_INNER + SSD_CONV_DIM
    prev = jnp.where(c > 0, prev_ref[:, xlo:xhi], 0.0)
    nxt = jnp.where(c < nct - 1, next_ref[:, xlo:xhi], 0.0)
    xp = jnp.concatenate([prev, main[:, xlo:xhi], nxt], axis=0)
    acc = jnp.broadcast_to(cb_ref[...], (t, SSD_CONV_DIM))
    for j in range(SSD_CONV):
        sh = (SSD_CONV // 2 - j) % (t + 2 * SUBLANE)
        r = xp if sh == 0 else pltpu.roll(xp, sh, 0)
        acc = acc + r[SUBLANE:SUBLANE + t] * cw_ref[j:j + 1, :]
    xc = _silu(acc)
    xs = xc[:, :SSD_INNER]
    gn = SSD_GROUPS * SSD_STATE
    bmat = xc[:, SSD_INNER:SSD_INNER + gn]
    cmat = xc[:, SSD_INNER + gn:]

    dtr = main[:, xhi:] + dtb_ref[...]
    dt = jnp.maximum(dtr, 0.0) + jnp.log1p(jnp.exp(-jnp.abs(dtr)))
    a = dt * (-jnp.exp(alog_ref[...]))

    row = lax.broadcasted_iota(I32, (t, t), 0)
    col = lax.broadcasted_iota(I32, (t, t), 1)
    lower = row >= col
    tri_l = jnp.where(lower, 1.0, 0.0).astype(BF16)
    tri_u = jnp.where(row <= col, 1.0, 0.0).astype(BF16)
    a_t = a.T
    cs = _dot_split_rhs(tri_l, a, 3)
    cs_t = _dot_split_lhs(a_t, tri_u, 3)
    tot = cs[t - 1:t, :]
    if rev:
        ecol = cs - a
        erow = cs_t - a_t
    lane = lax.broadcasted_iota(I32, (1, LANE), 1)
    hb = SSD_HEADS if rev else 0
    hpg = SSD_HEADS // SSD_GROUPS

    g = []
    bt = []
    for grp in range(SSD_GROUPS):
        bg = bmat[:, grp * SSD_STATE:(grp + 1) * SSD_STATE]
        cg = cmat[:, grp * SSD_STATE:(grp + 1) * SSD_STATE].astype(BF16)
        g.append((_dot_nt(cg, bg.astype(BF16)), cg))
        bt.append(bg.T.astype(BF16))

    ys = []
    for pair in range(SSD_HEADS // 2):
        xpair = xs[:, pair * LANE:(pair + 1) * LANE]
        hstate = h_scr[pair]
        hbf = hstate.astype(BF16)
        y = jnp.zeros((t, LANE), F32)
        hnew = jnp.zeros((SSD_STATE, LANE), F32)
        dec = jnp.zeros((1, LANE), F32)
        for sub in range(2):
            h = pair * 2 + sub
            grp = h // hpg
            gmat, cg = g[grp]
            keep = (lane >= sub * SSD_HEAD_DIM) & (lane < (sub + 1) * SSD_HEAD_DIM)
            hl = hb + h
            dtc = dt[:, hl:hl + 1]
            xm = jnp.where(keep, xpair * dtc, 0.0)
            if rev:
                dmat = jnp.exp(jnp.where(row <= col, erow[hl:hl + 1, :] - ecol[:, hl:hl + 1], -jnp.inf))
                off_scale = jnp.exp(tot[:, hl:hl + 1] - ecol[:, hl:hl + 1])
                st_w = jnp.exp(ecol[:, hl:hl + 1])
            else:
                dmat = jnp.exp(jnp.where(lower, cs[:, hl:hl + 1] - cs_t[hl:hl + 1, :], -jnp.inf))
                off_scale = jnp.exp(cs[:, hl:hl + 1])
                st_w = jnp.exp(tot[:, hl:hl + 1] - cs[:, hl:hl + 1])
            y = y + _dot((gmat * dmat).astype(BF16), xm.astype(BF16))
            y = y + jnp.where(keep, _dot(cg, hbf) * off_scale, 0.0)
            hnew = hnew + _dot(bt[grp], (xm * st_w).astype(BF16))
            dec = dec + jnp.where(keep, jnp.exp(tot[:, hl:hl + 1]), 0.0)
        h_scr[pair] = hstate * dec + hnew
        ys.append(y)

    yall = jnp.concatenate(ys, axis=1)
    if rev:
        yall = yall + yf_ref[...] + xs * dsk_ref[...]
        gt = yall * _silu(main[:, :SSD_INNER])
        yall = gt * lax.rsqrt(jnp.mean(gt * gt, axis=-1, keepdims=True) + EPS) * nw_ref[...]
    out_ref[...] = yall.astype(out_ref.dtype)


def _ssd_pass(bin_, yf, cw, cb, dtb, alog, dsk, nw, batch, seq, rev):
    t = SSD_T
    nct = seq // t
    hb = t // SUBLANE
    nrow8 = batch * seq // SUBLANE

    def cidx(i):
        return (nct - 1 - i) if rev else i

    main = pl.BlockSpec((t, SSD_IN), lambda b, i: (b * nct + cidx(i), 0))
    prev = pl.BlockSpec((SUBLANE, SSD_IN),
                        lambda b, i: (jnp.maximum((b * nct + cidx(i)) * hb - 1, 0), 0))
    nxt = pl.BlockSpec((SUBLANE, SSD_IN),
                       lambda b, i: (jnp.minimum((b * nct + cidx(i) + 1) * hb, nrow8 - 1), 0))
    full = lambda r, c: pl.BlockSpec((r, c), lambda b, i: (0, 0))
    yspec = pl.BlockSpec((t, SSD_INNER), lambda b, i: (b * nct + cidx(i), 0))
    if rev:
        in_specs = [main, prev, nxt, yspec, full(SSD_CONV, SSD_CONV_DIM), full(1, SSD_CONV_DIM),
                    full(1, LANE), full(1, LANE), full(1, SSD_INNER), full(1, SSD_INNER)]
        args = (bin_, bin_, bin_, yf, cw, cb, dtb, alog, dsk, nw)
    else:
        in_specs = [main, prev, nxt, full(SSD_CONV, SSD_CONV_DIM), full(1, SSD_CONV_DIM),
                    full(1, LANE), full(1, LANE)]
        args = (bin_, bin_, bin_, cw, cb, dtb, alog)
    return pl.pallas_call(
        functools.partial(_ssd_kernel, rev=rev, nct=nct, t=t),
        grid=(batch, nct),
        in_specs=in_specs,
        out_specs=yspec,
        out_shape=jax.ShapeDtypeStruct((batch * seq, SSD_INNER), BF16 if rev else F32),
        scratch_shapes=[pltpu.VMEM((SSD_HEADS // 2, SSD_STATE, LANE), F32)],
        compiler_params=_cparams("parallel", "arbitrary"),
        name="ssd_bwd" if rev else "ssd_fwd",
    )(*args)


def _outproj_kernel(x_ref, mla_ref, ssd_ref, dif_ref, lam_ref, sub_ref, wo_ref, fn_ref, wrh_ref,
                    xo_ref, hn_ref, aff_ref, *, lambda_init):
    lam = lam_ref[...]
    lam_full = (jnp.exp(jnp.sum(lam[0:1] * lam[1:2], keepdims=True))
                - jnp.exp(jnp.sum(lam[2:3] * lam[3:4], keepdims=True)) + lambda_init)
    dif = dif_ref[0]
    outs = []
    for hd in range(DIFF_HEADS):
        o = dif[2 * hd * DIFF_V:(2 * hd + 1) * DIFF_V] - lam_full * dif[(2 * hd + 1) * DIFF_V:(2 * hd + 2) * DIFF_V]
        o = o * lax.rsqrt(jnp.mean(o * o, axis=0, keepdims=True) + EPS) * sub_ref[...] * (1.0 - lambda_init)
        outs.append(o)
    dt_ = jnp.concatenate(outs, axis=0).astype(BF16)
    wo = wo_ref[...]
    x = x_ref[...]
    x = x + _dot_tn(mla_ref[0].astype(BF16), wo[:MLA_WIDTH])
    x = x + _dot(ssd_ref[...].astype(BF16), wo[MLA_WIDTH:MLA_WIDTH + SSD_INNER])
    x = x + _dot_tn(dt_, wo[MLA_WIDTH + SSD_INNER:])
    xo_ref[...] = x
    hn = x * lax.rsqrt(jnp.mean(x * x, axis=-1, keepdims=True) + EPS) * fn_ref[...]
    hn_ref[...] = hn
    hh, hl = _split_bf16(hn, 2)
    both = _dot(hh, wrh_ref[...])
    logits = both[:, :LANE] + both[:, LANE:] + _dot(hl, wrh_ref[:, :LANE])
    lane = lax.broadcasted_iota(I32, logits.shape, 1)
    logits = jnp.where(lane < N_EXPERTS, logits, -jnp.inf)
    e = jnp.exp(logits - jnp.max(logits, axis=-1, keepdims=True))
    aff = e / jnp.sum(e, axis=-1, keepdims=True)
    aff_ref[...] = aff.T[:N_EXPERTS]


def _outproj(x2, mla_t, ssd, dif_t, lam, sub, wo, fn, wr2, seq, lambda_init):
    t = x2.shape[0]
    tm = TOKEN_TILE
    nt = seq // tm
    full = lambda r, c: pl.BlockSpec((r, c), lambda i: (0, 0))
    row = lambda c: pl.BlockSpec((tm, c), lambda i: (i, 0))
    return pl.pallas_call(
        functools.partial(_outproj_kernel, lambda_init=lambda_init),
        grid=(t // tm,),
        in_specs=[row(D_MODEL),
                  pl.BlockSpec((1, MLA_WIDTH, tm), lambda i: (i // nt, 0, i % nt)),
                  row(SSD_INNER),
                  pl.BlockSpec((1, 2 * DIFF_WIDTH, tm), lambda i: (i // nt, 0, i % nt)),
                  full(4, DIFF_QK), full(DIFF_V, 1), full(D_MODEL, D_MODEL), full(1, D_MODEL),
                  full(D_MODEL, 2 * LANE)],
        out_specs=[row(D_MODEL), row(D_MODEL), pl.BlockSpec((N_EXPERTS, tm), lambda i: (0, i))],
        out_shape=[jax.ShapeDtypeStruct((t, D_MODEL), F32),
                   jax.ShapeDtypeStruct((t, D_MODEL), F32),
                   jax.ShapeDtypeStruct((N_EXPERTS, t), F32)],
        compiler_params=_cparams("parallel"),
        name="outproj",
    )(x2, mla_t, ssd, dif_t, lam, sub, wo, fn, wr2)


def _cumsum_rowmajor(mask_f32, tri_u, tri_ls):
    local = _dot(mask_f32.astype(BF16), tri_u)
    rowtot = jnp.broadcast_to(local[:, LANE - 1:LANE], local.shape).astype(BF16)
    return local + _dot(tri_ls, rowtot)


def _select_kernel(aff_ref, pos_ref, idx_ref, toff_ref, tcnt_ref, cum_scr, *, cap, sb, rows):
    a = aff_ref[0]
    key = lax.bitcast_convert_type(a, I32)

    def bit_body(it, prefix):
        cand = prefix | lax.shift_left(jnp.int32(1), 30 - it)
        cnt = jnp.sum((key >= cand).astype(I32))
        return jnp.where(cnt >= cap, cand, prefix)

    thr = lax.fori_loop(0, 31, bit_body, jnp.int32(0))
    gt = key > thr
    eq = key == thr
    need = cap - jnp.sum(gt.astype(I32))
    r_i = lax.broadcasted_iota(I32, (LANE, LANE), 0)
    c_i = lax.broadcasted_iota(I32, (LANE, LANE), 1)
    tri_u = jnp.where(r_i <= c_i, 1.0, 0.0).astype(BF16)
    r_r = lax.broadcasted_iota(I32, (rows, rows), 0)
    c_r = lax.broadcasted_iota(I32, (rows, rows), 1)
    tri_ls = jnp.where(r_r > c_r, 1.0, 0.0).astype(BF16)
    eqf = jnp.where(eq, 1.0, 0.0)
    tie_rank = _cumsum_rowmajor(eqf, tri_u, tri_ls) - eqf
    sel = gt | (eq & (tie_rank < need.astype(F32)))
    self_ = jnp.where(sel, 1.0, 0.0)
    cum = _cumsum_rowmajor(self_, tri_u, tri_ls).astype(I32)
    pos_ref[0] = jnp.where(sel, cum - 1, -1)
    cum_scr[...] = cum
    rowend = cum[:, LANE - 1:LANE]
    rowstart = rowend - jnp.sum(self_, axis=-1, keepdims=True).astype(I32)
    toff_ref[0] = rowstart
    tcnt_ref[0] = rowend - rowstart

    def blk(bi, carry):
        p0 = bi * sb
        r_lo = jnp.sum((rowend <= p0).astype(I32))
        r_hi = jnp.sum((rowstart < p0 + sb).astype(I32))
        s_col = p0 + lax.broadcasted_iota(I32, (sb, LANE), 0)

        def rowbody(r, acc):
            return acc + (cum_scr[pl.ds(r, 1), :] <= s_col).astype(I32)

        acc = lax.fori_loop(r_lo, r_hi, rowbody, jnp.zeros((sb, LANE), I32))
        idx_ref[0, pl.ds(pl.multiple_of(p0, sb), sb), :] = jnp.sum(acc, axis=1, keepdims=True) + r_lo * LANE
        return carry

    lax.fori_loop(0, cap // sb, blk, 0)


def _select(aff3, cap):
    e, rows, _ = aff3.shape
    sb = min(IDX_SLOTS, cap)
    return pl.pallas_call(
        functools.partial(_select_kernel, cap=cap, sb=sb, rows=rows),
        grid=(e,),
        in_specs=[pl.BlockSpec((1, rows, LANE), lambda i: (i, 0, 0))],
        out_specs=[pl.BlockSpec((1, rows, LANE), lambda i: (i, 0, 0)),
                   pl.BlockSpec((1, cap, 1), lambda i: (i, 0, 0)),
                   pl.BlockSpec((1, rows, 1), lambda i: (i, 0, 0)),
                   pl.BlockSpec((1, rows, 1), lambda i: (i, 0, 0))],
        out_shape=[jax.ShapeDtypeStruct((e, rows, LANE), I32),
                   jax.ShapeDtypeStruct((e, cap, 1), I32),
                   jax.ShapeDtypeStruct((e, rows, 1), I32),
                   jax.ShapeDtypeStruct((e, rows, 1), I32)],
        scratch_shapes=[pltpu.VMEM((rows, LANE), I32)],
        compiler_params=_cparams("parallel"),
        name="select",
    )(aff3)


def _row_copy(src_hbm, src_row, dst_vmem, dst_row, sem):
    return pltpu.make_async_copy(src_hbm.at[pl.ds(src_row, 1)], dst_vmem.at[pl.ds(dst_row, 1)], sem)


def _ffn_kernel(idx_ref, nxt_ref, hn_hbm, wg_ref, wu_ref, wd_ref, ye_ref, xbuf, sem, *, slots, steps):
    n = pl.program_id(0) * pl.num_programs(1) + pl.program_id(1)
    cur = n % 2

    def gather(iref, buf_slot):
        for r in range(slots):
            _row_copy(hn_hbm, iref[0, 0, r], xbuf.at[buf_slot], r, sem.at[buf_slot]).start()

    def wait_all(buf_slot):
        pltpu.make_async_copy(hn_hbm.at[pl.ds(0, slots)], xbuf.at[buf_slot], sem.at[buf_slot]).wait()

    @pl.when(n == 0)
    def _():
        gather(idx_ref, 0)

    gather(nxt_ref, 1 - cur)
    wait_all(cur)
    x = xbuf[cur].astype(BF16)
    gate = _dot(x, wg_ref[0])
    up = _dot(x, wu_ref[0])
    hid = (_silu(gate) * up).astype(BF16)
    ye_ref[...] = _dot(hid, wd_ref[0]).astype(BF16)

    @pl.when(n == steps - 1)
    def _():
        wait_all(1 - cur)


def _ffn(idx3, hn, wg, wu, wd, cap):
    slots = idx3.shape[-1]
    nb = cap // slots
    steps = N_EXPERTS * nb
    wspec = lambda: pl.BlockSpec((1, D_MODEL, EXPERT_FF), lambda e, b: (e, 0, 0))
    ispec = lambda f: pl.BlockSpec((1, 1, slots), f, memory_space=pltpu.SMEM)
    return pl.pallas_call(
        functools.partial(_ffn_kernel, slots=slots, steps=steps),
        grid=(N_EXPERTS, nb),
        in_specs=[ispec(lambda e, b: (e * nb + b, 0, 0)),
                  ispec(lambda e, b: (jnp.minimum(e * nb + b + 1, steps - 1), 0, 0)),
                  pl.BlockSpec(memory_space=pl.ANY),
                  wspec(), wspec(),
                  pl.BlockSpec((1, EXPERT_FF, D_MODEL), lambda e, b: (e, 0, 0))],
        out_specs=pl.BlockSpec((slots, D_MODEL), lambda e, b: (e * nb + b, 0)),
        out_shape=jax.ShapeDtypeStruct((N_EXPERTS * cap, D_MODEL), BF16),
        scratch_shapes=[pltpu.VMEM((2, slots, D_MODEL), F32), pltpu.SemaphoreType.DMA((2,))],
        compiler_params=_cparams("arbitrary", "arbitrary"),
        name="expert_ffn",
    )(idx3, idx3, hn, wg, wu, wd)


YE_SHIFT = 4
YE_BLOCK = 1 << YE_SHIFT
COMBINE_MAX_BLOCKS = N_EXPERTS * (COMBINE_TILE // YE_BLOCK + 1)
COMBINE_CHUNK = 512
COMBINE_MAX_ROWS = -(-COMBINE_MAX_BLOCKS * YE_BLOCK // COMBINE_CHUNK) * COMBINE_CHUNK


def _combine_kernel(toff_ref, tcnt_ref, x_ref, pos_ref, aff_ref, ye_hbm, out_ref, buf, sem, *, cap, rows, chunk):
    r = pl.program_id(0)
    cur = r % 2

    def layout(tile):
        firsts, nblks, dsts = [], [], []
        dst = jnp.int32(0)
        for e in range(N_EXPERTS):
            off = toff_ref[e * rows + tile]
            cnt = tcnt_ref[e * rows + tile]
            first = (off >> YE_SHIFT) << YE_SHIFT
            firsts.append(first)
            nblks.append(jnp.where(cnt > 0, (off + cnt - first + YE_BLOCK - 1) >> YE_SHIFT, 0))
            dsts.append(dst)
            dst = dst + nblks[-1]
        return firsts, nblks, dsts, dst

    def gather(tile, buf_slot):
        firsts, nblks, dsts, _ = layout(tile)
        for e in range(N_EXPERTS):
            def issue(j, carry, e=e):
                src = pl.multiple_of(e * cap + firsts[e] + j * YE_BLOCK, YE_BLOCK)
                dst = pl.multiple_of((dsts[e] + j) * YE_BLOCK, YE_BLOCK)
                pltpu.make_async_copy(ye_hbm.at[pl.ds(src, YE_BLOCK)], buf.at[buf_slot, pl.ds(dst, YE_BLOCK)],
                                      sem.at[buf_slot]).start()
                return carry

            lax.fori_loop(0, nblks[e], issue, 0)

    @pl.when(r == 0)
    def _():
        gather(0, 0)

    @pl.when(r + 1 < rows)
    def _():
        gather(r + 1, 1 - cur)

    firsts, _, dsts, nblk_total = layout(r)
    total = nblk_total * YE_BLOCK
    for bit in range(COMBINE_MAX_BLOCKS.bit_length()):
        @pl.when(((nblk_total >> bit) & 1) == 1)
        def _(bit=bit):
            n = YE_BLOCK << bit
            pltpu.make_async_copy(ye_hbm.at[pl.ds(0, n)], buf.at[cur, pl.ds(0, n)], sem.at[cur]).wait()

    out_ref[...] = x_ref[...]
    pos = pos_ref[...]
    aff = aff_ref[...]
    tgt = [jnp.where(pos[e:e + 1] >= 0, pos[e:e + 1] - firsts[e] + dsts[e] * YE_BLOCK, -1)
           for e in range(N_EXPERTS)]

    def chunk_body(k, carry):
        base = pl.multiple_of(k * chunk, chunk)
        cidx = base + lax.broadcasted_iota(I32, (chunk, COMBINE_TILE), 0)
        wt = jnp.zeros((chunk, COMBINE_TILE), F32)
        for e in range(N_EXPERTS):
            wt = wt + jnp.where(cidx == tgt[e], aff[e:e + 1], 0.0)
        rowid = base + lax.broadcasted_iota(I32, (chunk, 1), 0)
        rowsv = jnp.where(rowid < total, buf[cur, pl.ds(base, chunk), :], jnp.zeros((), BF16))
        wh, wl = _split_bf16(wt, 2)
        out_ref[...] += _dot_tn(wh, rowsv) + _dot_tn(wl, rowsv)
        return carry

    lax.fori_loop(0, (total + chunk - 1) // chunk, chunk_body, 0)


def _combine(toff, tcnt, x2, pos, aff_t, ye, cap):
    t = x2.shape[0]
    tile = COMBINE_TILE
    rows = t // tile
    chunk = COMBINE_CHUNK
    grid_spec = pltpu.PrefetchScalarGridSpec(
        num_scalar_prefetch=2,
        grid=(rows,),
        in_specs=[pl.BlockSpec((tile, D_MODEL), lambda i, *_: (i, 0)),
                  pl.BlockSpec((N_EXPERTS, tile), lambda i, *_: (0, i)),
                  pl.BlockSpec((N_EXPERTS, tile), lambda i, *_: (0, i)),
                  pl.BlockSpec(memory_space=pl.ANY)],
        out_specs=pl.BlockSpec((tile, D_MODEL), lambda i, *_: (i, 0)),
        scratch_shapes=[pltpu.VMEM((2, COMBINE_MAX_ROWS, D_MODEL), BF16), pltpu.SemaphoreType.DMA((2,))],
    )
    return pl.pallas_call(
        functools.partial(_combine_kernel, cap=cap, rows=rows, chunk=chunk),
        grid_spec=grid_spec,
        out_shape=jax.ShapeDtypeStruct((t, D_MODEL), F32),
        compiler_params=_cparams("arbitrary"),
        name="combine",
    )(toff, tcnt, x2, pos, aff_t, ye)


def _rope_tables(seq):
    inv = ROPE_THETA ** (-jnp.arange(0, MLA_ROPE, 2, dtype=F32) / MLA_ROPE)
    ang = jnp.arange(seq, dtype=F32)[:, None] * inv[None, :]
    cos, sin = jnp.cos(ang), jnp.sin(ang)
    one = jnp.ones((seq, 1), F32)
    zero = jnp.zeros((seq, 1), F32)
    rep = lambda v, n: jnp.broadcast_to(v, (seq, n))
    half = MLA_ROPE // 2
    mla = (jnp.concatenate([rep(one, MLA_NOPE), cos, cos, rep(one, LANE - MLA_QK)], axis=1),
           jnp.concatenate([rep(zero, MLA_NOPE + half), sin, rep(zero, LANE - MLA_QK)], axis=1),
           jnp.concatenate([rep(zero, MLA_NOPE), -sin, rep(zero, LANE - MLA_NOPE - half)], axis=1))
    nblk = LANE // DIFF_QK
    dif = (jnp.tile(jnp.concatenate([cos, cos], axis=1), (1, nblk)),
           jnp.tile(jnp.concatenate([rep(zero, half), sin], axis=1), (1, nblk)),
           jnp.tile(jnp.concatenate([-sin, rep(zero, half)], axis=1), (1, nblk)))
    return mla, dif


def _pad_cols(w, width):
    return jnp.pad(w, ((0, 0), (0, width - w.shape[1])))


def _layer_params(p, l):
    w_in = p['w_in'][l]
    z64 = jnp.zeros((D_MODEL, 64), F32)
    z32 = jnp.zeros((D_MODEL, 32), F32)
    w_proj = jnp.concatenate([
        w_in[:, :OFF_KPE], z64, w_in[:, OFF_KPE:OFF_Z], z32,
        w_in[:, OFF_Z:OFF_DT], _pad_cols(w_in[:, OFF_DT:OFF_DQ], LANE),
        w_in[:, OFF_DQ:]], axis=1).astype(BF16)
    wq = p['mla_w_uq'][l].reshape(MLA_Q_LORA, MLA_HEADS, MLA_QK)
    wq = jnp.pad(wq, ((0, 0), (0, 0), (0, LANE - MLA_QK))).reshape(MLA_Q_LORA, MLA_HEADS * LANE)
    wkv = p['mla_w_ukv'][l].reshape(MLA_KV_LORA, MLA_HEADS, MLA_NOPE + MLA_V)
    wk = jnp.pad(wkv[:, :, :MLA_NOPE], ((0, 0), (0, 0), (0, LANE - MLA_NOPE))).reshape(MLA_KV_LORA, MLA_HEADS * LANE)
    wv = wkv[:, :, MLA_NOPE:].reshape(MLA_KV_LORA, MLA_WIDTH)
    wr = _pad_cols(p['w_router'][l], LANE)
    wrh = wr.astype(BF16)
    w_router2 = jnp.concatenate([wrh, (wr - wrh.astype(F32)).astype(BF16)], axis=1)
    pad_row = lambda v, n: jnp.pad(v, (0, n - v.shape[0]))[None, :]
    return dict(
        attn_norm=p['attn_norm'][l][None, :], w_proj=w_proj,
        q_norm=p['mla_q_norm'][l][None, :], kv_norm=p['mla_kv_norm'][l][None, :],
        wq=wq.astype(BF16), wk=wk.astype(BF16), wv=wv.astype(BF16),
        q_gain=pad_row(p['mla_q_gain'][l], LANE), k_gain=pad_row(p['mla_k_gain'][l], LANE),
        conv_w=p['ssd_conv_w'][l], conv_b=p['ssd_conv_b'][l][None, :],
        dt_bias=pad_row(p['ssd_dt_bias'][l].reshape(-1), LANE),
        a_log=pad_row(p['ssd_a_log'][l].reshape(-1), LANE),
        d_skip=jnp.repeat(p['ssd_d'][l], SSD_HEAD_DIM)[None, :], ssd_norm=p['ssd_norm'][l][None, :],
        dq_gain=jnp.tile(p['diff_q_gain'][l], 2 * DIFF_HEADS)[None, :],
        dk_gain=jnp.tile(p['diff_k_gain'][l], 2 * DIFF_HEADS)[None, :],
        lam=p['diff_lambda'][l], subln=p['diff_subln'][l][:, None],
        w_out=p['w_out'][l].astype(BF16), ffn_norm=p['ffn_norm'][l][None, :],
        w_router2=w_router2,
        w_gate=p['w_gate'][l].astype(BF16), w_up=p['w_up'][l].astype(BF16), w_down=p['w_down'][l].astype(BF16),
    )


def _block_diag_ones(n, blk):
    i = jnp.arange(n)
    return (i[:, None] // blk == i[None, :] // blk).astype(BF16)


def _trunk(x, p):
    batch, seq, _ = x.shape
    t = batch * seq
    cap = EC_CAPACITY * t // N_EXPERTS
    tk = min(ATTN_TK, seq)
    mla_tabs, dif_tabs = _rope_tables(seq)
    bd = _block_diag_ones(2 * DIFF_HEADS * DIFF_QK, DIFF_QK)
    x2 = x.reshape(t, D_MODEL)
    rows = t // LANE
    for l in range(DEPTH):
        lp = _layer_params(p, l)
        a_in, b_in, c_in = _inproj(x2, lp['attn_norm'], lp['w_proj'])
        q, k, vt = _mla_prep(a_in, lp['q_norm'], lp['kv_norm'], lp['wq'], lp['wk'], lp['wv'],
                             lp['q_gain'], lp['k_gain'], mla_tabs, seq)
        mla_bound = (MLA_QK ** 0.5 * LOG2E) * jnp.max(jnp.abs(p['mla_q_gain'][l])) * jnp.max(jnp.abs(p['mla_k_gain'][l]))
        mla_t = _attention(q.reshape(batch, seq, -1), k.reshape(batch, seq, -1),
                           vt.reshape(batch, seq // tk, MLA_HEADS * V_ROWS, tk),
                           MLA_HEADS, lambda h: h, lambda h: h, MLA_V, tk, mla_bound, BF16)
        dq, dk, dvt = _diff_prep(c_in, lp['dq_gain'], lp['dk_gain'], bd, dif_tabs, seq)
        dif_bound = (DIFF_QK ** 0.5 * LOG2E) * jnp.max(jnp.abs(p['diff_q_gain'][l])) * jnp.max(jnp.abs(p['diff_k_gain'][l]))
        dif_t = _attention(dq.reshape(batch, seq, -1), dk.reshape(batch, seq, -1),
                           dvt.reshape(batch, seq // tk, DIFF_HEADS * V_ROWS, tk),
                           2 * DIFF_HEADS, lambda h: h // (LANE // DIFF_QK), lambda h: h // 2, DIFF_V, tk,
                           dif_bound, F32)
        yf = _ssd_pass(b_in, None, lp['conv_w'], lp['conv_b'], lp['dt_bias'], lp['a_log'],
                       None, None, batch, seq, rev=False)
        ssd = _ssd_pass(b_in, yf, lp['conv_w'], lp['conv_b'], lp['dt_bias'], lp['a_log'],
                        lp['d_skip'], lp['ssd_norm'], batch, seq, rev=True)
        lambda_init = 0.8 - 0.6 * math.exp(-0.3 * l)
        x2, hn, aff_t = _outproj(x2, mla_t, ssd, dif_t, lp['lam'], lp['subln'], lp['w_out'],
                                 lp['ffn_norm'], lp['w_router2'], seq, lambda_init)
        pos3, idx, toff, tcnt = _select(aff_t.reshape(N_EXPERTS, rows, LANE), cap)
        slots = min(FFN_SLOTS, cap)
        ye = _ffn(idx.reshape(N_EXPERTS * cap // slots, 1, slots), hn,
                  lp['w_gate'], lp['w_up'], lp['w_down'], cap)
        x2 = _combine(toff.reshape(-1), tcnt.reshape(-1), x2, pos3.reshape(N_EXPERTS, t), aff_t, ye, cap)
    return x2.reshape(batch, seq, D_MODEL)


def kernel(x_prompt, x_sample, attn_norm, w_in, mla_q_norm, mla_kv_norm, mla_w_uq, mla_w_ukv, mla_q_gain,
           mla_k_gain, ssd_conv_w, ssd_conv_b, ssd_dt_bias, ssd_a_log, ssd_d, ssd_norm, diff_q_gain,
           diff_k_gain, diff_lambda, diff_subln, w_out, ffn_norm, w_router, w_gate, w_up, w_down):
    p = dict(attn_norm=attn_norm, w_in=w_in, mla_q_norm=mla_q_norm, mla_kv_norm=mla_kv_norm,
             mla_w_uq=mla_w_uq, mla_w_ukv=mla_w_ukv, mla_q_gain=mla_q_gain, mla_k_gain=mla_k_gain,
             ssd_conv_w=ssd_conv_w, ssd_conv_b=ssd_conv_b, ssd_dt_bias=ssd_dt_bias, ssd_a_log=ssd_a_log,
             ssd_d=ssd_d, ssd_norm=ssd_norm, diff_q_gain=diff_q_gain, diff_k_gain=diff_k_gain,
             diff_lambda=diff_lambda, diff_subln=diff_subln, w_out=w_out, ffn_norm=ffn_norm,
             w_router=w_router, w_gate=w_gate, w_up=w_up, w_down=w_down)
    return _trunk(x_prompt, p), _trunk(x_sample, p)
```

```python
import functools
import math

import jax
import jax.numpy as jnp
from jax import lax
from jax.experimental import pallas as pl
from jax.experimental.pallas import tpu as pltpu

F32 = jnp.float32
BF16 = jnp.bfloat16
I32 = jnp.int32

D_MODEL = 1024
DEPTH = 2
EPS = 1e-6
ROPE_THETA = 10000.0

MLA_HEADS = 6
MLA_Q_LORA = 256
MLA_KV_LORA = 128
MLA_NOPE = 64
MLA_ROPE = 32
MLA_QK = MLA_NOPE + MLA_ROPE
MLA_V = 64
MLA_WIDTH = MLA_HEADS * MLA_V

SSD_HEADS = 6
SSD_HEAD_DIM = 64
SSD_INNER = SSD_HEADS * SSD_HEAD_DIM
SSD_GROUPS = 2
SSD_STATE = 128
SSD_CONV = 5
SSD_CONV_DIM = SSD_INNER + 2 * SSD_GROUPS * SSD_STATE

DIFF_HEADS = 4
DIFF_QK = 32
DIFF_V = 2 * DIFF_QK
DIFF_WIDTH = DIFF_HEADS * DIFF_V

N_EXPERTS = 16
EC_CAPACITY = 2
EXPERT_FF = 1024

OFF_KV = MLA_Q_LORA
OFF_KPE = OFF_KV + MLA_KV_LORA
OFF_Z = OFF_KPE + MLA_ROPE
OFF_XBC = OFF_Z + SSD_INNER
OFF_DT = OFF_XBC + SSD_CONV_DIM
OFF_DQ = OFF_DT + 2 * SSD_HEADS
OFF_DK = OFF_DQ + DIFF_HEADS * 2 * DIFF_QK
OFF_DV = OFF_DK + DIFF_HEADS * 2 * DIFF_QK
IN_COLS = OFF_DV + DIFF_WIDTH

LANE = 128
SUBLANE = 8

MLA_IN = 512
SSD_IN = SSD_INNER + SSD_CONV_DIM + LANE
DIFF_IN = 768
PROJ_COLS = MLA_IN + SSD_IN + DIFF_IN

TOKEN_TILE = 512
ATTN_TQ = 4096
ATTN_TK = 512
ATTN_GROUP = 4
SSD_T = 256
COMBINE_TILE = 128
V_ROWS = 80
LOG2E = math.log2(math.e)
NOSHIFT_MAX_LOG2 = 60.0
FFN_SLOTS = 512
IDX_SLOTS = 256
VMEM_LIMIT = 56 * 1024 * 1024


def _cparams(*sem):
    return pltpu.CompilerParams(dimension_semantics=sem, vmem_limit_bytes=VMEM_LIMIT)


def _split_bf16(a, terms):
    parts = []
    rem = a
    for _ in range(terms):
        p = rem.astype(BF16)
        parts.append(p)
        rem = rem - p.astype(F32)
    return parts


def _dot(a, b):
    return jnp.dot(a, b, preferred_element_type=F32)


def _dot_nt(a, b):
    return lax.dot_general(a, b, (((1,), (1,)), ((), ())), preferred_element_type=F32)


def _dot_tn(a, b):
    return lax.dot_general(a, b, (((0,), (0,)), ((), ())), preferred_element_type=F32)


def _dot_split_lhs(a_f32, b_bf16, terms):
    out = None
    for p in _split_bf16(a_f32, terms):
        d = _dot(p, b_bf16)
        out = d if out is None else out + d
    return out


def _dot_split_rhs(a_bf16, b_f32, terms):
    out = None
    for p in _split_bf16(b_f32, terms):
        d = _dot(a_bf16, p)
        out = d if out is None else out + d
    return out


def _silu(x):
    return x * jax.nn.sigmoid(x)


def _inproj_kernel(x_ref, g_ref, w_ref, a_ref, b_ref, c_ref):
    x = x_ref[...]
    ms = jnp.mean(x * x, axis=-1, keepdims=True)
    h = (x * lax.rsqrt(ms + EPS) * g_ref[...]).astype(BF16)
    y = _dot(h, w_ref[...])
    a_ref[...] = y[:, :MLA_IN]
    b_ref[...] = y[:, MLA_IN:MLA_IN + SSD_IN]
    c_ref[...] = y[:, MLA_IN + SSD_IN:]


def _inproj(x2, gain, w):
    t = x2.shape[0]
    tm = TOKEN_TILE
    return pl.pallas_call(
        _inproj_kernel,
        grid=(t // tm,),
        in_specs=[pl.BlockSpec((tm, D_MODEL), lambda i: (i, 0)),
                  pl.BlockSpec((1, D_MODEL), lambda i: (0, 0)),
                  pl.BlockSpec((D_MODEL, PROJ_COLS), lambda i: (0, 0))],
        out_specs=[pl.BlockSpec((tm, MLA_IN), lambda i: (i, 0)),
                   pl.BlockSpec((tm, SSD_IN), lambda i: (i, 0)),
                   pl.BlockSpec((tm, DIFF_IN), lambda i: (i, 0))],
        out_shape=[jax.ShapeDtypeStruct((t, MLA_IN), F32),
                   jax.ShapeDtypeStruct((t, SSD_IN), F32),
                   jax.ShapeDtypeStruct((t, DIFF_IN), F32)],
        compiler_params=_cparams("parallel"),
        name="inproj",
    )(x2, gain, w)


def _rope(x, c, sa, sb, shift):
    return x * c + pltpu.roll(x, shift, 1) * sa + pltpu.roll(x, LANE - shift, 1) * sb


def _mla_prep_kernel(a_ref, qn_ref, kvn_ref, wq_ref, wk_ref, wv_ref, qg_ref, kg_ref,
                     c_ref, sa_ref, sb_ref, q_ref, k_ref, vt_ref):
    a = a_ref[...]
    cq = a[:, :MLA_Q_LORA]
    ckv = a[:, MLA_Q_LORA:MLA_Q_LORA + MLA_KV_LORA]
    kpe = a[:, MLA_Q_LORA + MLA_KV_LORA:]
    cqn = (cq * lax.rsqrt(jnp.mean(cq * cq, axis=-1, keepdims=True) + EPS) * qn_ref[...]).astype(BF16)
    ckvn = (ckv * lax.rsqrt(jnp.mean(ckv * ckv, axis=-1, keepdims=True) + EPS) * kvn_ref[...]).astype(BF16)
    q = _dot(cqn, wq_ref[...])
    kn = _dot(ckvn, wk_ref[...])
    v = _dot(ckvn, wv_ref[...])
    c = c_ref[...]
    sa = sa_ref[...]
    sb = sb_ref[...]
    scale = MLA_QK ** -0.5 * LOG2E
    for h in range(MLA_HEADS):
        sl = slice(h * LANE, (h + 1) * LANE)
        qh = q[:, sl]
        qh = qh * lax.rsqrt(jnp.sum(qh * qh, axis=-1, keepdims=True) / MLA_QK + EPS) * qg_ref[...]
        q_ref[:, sl] = (_rope(qh, c, sa, sb, MLA_ROPE // 2) * scale).astype(BF16)
        kh = kn[:, sl] + kpe
        kh = kh * lax.rsqrt(jnp.sum(kh * kh, axis=-1, keepdims=True) / MLA_QK + EPS) * kg_ref[...]
        k_ref[:, sl] = _rope(kh, c, sa, sb, MLA_ROPE // 2).astype(BF16)
    vt = v.T.astype(BF16)
    ones = jnp.ones((V_ROWS - MLA_V, vt.shape[1]), BF16)
    for h in range(MLA_HEADS):
        vt_ref[0, h * V_ROWS:(h + 1) * V_ROWS, :] = jnp.concatenate([vt[h * MLA_V:(h + 1) * MLA_V], ones], axis=0)


def _mla_prep(a, qn, kvn, wq, wk, wv, qg, kg, tabs, seq):
    t = a.shape[0]
    tm = TOKEN_TILE
    nt = seq // tm
    full = lambda r, c: pl.BlockSpec((r, c), lambda i: (0, 0))
    tab = pl.BlockSpec((tm, LANE), lambda i: (i % nt, 0))
    return pl.pallas_call(
        _mla_prep_kernel,
        grid=(t // tm,),
        in_specs=[pl.BlockSpec((tm, MLA_IN), lambda i: (i, 0)),
                  full(1, MLA_Q_LORA), full(1, MLA_KV_LORA),
                  full(MLA_Q_LORA, MLA_HEADS * LANE), full(MLA_KV_LORA, MLA_HEADS * LANE),
                  full(MLA_KV_LORA, MLA_WIDTH), full(1, LANE), full(1, LANE), tab, tab, tab],
        out_specs=[pl.BlockSpec((tm, MLA_HEADS * LANE), lambda i: (i, 0)),
                   pl.BlockSpec((tm, MLA_HEADS * LANE), lambda i: (i, 0)),
                   pl.BlockSpec((1, MLA_HEADS * V_ROWS, tm), lambda i: (i, 0, 0))],
        out_shape=[jax.ShapeDtypeStruct((t, MLA_HEADS * LANE), BF16),
                   jax.ShapeDtypeStruct((t, MLA_HEADS * LANE), BF16),
                   jax.ShapeDtypeStruct((t // tm, MLA_HEADS * V_ROWS, tm), BF16)],
        compiler_params=_cparams("parallel"),
        name="mla_prep",
    )(a, qn, kvn, wq, wk, wv, qg, kg, *tabs)


def _diff_prep_kernel(cin_ref, qg_ref, kg_ref, bd_ref, c_ref, sa_ref, sb_ref, q_ref, k_ref, vt_ref):
    cin = cin_ref[...]
    bd = bd_ref[...]
    c = jnp.concatenate([c_ref[...], c_ref[...]], axis=1)
    sa = jnp.concatenate([sa_ref[...], sa_ref[...]], axis=1)
    sb = jnp.concatenate([sb_ref[...], sb_ref[...]], axis=1)
    lane = lax.broadcasted_iota(I32, (1, LANE), 1)

    def norm_rope(x, g):
        ms = _dot_split_lhs(x * x, bd, 3) / DIFF_QK
        x = x * lax.rsqrt(ms + EPS) * g
        halves = []
        for j in range(2):
            sl = slice(j * LANE, (j + 1) * LANE)
            halves.append(_rope(x[:, sl], c[:, sl], sa[:, sl], sb[:, sl], DIFF_QK // 2))
        return halves

    qh = norm_rope(cin[:, :256], qg_ref[...])
    kh = norm_rope(cin[:, 256:512], kg_ref[...])
    scale = DIFF_QK ** -0.5 * LOG2E
    for j in range(2 * DIFF_HEADS):
        grp, sub = divmod(j, LANE // DIFF_QK)
        keep = (lane >= sub * DIFF_QK) & (lane < (sub + 1) * DIFF_QK)
        q_ref[:, j * LANE:(j + 1) * LANE] = jnp.where(keep, qh[grp] * scale, 0.0).astype(BF16)
    k_ref[...] = jnp.concatenate(kh, axis=1).astype(BF16)
    vt = cin[:, 512:].T.astype(BF16)
    ones = jnp.ones((V_ROWS - DIFF_V, vt.shape[1]), BF16)
    for h in range(DIFF_HEADS):
        vt_ref[0, h * V_ROWS:(h + 1) * V_ROWS, :] = jnp.concatenate([vt[h * DIFF_V:(h + 1) * DIFF_V], ones], axis=0)


def _diff_prep(cin, qg, kg, bd, tabs, seq):
    t = cin.shape[0]
    tm = TOKEN_TILE
    nt = seq // tm
    full = lambda r, c: pl.BlockSpec((r, c), lambda i: (0, 0))
    tab = pl.BlockSpec((tm, LANE), lambda i: (i % nt, 0))
    nq = 2 * DIFF_HEADS * LANE
    return pl.pallas_call(
        _diff_prep_kernel,
        grid=(t // tm,),
        in_specs=[pl.BlockSpec((tm, DIFF_IN), lambda i: (i, 0)),
                  full(1, 256), full(1, 256), full(256, 256), tab, tab, tab],
        out_specs=[pl.BlockSpec((tm, nq), lambda i: (i, 0)),
                   pl.BlockSpec((tm, 256), lambda i: (i, 0)),
                   pl.BlockSpec((1, DIFF_HEADS * V_ROWS, tm), lambda i: (i, 0, 0))],
        out_shape=[jax.ShapeDtypeStruct((t, nq), BF16),
                   jax.ShapeDtypeStruct((t, 256), BF16),
                   jax.ShapeDtypeStruct((t // tm, DIFF_HEADS * V_ROWS, tm), BF16)],
        compiler_params=_cparams("parallel"),
        name="diff_prep",
    )(cin, qg, kg, bd, *tabs)


def _attn_noshift_kernel(q_ref, k_ref, vt_ref, o_ref, acc_scr, l_scr, pa_scr, pb_scr, *, nkc, tk, dv):
    q = q_ref[0]

    def probs(kc, p_ref):
        k = k_ref[0, pl.ds(pl.multiple_of(kc * tk, tk), tk), :]
        p = jnp.exp2(_dot_nt(k, q))
        p_ref[...] = p.astype(BF16)
        return jnp.sum(p, axis=0, keepdims=True)

    def pv(kc, p_ref):
        return _dot(vt_ref[0, kc, :dv, :], p_ref[...])

    bufs = (pa_scr, pb_scr)
    unroll = ATTN_GROUP

    def run(first, count, last):
        tot = None
        lsum = None
        for j in range(count):
            if not (last and j == count - 1):
                ls = probs(first + j + 1, bufs[(j + 1) % 2])
                lsum = ls if lsum is None else lsum + ls
            d = pv(first + j, bufs[j % 2])
            tot = d if tot is None else tot + d
        return tot, lsum

    l_scr[...] = probs(0, pa_scr)
    acc_scr[...] = jnp.zeros_like(acc_scr)
    trips = (nkc - 1) // unroll

    def body(i, carry):
        tot, lsum = run(i * unroll, unroll, False)
        acc_scr[...] += tot
        l_scr[...] += lsum
        return carry

    lax.fori_loop(0, trips, body, 0)
    tot, lsum = run(trips * unroll, nkc - trips * unroll, True)
    den = l_scr[...] if lsum is None else l_scr[...] + lsum
    o_ref[0] = ((acc_scr[...] + tot) / den).astype(o_ref.dtype)


def _attn_online_kernel(q_ref, k_ref, vt_ref, o_ref, m_scr, acc_scr, *, nkc, tk, dv):
    q = q_ref[0]
    m_scr[...] = jnp.full_like(m_scr, -jnp.inf)
    acc_scr[...] = jnp.zeros_like(acc_scr)

    def body(kc, carry):
        k = k_ref[0, pl.ds(pl.multiple_of(kc * tk, tk), tk), :]
        s = _dot_nt(k, q)
        m_prev = m_scr[...]
        m_new = jnp.maximum(m_prev, jnp.max(s, axis=0, keepdims=True))
        p = jnp.exp2(s - m_new).astype(BF16)
        acc_scr[...] = jnp.exp2(m_prev - m_new) * acc_scr[...] + _dot(vt_ref[0, kc], p)
        m_scr[...] = m_new
        return carry

    lax.fori_loop(0, nkc, body, 0)
    acc = acc_scr[...]
    o_ref[0] = (acc[:dv] / acc[dv:dv + 1]).astype(o_ref.dtype)


def _attention(q, k, vt, heads, kmap, vmap, dv, tk, score_bound, out_dtype):
    b, s, _ = q.shape
    tq = min(ATTN_TQ, s)
    nkc = s // tk

    def call(body, scratch, name):
        return pl.pallas_call(
            functools.partial(body, nkc=nkc, tk=tk, dv=dv),
            grid=(b, heads, s // tq),
            in_specs=[pl.BlockSpec((1, tq, LANE), lambda bi, h, qi: (bi, qi, h)),
                      pl.BlockSpec((1, s, LANE), lambda bi, h, qi: (bi, 0, kmap(h))),
                      pl.BlockSpec((1, nkc, V_ROWS, tk), lambda bi, h, qi: (bi, 0, vmap(h), 0))],
            out_specs=pl.BlockSpec((1, dv, tq), lambda bi, h, qi: (bi, h, qi)),
            out_shape=jax.ShapeDtypeStruct((b, heads * dv, s), out_dtype),
            scratch_shapes=scratch,
            compiler_params=_cparams("parallel", "parallel", "arbitrary"),
            name=name,
        )

    acc = pltpu.VMEM((V_ROWS, tq), F32)
    pbuf = pltpu.VMEM((tk, tq), BF16)
    fast = call(_attn_noshift_kernel, [pltpu.VMEM((dv, tq), F32), pltpu.VMEM((1, tq), F32), pbuf, pbuf],
                "attention")
    safe = call(_attn_online_kernel, [pltpu.VMEM((1, tq), F32), acc], "attention_online")
    return lax.cond(score_bound < NOSHIFT_MAX_LOG2, fast, safe, q, k, vt)


def _ssd_kernel(*refs, rev, nct, t):
    if rev:
        (main_ref, prev_ref, next_ref, yf_ref, cw_ref, cb_ref, dtb_ref, alog_ref, dsk_ref, nw_ref,
         out_ref, h_scr) = refs
    else:
        (main_ref, prev_ref, next_ref, cw_ref, cb_ref, dtb_ref, alog_ref, out_ref, h_scr) = refs
    i = pl.program_id(1)
    c = (nct - 1 - i) if rev else i

    @pl.when(i == 0)
    def _():
        h_scr[...] = jnp.zeros_like(h_scr)

    main = main_ref[...]
    xlo, xhi = SSD_INNER, SSD_INNER + SSD_CONV_DIM
    prev = jnp.where(c > 0, prev_ref[:, xlo:xhi], 0.0)
    nxt = jnp.where(c < nct - 1, next_ref[:, xlo:xhi], 0.0)
    xp = jnp.concatenate([prev, main[:, xlo:xhi], nxt], axis=0)
    acc = jnp.broadcast_to(cb_ref[...], (t, SSD_CONV_DIM))
    for j in range(SSD_CONV):
        sh = (SSD_CONV // 2 - j) % (t + 2 * SUBLANE)
        r = xp if sh == 0 else pltpu.roll(xp, sh, 0)
        acc = acc + r[SUBLANE:SUBLANE + t] * cw_ref[j:j + 1, :]
    xc = _silu(acc)
    xs = xc[:, :SSD_INNER]
    gn = SSD_GROUPS * SSD_STATE
    bmat = xc[:, SSD_INNER:SSD_INNER + gn]
    cmat = xc[:, SSD_INNER + gn:]

    dtr = main[:, xhi:] + dtb_ref[...]
    dt = jnp.maximum(dtr, 0.0) + jnp.log1p(jnp.exp(-jnp.abs(dtr)))
    a = dt * (-jnp.exp(alog_ref[...]))

    row = lax.broadcasted_iota(I32, (t, t), 0)
    col = lax.broadcasted_iota(I32, (t, t), 1)
    lower = row >= col
    tri_l = jnp.where(lower, 1.0, 0.0).astype(BF16)
    tri_u = jnp.where(row <= col, 1.0, 0.0).astype(BF16)
    a_t = a.T
    cs = _dot_split_rhs(tri_l, a, 3)
    cs_t = _dot_split_lhs(a_t, tri_u, 3)
    tot = cs[t - 1:t, :]
    if rev:
        ecol = cs - a
        erow = cs_t - a_t
    lane = lax.broadcasted_iota(I32, (1, LANE), 1)
    hb = SSD_HEADS if rev else 0
    hpg = SSD_HEADS // SSD_GROUPS

    g = []
    bt = []
    for grp in range(SSD_GROUPS):
        bg = bmat[:, grp * SSD_STATE:(grp + 1) * SSD_STATE]
        cg = cmat[:, grp * SSD_STATE:(grp + 1) * SSD_STATE].astype(BF16)
        g.append((_dot_nt(cg, bg.astype(BF16)), cg))
        bt.append(bg.T.astype(BF16))

    ys = []
    for pair in range(SSD_HEADS // 2):
        xpair = xs[:, pair * LANE:(pair + 1) * LANE]
        hstate = h_scr[pair]
        hbf = hstate.astype(BF16)
        y = jnp.zeros((t, LANE), F32)
        hnew = jnp.zeros((SSD_STATE, LANE), F32)
        dec = jnp.zeros((1, LANE), F32)
        for sub in range(2):
            h = pair * 2 + sub
            grp = h // hpg
            gmat, cg = g[grp]
            keep = (lane >= sub * SSD_HEAD_DIM) & (lane < (sub + 1) * SSD_HEAD_DIM)
            hl = hb + h
            dtc = dt[:, hl:hl + 1]
            xm = jnp.where(keep, xpair * dtc, 0.0)
            if rev:
                dmat = jnp.exp(jnp.where(row <= col, erow[hl:hl + 1, :] - ecol[:, hl:hl + 1], -jnp.inf))
                off_scale = jnp.exp(tot[:, hl:hl + 1] - ecol[:, hl:hl + 1])
                st_w = jnp.exp(ecol[:, hl:hl + 1])
            else:
                dmat = jnp.exp(jnp.where(lower, cs[:, hl:hl + 1] - cs_t[hl:hl + 1, :], -jnp.inf))
                off_scale = jnp.exp(cs[:, hl:hl + 1])
                st_w = jnp.exp(tot[:, hl:hl + 1] - cs[:, hl:hl + 1])
            y = y + _dot((gmat * dmat).astype(BF16), xm.astype(BF16))
            y = y + jnp.where(keep, _dot(cg, hbf) * off_scale, 0.0)
            hnew = hnew + _dot(bt[grp], (xm * st_w).astype(BF16))
            dec = dec + jnp.where(keep, jnp.exp(tot[:, hl:hl + 1]), 0.0)
        h_scr[pair] = hstate * dec + hnew
        ys.append(y)

    yall = jnp.concatenate(ys, axis=1)
    if rev:
        yall = yall + yf_ref[...] + xs * dsk_ref[...]
        gt = yall * _silu(main[:, :SSD_INNER])
        yall = gt * lax.rsqrt(jnp.mean(gt * gt, axis=-1, keepdims=True) + EPS) * nw_ref[...]
    out_ref[...] = yall.astype(out_ref.dtype)


def _ssd_pass(bin_, yf, cw, cb, dtb, alog, dsk, nw, batch, seq, rev):
    t = SSD_T
    nct = seq // t
    hb = t // SUBLANE
    nrow8 = batch * seq // SUBLANE

    def cidx(i):
        return (nct - 1 - i) if rev else i

    main = pl.BlockSpec((t, SSD_IN), lambda b, i: (b * nct + cidx(i), 0))
    prev = pl.BlockSpec((SUBLANE, SSD_IN),
                        lambda b, i: (jnp.maximum((b * nct + cidx(i)) * hb - 1, 0), 0))
    nxt = pl.BlockSpec((SUBLANE, SSD_IN),
                       lambda b, i: (jnp.minimum((b * nct + cidx(i) + 1) * hb, nrow8 - 1), 0))
    full = lambda r, c: pl.BlockSpec((r, c), lambda b, i: (0, 0))
    yspec = pl.BlockSpec((t, SSD_INNER), lambda b, i: (b * nct + cidx(i), 0))
    if rev:
        in_specs = [main, prev, nxt, yspec, full(SSD_CONV, SSD_CONV_DIM), full(1, SSD_CONV_DIM),
                    full(1, LANE), full(1, LANE), full(1, SSD_INNER), full(1, SSD_INNER)]
        args = (bin_, bin_, bin_, yf, cw, cb, dtb, alog, dsk, nw)
    else:
        in_specs = [main, prev, nxt, full(SSD_CONV, SSD_CONV_DIM), full(1, SSD_CONV_DIM),
                    full(1, LANE), full(1, LANE)]
        args = (bin_, bin_, bin_, cw, cb, dtb, alog)
    return pl.pallas_call(
        functools.partial(_ssd_kernel, rev=rev, nct=nct, t=t),
        grid=(batch, nct),
        in_specs=in_specs,
        out_specs=yspec,
        out_shape=jax.ShapeDtypeStruct((batch * seq, SSD_INNER), BF16 if rev else F32),
        scratch_shapes=[pltpu.VMEM((SSD_HEADS // 2, SSD_STATE, LANE), F32)],
        compiler_params=_cparams("parallel", "arbitrary"),
        name="ssd_bwd" if rev else "ssd_fwd",
    )(*args)


def _outproj_kernel(x_ref, mla_ref, ssd_ref, dif_ref, lam_ref, sub_ref, wo_ref, fn_ref, wrh_ref,
                    xo_ref, hn_ref, aff_ref, *, lambda_init):
    lam = lam_ref[...]
    lam_full = (jnp.exp(jnp.sum(lam[0:1] * lam[1:2], keepdims=True))
                - jnp.exp(jnp.sum(lam[2:3] * lam[3:4], keepdims=True)) + lambda_init)
    dif = dif_ref[0]
    outs = []
    for hd in range(DIFF_HEADS):
        o = dif[2 * hd * DIFF_V:(2 * hd + 1) * DIFF_V] - lam_full * dif[(2 * hd + 1) * DIFF_V:(2 * hd + 2) * DIFF_V]
        o = o * lax.rsqrt(jnp.mean(o * o, axis=0, keepdims=True) + EPS) * sub_ref[...] * (1.0 - lambda_init)
        outs.append(o)
    dt_ = jnp.concatenate(outs, axis=0).astype(BF16)
    wo = wo_ref[...]
    x = x_ref[...]
    x = x + _dot_tn(mla_ref[0].astype(BF16), wo[:MLA_WIDTH])
    x = x + _dot(ssd_ref[...].astype(BF16), wo[MLA_WIDTH:MLA_WIDTH + SSD_INNER])
    x = x + _dot_tn(dt_, wo[MLA_WIDTH + SSD_INNER:])
    xo_ref[...] = x
    hn = x * lax.rsqrt(jnp.mean(x * x, axis=-1, keepdims=True) + EPS) * fn_ref[...]
    hn_ref[...] = hn
    hh, hl = _split_bf16(hn, 2)
    both = _dot(hh, wrh_ref[...])
    logits = both[:, :LANE] + both[:, LANE:] + _dot(hl, wrh_ref[:, :LANE])
    lane = lax.broadcasted_iota(I32, logits.shape, 1)
    logits = jnp.where(lane < N_EXPERTS, logits, -jnp.inf)
    e = jnp.exp(logits - jnp.max(logits, axis=-1, keepdims=True))
    aff = e / jnp.sum(e, axis=-1, keepdims=True)
    aff_ref[...] = aff.T[:N_EXPERTS]


def _outproj(x2, mla_t, ssd, dif_t, lam, sub, wo, fn, wr2, seq, lambda_init):
    t = x2.shape[0]
    tm = TOKEN_TILE
    nt = seq // tm
    full = lambda r, c: pl.BlockSpec((r, c), lambda i: (0, 0))
    row = lambda c: pl.BlockSpec((tm, c), lambda i: (i, 0))
    return pl.pallas_call(
        functools.partial(_outproj_kernel, lambda_init=lambda_init),
        grid=(t // tm,),
        in_specs=[row(D_MODEL),
                  pl.BlockSpec((1, MLA_WIDTH, tm), lambda i: (i // nt, 0, i % nt)),
                  row(SSD_INNER),
                  pl.BlockSpec((1, 2 * DIFF_WIDTH, tm), lambda i: (i // nt, 0, i % nt)),
                  full(4, DIFF_QK), full(DIFF_V, 1), full(D_MODEL, D_MODEL), full(1, D_MODEL),
                  full(D_MODEL, 2 * LANE)],
        out_specs=[row(D_MODEL), row(D_MODEL), pl.BlockSpec((N_EXPERTS, tm), lambda i: (0, i))],
        out_shape=[jax.ShapeDtypeStruct((t, D_MODEL), F32),
                   jax.ShapeDtypeStruct((t, D_MODEL), F32),
                   jax.ShapeDtypeStruct((N_EXPERTS, t), F32)],
        compiler_params=_cparams("parallel"),
        name="outproj",
    )(x2, mla_t, ssd, dif_t, lam, sub, wo, fn, wr2)


def _cumsum_rowmajor(mask_f32, tri_u, tri_ls):
    local = _dot(mask_f32.astype(BF16), tri_u)
    rowtot = jnp.broadcast_to(local[:, LANE - 1:LANE], local.shape).astype(BF16)
    return local + _dot(tri_ls, rowtot)


def _select_kernel(aff_ref, pos_ref, idx_ref, toff_ref, tcnt_ref, cum_scr, *, cap, sb, rows):
    a = aff_ref[0]
    key = lax.bitcast_convert_type(a, I32)

    def bit_body(it, prefix):
        cand = prefix | lax.shift_left(jnp.int32(1), 30 - it)
        cnt = jnp.sum((key >= cand).astype(I32))
        return jnp.where(cnt >= cap, cand, prefix)

    thr = lax.fori_loop(0, 31, bit_body, jnp.int32(0))
    gt = key > thr
    eq = key == thr
    need = cap - jnp.sum(gt.astype(I32))
    r_i = lax.broadcasted_iota(I32, (LANE, LANE), 0)
    c_i = lax.broadcasted_iota(I32, (LANE, LANE), 1)
    tri_u = jnp.where(r_i <= c_i, 1.0, 0.0).astype(BF16)
    r_r = lax.broadcasted_iota(I32, (rows, rows), 0)
    c_r = lax.broadcasted_iota(I32, (rows, rows), 1)
    tri_ls = jnp.where(r_r > c_r, 1.0, 0.0).astype(BF16)
    eqf = jnp.where(eq, 1.0, 0.0)
    tie_rank = _cumsum_rowmajor(eqf, tri_u, tri_ls) - eqf
    sel = gt | (eq & (tie_rank < need.astype(F32)))
    self_ = jnp.where(sel, 1.0, 0.0)
    cum = _cumsum_rowmajor(self_, tri_u, tri_ls).astype(I32)
    pos_ref[0] = jnp.where(sel, cum - 1, -1)
    cum_scr[...] = cum
    rowend = cum[:, LANE - 1:LANE]
    rowstart = rowend - jnp.sum(self_, axis=-1, keepdims=True).astype(I32)
    toff_ref[0] = rowstart
    tcnt_ref[0] = rowend - rowstart

    def blk(bi, carry):
        p0 = bi * sb
        r_lo = jnp.sum((rowend <= p0).astype(I32))
        r_hi = jnp.sum((rowstart < p0 + sb).astype(I32))
        s_col = p0 + lax.broadcasted_iota(I32, (sb, LANE), 0)

        def rowbody(r, acc):
            return acc + (cum_scr[pl.ds(r, 1), :] <= s_col).astype(I32)

        acc = lax.fori_loop(r_lo, r_hi, rowbody, jnp.zeros((sb, LANE), I32))
        idx_ref[0, pl.ds(pl.multiple_of(p0, sb), sb), :] = jnp.sum(acc, axis=1, keepdims=True) + r_lo * LANE
        return carry

    lax.fori_loop(0, cap // sb, blk, 0)


def _select(aff3, cap):
    e, rows, _ = aff3.shape
    sb = min(IDX_SLOTS, cap)
    return pl.pallas_call(
        functools.partial(_select_kernel, cap=cap, sb=sb, rows=rows),
        grid=(e,),
        in_specs=[pl.BlockSpec((1, rows, LANE), lambda i: (i, 0, 0))],
        out_specs=[pl.BlockSpec((1, rows, LANE), lambda i: (i, 0, 0)),
                   pl.BlockSpec((1, cap, 1), lambda i: (i, 0, 0)),
                   pl.BlockSpec((1, rows, 1), lambda i: (i, 0, 0)),
                   pl.BlockSpec((1, rows, 1), lambda i: (i, 0, 0))],
        out_shape=[jax.ShapeDtypeStruct((e, rows, LANE), I32),
                   jax.ShapeDtypeStruct((e, cap, 1), I32),
                   jax.ShapeDtypeStruct((e, rows, 1), I32),
                   jax.ShapeDtypeStruct((e, rows, 1), I32)],
        scratch_shapes=[pltpu.VMEM((rows, LANE), I32)],
        compiler_params=_cparams("parallel"),
        name="select",
    )(aff3)


def _row_copy(src_hbm, src_row, dst_vmem, dst_row, sem):
    return pltpu.make_async_copy(src_hbm.at[pl.ds(src_row, 1)], dst_vmem.at[pl.ds(dst_row, 1)], sem)


def _ffn_kernel(idx_ref, nxt_ref, hn_hbm, wg_ref, wu_ref, wd_ref, ye_ref, xbuf, sem, *, slots, steps):
    n = pl.program_id(0) * pl.num_programs(1) + pl.program_id(1)
    cur = n % 2

    def gather(iref, buf_slot):
        for r in range(slots):
            _row_copy(hn_hbm, iref[0, 0, r], xbuf.at[buf_slot], r, sem.at[buf_slot]).start()

    def wait_all(buf_slot):
        pltpu.make_async_copy(hn_hbm.at[pl.ds(0, slots)], xbuf.at[buf_slot], sem.at[buf_slot]).wait()

    @pl.when(n == 0)
    def _():
        gather(idx_ref, 0)

    gather(nxt_ref, 1 - cur)
    wait_all(cur)
    x = xbuf[cur].astype(BF16)
    gate = _dot(x, wg_ref[0])
    up = _dot(x, wu_ref[0])
    hid = (_silu(gate) * up).astype(BF16)
    ye_ref[...] = _dot(hid, wd_ref[0]).astype(BF16)

    @pl.when(n == steps - 1)
    def _():
        wait_all(1 - cur)


def _ffn(idx3, hn, wg, wu, wd, cap):
    slots = idx3.shape[-1]
    nb = cap // slots
    steps = N_EXPERTS * nb
    wspec = lambda: pl.BlockSpec((1, D_MODEL, EXPERT_FF), lambda e, b: (e, 0, 0))
    ispec = lambda f: pl.BlockSpec((1, 1, slots), f, memory_space=pltpu.SMEM)
    return pl.pallas_call(
        functools.partial(_ffn_kernel, slots=slots, steps=steps),
        grid=(N_EXPERTS, nb),
        in_specs=[ispec(lambda e, b: (e * nb + b, 0, 0)),
                  ispec(lambda e, b: (jnp.minimum(e * nb + b + 1, steps - 1), 0, 0)),
                  pl.BlockSpec(memory_space=pl.ANY),
                  wspec(), wspec(),
                  pl.BlockSpec((1, EXPERT_FF, D_MODEL), lambda e, b: (e, 0, 0))],
        out_specs=pl.BlockSpec((slots, D_MODEL), lambda e, b: (e * nb + b, 0)),
        out_shape=jax.ShapeDtypeStruct((N_EXPERTS * cap, D_MODEL), BF16),
        scratch_shapes=[pltpu.VMEM((2, slots, D_MODEL), F32), pltpu.SemaphoreType.DMA((2,))],
        compiler_params=_cparams("arbitrary", "arbitrary"),
        name="expert_ffn",
    )(idx3, idx3, hn, wg, wu, wd)


YE_SHIFT = 4
YE_BLOCK = 1 << YE_SHIFT
COMBINE_MAX_BLOCKS = N_EXPERTS * (COMBINE_TILE // YE_BLOCK + 1)
COMBINE_CHUNK = 512
COMBINE_MAX_ROWS = -(-COMBINE_MAX_BLOCKS * YE_BLOCK // COMBINE_CHUNK) * COMBINE_CHUNK


def _combine_kernel(toff_ref, tcnt_ref, x_ref, pos_ref, aff_ref, ye_hbm, out_ref, buf, sem, *, cap, rows, chunk):
    r = pl.program_id(0)
    cur = r % 2

    def layout(tile):
        firsts, nblks, dsts = [], [], []
        dst = jnp.int32(0)
        for e in range(N_EXPERTS):
            off = toff_ref[e * rows + tile]
            cnt = tcnt_ref[e * rows + tile]
            first = (off >> YE_SHIFT) << YE_SHIFT
            firsts.append(first)
            nblks.append(jnp.where(cnt > 0, (off + cnt - first + YE_BLOCK - 1) >> YE_SHIFT, 0))
            dsts.append(dst)
            dst = dst + nblks[-1]
        return firsts, nblks, dsts, dst

    def gather(tile, buf_slot):
        firsts, nblks, dsts, _ = layout(tile)
        for e in range(N_EXPERTS):
            def issue(j, carry, e=e):
                src = pl.multiple_of(e * cap + firsts[e] + j * YE_BLOCK, YE_BLOCK)
                dst = pl.multiple_of((dsts[e] + j) * YE_BLOCK, YE_BLOCK)
                pltpu.make_async_copy(ye_hbm.at[pl.ds(src, YE_BLOCK)], buf.at[buf_slot, pl.ds(dst, YE_BLOCK)],
                                      sem.at[buf_slot]).start()
                return carry

            lax.fori_loop(0, nblks[e], issue, 0)

    @pl.when(r == 0)
    def _():
        gather(0, 0)

    @pl.when(r + 1 < rows)
    def _():
        gather(r + 1, 1 - cur)

    firsts, _, dsts, nblk_total = layout(r)
    total = nblk_total * YE_BLOCK
    for bit in range(COMBINE_MAX_BLOCKS.bit_length()):
        @pl.when(((nblk_total >> bit) & 1) == 1)
        def _(bit=bit):
            n = YE_BLOCK << bit
            pltpu.make_async_copy(ye_hbm.at[pl.ds(0, n)], buf.at[cur, pl.ds(0, n)], sem.at[cur]).wait()

    out_ref[...] = x_ref[...]
    pos = pos_ref[...]
    aff = aff_ref[...]
    tgt = [jnp.where(pos[e:e + 1] >= 0, pos[e:e + 1] - firsts[e] + dsts[e] * YE_BLOCK, -1)
           for e in range(N_EXPERTS)]

    def chunk_body(k, carry):
        base = pl.multiple_of(k * chunk, chunk)
        cidx = base + lax.broadcasted_iota(I32, (chunk, COMBINE_TILE), 0)
        wt = jnp.zeros((chunk, COMBINE_TILE), F32)
        for e in range(N_EXPERTS):
            wt = wt + jnp.where(cidx == tgt[e], aff[e:e + 1], 0.0)
        rowid = base + lax.broadcasted_iota(I32, (chunk, 1), 0)
        rowsv = jnp.where(rowid < total, buf[cur, pl.ds(base, chunk), :], jnp.zeros((), BF16))
        wh, wl = _split_bf16(wt, 2)
        out_ref[...] += _dot_tn(wh, rowsv) + _dot_tn(wl, rowsv)
        return carry

    lax.fori_loop(0, (total + chunk - 1) // chunk, chunk_body, 0)


def _combine(toff, tcnt, x2, pos, aff_t, ye, cap):
    t = x2.shape[0]
    tile = COMBINE_TILE
    rows = t // tile
    chunk = COMBINE_CHUNK
    grid_spec = pltpu.PrefetchScalarGridSpec(
        num_scalar_prefetch=2,
        grid=(rows,),
        in_specs=[pl.BlockSpec((tile, D_MODEL), lambda i, *_: (i, 0)),
                  pl.BlockSpec((N_EXPERTS, tile), lambda i, *_: (0, i)),
                  pl.BlockSpec((N_EXPERTS, tile), lambda i, *_: (0, i)),
                  pl.BlockSpec(memory_space=pl.ANY)],
        out_specs=pl.BlockSpec((tile, D_MODEL), lambda i, *_: (i, 0)),
        scratch_shapes=[pltpu.VMEM((2, COMBINE_MAX_ROWS, D_MODEL), BF16), pltpu.SemaphoreType.DMA((2,))],
    )
    return pl.pallas_call(
        functools.partial(_combine_kernel, cap=cap, rows=rows, chunk=chunk),
        grid_spec=grid_spec,
        out_shape=jax.ShapeDtypeStruct((t, D_MODEL), F32),
        compiler_params=_cparams("arbitrary"),
        name="combine",
    )(toff, tcnt, x2, pos, aff_t, ye)


def _rope_tables(seq):
    inv = ROPE_THETA ** (-jnp.arange(0, MLA_ROPE, 2, dtype=F32) / MLA_ROPE)
    ang = jnp.arange(seq, dtype=F32)[:, None] * inv[None, :]
    cos, sin = jnp.cos(ang), jnp.sin(ang)
    one = jnp.ones((seq, 1), F32)
    zero = jnp.zeros((seq, 1), F32)
    rep = lambda v, n: jnp.broadcast_to(v, (seq, n))
    half = MLA_ROPE // 2
    mla = (jnp.concatenate([rep(one, MLA_NOPE), cos, cos, rep(one, LANE - MLA_QK)], axis=1),
           jnp.concatenate([rep(zero, MLA_NOPE + half), sin, rep(zero, LANE - MLA_QK)], axis=1),
           jnp.concatenate([rep(zero, MLA_NOPE), -sin, rep(zero, LANE - MLA_NOPE - half)], axis=1))
    nblk = LANE // DIFF_QK
    dif = (jnp.tile(jnp.concatenate([cos, cos], axis=1), (1, nblk)),
           jnp.tile(jnp.concatenate([rep(zero, half), sin], axis=1), (1, nblk)),
           jnp.tile(jnp.concatenate([-sin, rep(zero, half)], axis=1), (1, nblk)))
    return mla, dif


def _pad_cols(w, width):
    return jnp.pad(w, ((0, 0), (0, width - w.shape[1])))


def _layer_params(p, l):
    w_in = p['w_in'][l]
    z64 = jnp.zeros((D_MODEL, 64), F32)
    z32 = jnp.zeros((D_MODEL, 32), F32)
    w_proj = jnp.concatenate([
        w_in[:, :OFF_KPE], z64, w_in[:, OFF_KPE:OFF_Z], z32,
        w_in[:, OFF_Z:OFF_DT], _pad_cols(w_in[:, OFF_DT:OFF_DQ], LANE),
        w_in[:, OFF_DQ:]], axis=1).astype(BF16)
    wq = p['mla_w_uq'][l].reshape(MLA_Q_LORA, MLA_HEADS, MLA_QK)
    wq = jnp.pad(wq, ((0, 0), (0, 0), (0, LANE - MLA_QK))).reshape(MLA_Q_LORA, MLA_HEADS * LANE)
    wkv = p['mla_w_ukv'][l].reshape(MLA_KV_LORA, MLA_HEADS, MLA_NOPE + MLA_V)
    wk = jnp.pad(wkv[:, :, :MLA_NOPE], ((0, 0), (0, 0), (0, LANE - MLA_NOPE))).reshape(MLA_KV_LORA, MLA_HEADS * LANE)
    wv = wkv[:, :, MLA_NOPE:].reshape(MLA_KV_LORA, MLA_WIDTH)
    wr = _pad_cols(p['w_router'][l], LANE)
    wrh = wr.astype(BF16)
    w_router2 = jnp.concatenate([wrh, (wr - wrh.astype(F32)).astype(BF16)], axis=1)
    pad_row = lambda v, n: jnp.pad(v, (0, n - v.shape[0]))[None, :]
    return dict(
        attn_norm=p['attn_norm'][l][None, :], w_proj=w_proj,
        q_norm=p['mla_q_norm'][l][None, :], kv_norm=p['mla_kv_norm'][l][None, :],
        wq=wq.astype(BF16), wk=wk.astype(BF16), wv=wv.astype(BF16),
        q_gain=pad_row(p['mla_q_gain'][l], LANE), k_gain=pad_row(p['mla_k_gain'][l], LANE),
        conv_w=p['ssd_conv_w'][l], conv_b=p['ssd_conv_b'][l][None, :],
        dt_bias=pad_row(p['ssd_dt_bias'][l].reshape(-1), LANE),
        a_log=pad_row(p['ssd_a_log'][l].reshape(-1), LANE),
        d_skip=jnp.repeat(p['ssd_d'][l], SSD_HEAD_DIM)[None, :], ssd_norm=p['ssd_norm'][l][None, :],
        dq_gain=jnp.tile(p['diff_q_gain'][l], 2 * DIFF_HEADS)[None, :],
        dk_gain=jnp.tile(p['diff_k_gain'][l], 2 * DIFF_HEADS)[None, :],
        lam=p['diff_lambda'][l], subln=p['diff_subln'][l][:, None],
        w_out=p['w_out'][l].astype(BF16), ffn_norm=p['ffn_norm'][l][None, :],
        w_router2=w_router2,
        w_gate=p['w_gate'][l].astype(BF16), w_up=p['w_up'][l].astype(BF16), w_down=p['w_down'][l].astype(BF16),
    )


def _block_diag_ones(n, blk):
    i = jnp.arange(n)
    return (i[:, None] // blk == i[None, :] // blk).astype(BF16)


def _trunk(x, p):
    batch, seq, _ = x.shape
    t = batch * seq
    cap = EC_CAPACITY * t // N_EXPERTS
    tk = min(ATTN_TK, seq)
    mla_tabs, dif_tabs = _rope_tables(seq)
    bd = _block_diag_ones(2 * DIFF_HEADS * DIFF_QK, DIFF_QK)
    x2 = x.reshape(t, D_MODEL)
    rows = t // LANE
    for l in range(DEPTH):
        lp = _layer_params(p, l)
        a_in, b_in, c_in = _inproj(x2, lp['attn_norm'], lp['w_proj'])
        q, k, vt = _mla_prep(a_in, lp['q_norm'], lp['kv_norm'], lp['wq'], lp['wk'], lp['wv'],
                             lp['q_gain'], lp['k_gain'], mla_tabs, seq)
        mla_bound = (MLA_QK ** 0.5 * LOG2E) * jnp.max(jnp.abs(p['mla_q_gain'][l])) * jnp.max(jnp.abs(p['mla_k_gain'][l]))
        mla_t = _attention(q.reshape(batch, seq, -1), k.reshape(batch, seq, -1),
                           vt.reshape(batch, seq // tk, MLA_HEADS * V_ROWS, tk),
                           MLA_HEADS, lambda h: h, lambda h: h, MLA_V, tk, mla_bound, BF16)
        dq, dk, dvt = _diff_prep(c_in, lp['dq_gain'], lp['dk_gain'], bd, dif_tabs, seq)
        dif_bound = (DIFF_QK ** 0.5 * LOG2E) * jnp.max(jnp.abs(p['diff_q_gain'][l])) * jnp.max(jnp.abs(p['diff_k_gain'][l]))
        dif_t = _attention(dq.reshape(batch, seq, -1), dk.reshape(batch, seq, -1),
                           dvt.reshape(batch, seq // tk, DIFF_HEADS * V_ROWS, tk),
                           2 * DIFF_HEADS, lambda h: h // (LANE // DIFF_QK), lambda h: h // 2, DIFF_V, tk,
                           dif_bound, F32)
        yf = _ssd_pass(b_in, None, lp['conv_w'], lp['conv_b'], lp['dt_bias'], lp['a_log'],
                       None, None, batch, seq, rev=False)
        ssd = _ssd_pass(b_in, yf, lp['conv_w'], lp['conv_b'], lp['dt_bias'], lp['a_log'],
                        lp['d_skip'], lp['ssd_norm'], batch, seq, rev=True)
        lambda_init = 0.8 - 0.6 * math.exp(-0.3 * l)
        x2, hn, aff_t = _outproj(x2, mla_t, ssd, dif_t, lp['lam'], lp['subln'], lp['w_out'],
                                 lp['ffn_norm'], lp['w_router2'], seq, lambda_init)
        pos3, idx, toff, tcnt = _select(aff_t.reshape(N_EXPERTS, rows, LANE), cap)
        slots = min(FFN_SLOTS, cap)
        ye = _ffn(idx.reshape(N_EXPERTS * cap // slots, 1, slots), hn,
                  lp['w_gate'], lp['w_up'], lp['w_down'], cap)
        x2 = _combine(toff.reshape(-1), tcnt.reshape(-1), x2, pos3.reshape(N_EXPERTS, t), aff_t, ye, cap)
    return x2.reshape(batch, seq, D_MODEL)


def kernel(x_prompt, x_sample, attn_norm, w_in, mla_q_norm, mla_kv_norm, mla_w_uq, mla_w_ukv, mla_q_gain,
           mla_k_gain, ssd_conv_w, ssd_conv_b, ssd_dt_bias, ssd_a_log, ssd_d, ssd_norm, diff_q_gain,
           diff_k_gain, diff_lambda, diff_subln, w_out, ffn_norm, w_router, w_gate, w_up, w_down):
    p = dict(attn_norm=attn_norm, w_in=w_in, mla_q_norm=mla_q_norm, mla_kv_norm=mla_kv_norm,
             mla_w_uq=mla_w_uq, mla_w_ukv=mla_w_ukv, mla_q_gain=mla_q_gain, mla_k_gain=mla_k_gain,
             ssd_conv_w=ssd_conv_w, ssd_conv_b=ssd_conv_b, ssd_dt_bias=ssd_dt_bias, ssd_a_log=ssd_a_log,
             ssd_d=ssd_d, ssd_norm=ssd_norm, diff_q_gain=diff_q_gain, diff_k_gain=diff_k_gain,
             diff_lambda=diff_lambda, diff_subln=diff_subln, w_out=w_out, ffn_norm=ffn_norm,
             w_router=w_router, w_gate=w_gate, w_up=w_up, w_down=w_down)
    return _trunk(x_prompt, p), _trunk(x_sample, p)
```
